```python
import jax, jax.numpy as jnp
from jax import lax
import numpy as np

D_MODEL = 4096
BATCH = 4
SEQ = 2048
DEPTH = 2
DEC_BATCH = 8
DEC_SEQ = 8
PAST_LEN = 16384
PAGE_SIZE = 128

HEAD_DIM = 128
NSA_HEADS = 16
NSA_GROUPS = 2
CMP_BLOCK = 64
N_SEL = 16
WINDOW = 512
DSA_HEADS = 16
DSA_GROUPS = 2
IDX_HEADS = 32
IDX_DIM = 64
IDX_TOPK = 256
Q_BLOCK = 128
LN_EPS = 1e-5
ALPHA = (2 * DEPTH) ** 0.25
BETA = (8 * DEPTH) ** -0.25
NEG = -1e30
SEL_FORCE = 1e4

NSA_W = NSA_HEADS * HEAD_DIM
NSA_KV_W = NSA_GROUPS * HEAD_DIM
DSA_W = DSA_HEADS * HEAD_DIM
DSA_KV_W = DSA_GROUPS * HEAD_DIM
IN_SPLITS = (('nsa_q', NSA_W), ('nsa_kv', 6 * NSA_KV_W), ('nsa_gate', 3 * NSA_HEADS), ('nsa_z', NSA_W),
             ('dsa_q', DSA_W), ('dsa_kv', 2 * DSA_KV_W), ('idx_q', IDX_HEADS * IDX_DIM), ('idx_w', IDX_HEADS),
             ('idx_k', IDX_DIM), ('dsa_z', DSA_W), ('merge', 2 * D_MODEL))
IN_W = sum(w for _, w in IN_SPLITS)

kernel_name = 'nsa_dsa_gated_hybrid_step'


def alibi_slopes(n):
    return jnp.exp2(-8.0 * jnp.arange(1, n + 1, dtype=jnp.float32) / n)


def masked_softmax(s, mask):
    s = jnp.where(mask, s, NEG)
    e = jnp.exp(s - jnp.max(s, axis=-1, keepdims=True)) * mask
    return e / jnp.maximum(jnp.sum(e, axis=-1, keepdims=True), 1e-30)


def layer_norm(x, g, b):
    xf = x.astype(jnp.float32)
    mu = jnp.mean(xf, axis=-1, keepdims=True)
    var = jnp.mean(jnp.square(xf - mu), axis=-1, keepdims=True)
    return ((xf - mu) * lax.rsqrt(var + LN_EPS) * g + b).astype(x.dtype)


def query_block(T):
    return Q_BLOCK if T % Q_BLOCK == 0 else T


def compress_blocks(kv, pe, w1, w2):
    B, L, G, d = kv.shape
    nb = L // CMP_BLOCK
    blk = kv[:, :nb * CMP_BLOCK].reshape(B, nb, CMP_BLOCK, G, d) + pe[:, None, :]
    flat = blk.transpose(0, 1, 3, 2, 4).reshape(B, nb, G, CMP_BLOCK * d)
    return jax.nn.gelu(flat @ w1) @ w2


def nsa_attention(q, gates, kv_full, win_kv, win_p0, pe, w1, w2):
    B, T, H, d = q.shape
    L, G = kv_full.shape[1], kv_full.shape[3]
    R = H // G
    qb = query_block(T)
    n_qb = T // qb
    scale = d ** -0.5
    slopes = alibi_slopes(H).reshape(G, R)
    kc = compress_blocks(kv_full[:, :, 0], pe[0], w1[0], w2[0])
    vc = compress_blocks(kv_full[:, :, 1], pe[1], w1[1], w2[1])
    nb = kc.shape[1]
    blk_end = jnp.arange(nb) * CMP_BLOCK + (CMP_BLOCK - 1)
    nbs = -(-L // CMP_BLOCK)
    n_sel = min(N_SEL, nbs)
    sel_kv = jnp.pad(kv_full[:, :, 2:4], ((0, 0), (0, nbs * CMP_BLOCK - L), (0, 0), (0, 0), (0, 0)))
    sel_kv = sel_kv.reshape(B, nbs, CMP_BLOCK, 2, G, d).transpose(3, 0, 4, 1, 2, 5)
    k_blocks, v_blocks = sel_kv[0], sel_kv[1]
    win_kv_p = jnp.pad(win_kv, ((0, 0), (WINDOW, 0), (0, 0), (0, 0), (0, 0)))
    n_win = qb + WINDOW - 1
    b_idx = jnp.arange(B)[:, None, None, None]
    g_idx = jnp.arange(G)[None, :, None, None]
    j = jnp.arange(nbs)

    def block(args):
        i, qi, gi = args
        t = L - T + i * qb + jnp.arange(qb)
        s = jnp.einsum('bqgrd,bngd->bgrqn', qi, kc).astype(jnp.float32) * scale
        s = s - slopes[:, :, None, None] * (t[:, None] - blk_end[None, :]).astype(jnp.float32)
        p_cmp = masked_softmax(s, blk_end[None, :] <= t[:, None])
        o_cmp = jnp.einsum('bgrqn,bngd->bqgrd', p_cmp, vc)
        imp = jnp.pad(p_cmp.sum(axis=2), ((0, 0), (0, 0), (0, 0), (0, nbs - nb)))
        cur = t // CMP_BLOCK
        forced = (j[None, :] == 0) | (j[None, :] == cur[:, None]) | (j[None, :] == cur[:, None] - 1)
        score = jnp.where(j[None, :] <= cur[:, None], jnp.where(forced, SEL_FORCE, imp), -SEL_FORCE)
        _, sel = lax.top_k(score, n_sel)
        kb = k_blocks[b_idx, g_idx, sel]
        vb = v_blocks[b_idx, g_idx, sel]
        pos = sel[..., None] * CMP_BLOCK + jnp.arange(CMP_BLOCK)
        dist = t[None, None, :, None, None] - pos
        s = jnp.einsum('bqgrd,bgqnkd->bgrqnk', qi, kb).astype(jnp.float32) * scale
        s = s - slopes[None, :, :, None, None, None] * dist[:, :, None].astype(jnp.float32)
        m = n_sel * CMP_BLOCK
        p_sel = masked_softmax(s.reshape(B, G, R, qb, m), (dist >= 0)[:, :, None].reshape(B, G, 1, qb, m))
        o_sel = jnp.einsum('bgrqm,bgqmd->bqgrd', p_sel, vb.reshape(B, G, qb, m, d))
        start = t[0] - win_p0 + 1
        kw = lax.dynamic_slice_in_dim(win_kv_p, start, n_win, axis=1)
        pos_w = win_p0 - WINDOW + start + jnp.arange(n_win)
        dist_w = t[:, None] - pos_w[None, :]
        mask_w = (pos_w[None, :] >= win_p0) & (dist_w >= 0) & (dist_w < WINDOW)
        s = jnp.einsum('bqgrd,bkgd->bgrqk', qi, kw[:, :, 0]).astype(jnp.float32) * scale
        s = s - slopes[:, :, None, None] * dist_w.astype(jnp.float32)
        p_win = masked_softmax(s, mask_w)
        o_win = jnp.einsum('bgrqk,bkgd->bqgrd', p_win, kw[:, :, 1])
        o = gi[..., 0:1] * o_cmp + gi[..., 1:2] * o_sel + gi[..., 2:3] * o_win
        return o.astype(qi.dtype)

    qs = q.reshape(B, n_qb, qb, G, R, d).transpose(1, 0, 2, 3, 4, 5)
    gs = gates.reshape(B, n_qb, qb, G, R, 3).transpose(1, 0, 2, 3, 4, 5)
    o = lax.map(block, (jnp.arange(n_qb), qs, gs))
    return o.transpose(1, 0, 2, 3, 4, 5).reshape(B, T, H * d)


def dsa_attention(q, kv_full, q_idx, w_idx, k_idx):
    B, T, H, d = q.shape
    L, G = kv_full.shape[1], kv_full.shape[3]
    R = H // G
    qb = query_block(T)
    n_qb = T // qb
    top = min(IDX_TOPK, L // 4)
    scale = d ** -0.5
    slopes = alibi_slopes(H).reshape(G, R)
    b_idx = jnp.arange(B)[:, None, None]
    s_pos = jnp.arange(L)

    def block(args):
        i, qi, qix, wi = args
        t = L - T + i * qb + jnp.arange(qb)
        logits = jnp.einsum('bqhe,bse->bqhs', qix, k_idx).astype(jnp.float32) * IDX_DIM ** -0.5
        score = jnp.einsum('bqh,bqhs->bqs', wi.astype(jnp.float32) * IDX_HEADS ** -0.5, jax.nn.relu(logits))
        score = jnp.where(s_pos[None, None, :] <= t[None, :, None], score, NEG)
        _, sel = lax.top_k(score, top)
        kv = kv_full[b_idx, sel]
        dist = t[None, :, None] - sel
        s = jnp.einsum('bqgrd,bqkgd->bgrqk', qi, kv[:, :, :, 0]).astype(jnp.float32) * scale
        s = s - slopes[None, :, :, None, None] * dist[:, None, None].astype(jnp.float32)
        p = masked_softmax(s, (dist >= 0)[:, None, None])
        o = jnp.einsum('bgrqk,bqkgd->bqgrd', p, kv[:, :, :, 1])
        return o.astype(qi.dtype)

    qs = q.reshape(B, n_qb, qb, G, R, d).transpose(1, 0, 2, 3, 4, 5)
    qis = q_idx.reshape(B, n_qb, qb, IDX_HEADS, IDX_DIM).transpose(1, 0, 2, 3, 4)
    wis = w_idx.reshape(B, n_qb, qb, IDX_HEADS).transpose(1, 0, 2, 3)
    o = lax.map(block, (jnp.arange(n_qb), qs, qis, wis))
    return o.transpose(1, 0, 2, 3, 4, 5).reshape(B, T, H * d)


def mixer_layer(x, past, w_in, cmp_pe, cmp_w1, cmp_w2, w_proj_nsa, w_proj_dsa, w_out, ln_gain, ln_bias):
    B, T, _ = x.shape
    points = [int(p) for p in np.cumsum([w for _, w in IN_SPLITS])[:-1]]
    nsa_q, nsa_kv, nsa_gate, nsa_z, dsa_q, dsa_kv, idx_q, idx_w, idx_k, dsa_z, merge = jnp.split(x @ w_in, points, axis=-1)
    nsa_q = nsa_q.reshape(B, T, NSA_HEADS, HEAD_DIM)
    nsa_kv = nsa_kv.reshape(B, T, 6, NSA_GROUPS, HEAD_DIM)
    new_nsa_kv, new_win = nsa_kv[:, :, :4], nsa_kv[:, :, 4:]
    nsa_gate = jax.nn.sigmoid(nsa_gate).reshape(B, T, NSA_HEADS, 3)
    dsa_q = dsa_q.reshape(B, T, DSA_HEADS, HEAD_DIM)
    dsa_kv = dsa_kv.reshape(B, T, 2, DSA_GROUPS, HEAD_DIM)
    idx_q = idx_q.reshape(B, T, IDX_HEADS, IDX_DIM)
    merge = jax.nn.sigmoid(merge).reshape(B, T, 2, D_MODEL)
    if past is None:
        nsa_full, win_keys, win_p0, dsa_full, idx_full = new_nsa_kv, new_win, 0, dsa_kv, idx_k
    else:
        past_nsa, past_win, past_dsa, past_idx = past
        past_len = past_nsa.shape[1]
        nsa_full = jnp.concatenate([past_nsa, new_nsa_kv], axis=1)
        win_keys = jnp.concatenate([past_win, new_win], axis=1)
        win_p0 = past_len - past_win.shape[1]
        dsa_full = jnp.concatenate([past_dsa, dsa_kv], axis=1)
        idx_full = jnp.concatenate([past_idx, idx_k], axis=1)
    o_a = nsa_attention(nsa_q, nsa_gate, nsa_full, win_keys, win_p0, cmp_pe, cmp_w1, cmp_w2)
    o_b = dsa_attention(dsa_q, dsa_full, idx_q, idx_w, idx_full)
    a = (o_a * jax.nn.silu(nsa_z)) @ w_proj_nsa
    b = (o_b * jax.nn.silu(dsa_z)) @ w_proj_dsa
    h = (merge[:, :, 0] * a + merge[:, :, 1] * b) @ w_out
    y = layer_norm(ALPHA * x + h, ln_gain, ln_bias)
    keep = min(WINDOW, win_keys.shape[1])
    return y, (new_nsa_kv, win_keys[:, win_keys.shape[1] - keep:], dsa_kv, idx_k)


def setup_inputs(seed: int = 0) -> dict:
    key = jax.random.key(seed)
    ks = jax.random.split(key, 16)
    n_pages = PAST_LEN // PAGE_SIZE
    n_used = DEC_BATCH * n_pages
    n_pool = n_used + -(-n_used // 4)
    wbuf = min(WINDOW, PAST_LEN)
    nrm = jax.random.normal
    segs = []
    for name, width in IN_SPLITS:
        if name == 'nsa_kv':
            segs.append(jnp.tile(jnp.repeat(jnp.array([1.0, BETA], jnp.float32), NSA_KV_W), 3))
        elif name == 'dsa_kv':
            segs.append(jnp.repeat(jnp.array([1.0, BETA], jnp.float32), DSA_KV_W))
        else:
            segs.append(jnp.ones((width,), jnp.float32))
    col_scale = jnp.concatenate(segs)
    return {
        'x_prompt': nrm(ks[0], (BATCH, SEQ, D_MODEL), jnp.float32),
        'x_sample': nrm(ks[1], (DEC_BATCH, DEC_SEQ, D_MODEL), jnp.float32),
        'cache_nsa_kv': nrm(ks[2], (DEPTH, n_pool, PAGE_SIZE, 4, NSA_GROUPS, HEAD_DIM), jnp.float32),
        'state_nsa_win': nrm(ks[3], (DEPTH, DEC_BATCH, wbuf, 2, NSA_GROUPS, HEAD_DIM), jnp.float32),
        'cache_dsa_kv': nrm(ks[4], (DEPTH, n_pool, PAGE_SIZE, 2, DSA_GROUPS, HEAD_DIM), jnp.float32),
        'cache_dsa_idx': nrm(ks[5], (DEPTH, n_pool, PAGE_SIZE, IDX_DIM), jnp.float32),
        'page_table': jax.random.permutation(ks[6], n_pool)[:n_used].reshape(DEC_BATCH, n_pages).astype(jnp.int32),
        'w_in': nrm(ks[7], (DEPTH, D_MODEL, IN_W), jnp.float32) * (D_MODEL ** -0.5) * col_scale,
        'cmp_pe': 0.1 * nrm(ks[8], (DEPTH, 2, CMP_BLOCK, HEAD_DIM), jnp.float32),
        'cmp_w1': nrm(ks[9], (DEPTH, 2, CMP_BLOCK * HEAD_DIM, HEAD_DIM), jnp.float32) * (CMP_BLOCK * HEAD_DIM) ** -0.5,
        'cmp_w2': nrm(ks[10], (DEPTH, 2, HEAD_DIM, HEAD_DIM), jnp.float32) * HEAD_DIM ** -0.5,
        'w_proj_nsa': nrm(ks[11], (DEPTH, NSA_W, D_MODEL), jnp.float32) * (NSA_W ** -0.5) * BETA,
        'w_proj_dsa': nrm(ks[12], (DEPTH, DSA_W, D_MODEL), jnp.float32) * (DSA_W ** -0.5) * BETA,
        'w_out': nrm(ks[13], (DEPTH, D_MODEL, D_MODEL), jnp.float32) * (D_MODEL ** -0.5) * BETA,
        'ln_gain': 1.0 + 0.02 * nrm(ks[14], (DEPTH, D_MODEL), jnp.float32),
        'ln_bias': 0.02 * nrm(ks[15], (DEPTH, D_MODEL), jnp.float32),
    }


def reference(x_prompt, x_sample, cache_nsa_kv, state_nsa_win, cache_dsa_kv, cache_dsa_idx, page_table,
              w_in, cmp_pe, cmp_w1, cmp_w2, w_proj_nsa, w_proj_dsa, w_out, ln_gain, ln_bias):
    n_pages = page_table.shape[1]

    def paged(pool):
        rows = pool[page_table]
        return rows.reshape((rows.shape[0], n_pages * PAGE_SIZE) + rows.shape[3:])

    y_p, y_s = x_prompt, x_sample
    nsa_p, nsa_s, win_p, win_s, dkv_p, dkv_s, idx_p, idx_s = [], [], [], [], [], [], [], []
    for l in range(DEPTH):
        w = (w_in[l], cmp_pe[l], cmp_w1[l], cmp_w2[l], w_proj_nsa[l], w_proj_dsa[l], w_out[l], ln_gain[l], ln_bias[l])
        y_p, st_p = mixer_layer(y_p, None, *w)
        past = (paged(cache_nsa_kv[l]), state_nsa_win[l], paged(cache_dsa_kv[l]), paged(cache_dsa_idx[l]))
        y_s, st_s = mixer_layer(y_s, past, *w)
        nsa_p.append(st_p[0]); win_p.append(st_p[1]); dkv_p.append(st_p[2]); idx_p.append(st_p[3])
        nsa_s.append(st_s[0]); win_s.append(st_s[1]); dkv_s.append(st_s[2]); idx_s.append(st_s[3])
    return (y_p, y_s, jnp.stack(nsa_p), jnp.stack(nsa_s), jnp.stack(win_p), jnp.stack(win_s),
            jnp.stack(dkv_p), jnp.stack(dkv_s), jnp.stack(idx_p), jnp.stack(idx_s))
```

```python
import functools

import jax
import jax.numpy as jnp
from jax import lax
from jax.experimental import pallas as pl
from jax.experimental.pallas import tpu as pltpu

D_MODEL = 4096
DEPTH = 2
PAGE = 128
HD = 128
NSA_H = 16
DSA_H = 16
GROUPS = 2
RH = NSA_H // GROUPS
CMP = 64
CMP_SHIFT = 6
N_SEL = 16
WINDOW = 512
IDX_H = 32
IDX_D = 64
IDX_TOPK = 256
LN_EPS = 1e-5
ALPHA = (2 * DEPTH) ** 0.25
NEG = -1e30
SEL_FORCE = 1e4
SCALE = HD ** -0.5
INT_MIN = -(2 ** 31)

IN_SPLITS = (('nsa_q', 2048), ('nsa_kv', 1536), ('nsa_gate', 48), ('nsa_z', 2048), ('dsa_q', 2048),
             ('dsa_kv', 512), ('idx_q', 2048), ('idx_w', 32), ('idx_k', 64), ('dsa_z', 2048), ('merge', 8192))

C_QN, C_ZN, C_QD, C_ZD, C_QI, C_MG, C_KVN, C_KVD, C_SM = 0, 2048, 4096, 6144, 8192, 10240, 18432, 19968, 20480
PROJ_W = 20992
SM_IDXW = 64
LANE = 128

VMEM_LIMIT = 48 * 1024 * 1024
BF = jnp.bfloat16
F32 = jnp.float32


def _cp(sem):
    return pltpu.CompilerParams(dimension_semantics=sem, vmem_limit_bytes=VMEM_LIMIT)


def _dot_nt(a, b):
    return lax.dot_general(a, b, (((1,), (1,)), ((), ())), preferred_element_type=F32)


def _dot(a, b):
    return jnp.dot(a, b, preferred_element_type=F32)


def _mm_kernel(x_ref, w_ref, o_ref):
    o_ref[...] = _dot(x_ref[...], w_ref[...]).astype(o_ref.dtype)


def _matmul(x, w, tm, tn, out_dtype=F32):
    m, k = x.shape
    n = w.shape[1]
    return pl.pallas_call(
        _mm_kernel, grid=(m // tm, n // tn),
        in_specs=[pl.BlockSpec((tm, k), lambda i, j: (i, 0)), pl.BlockSpec((k, tn), lambda i, j: (0, j))],
        out_specs=pl.BlockSpec((tm, tn), lambda i, j: (i, j)),
        out_shape=jax.ShapeDtypeStruct((m, n), out_dtype),
        compiler_params=_cp(("parallel", "parallel")), name="matmul")(x, w)


def _gated_mm_kernel(a_ref, b_ref, wa_ref, wb_ref, m0_ref, m1_ref, o_ref):
    a = _dot(a_ref[...], wa_ref[...])
    b = _dot(b_ref[...], wb_ref[...])
    o_ref[...] = (jax.nn.sigmoid(m0_ref[...]) * a + jax.nn.sigmoid(m1_ref[...]) * b).astype(o_ref.dtype)


def _gated_matmul(a_in, b_in, wa, wb, proj, tm, tn):
    m, k = a_in.shape
    n = wa.shape[1]
    c0, c1 = C_MG // tn, (C_MG + D_MODEL) // tn
    return pl.pallas_call(
        _gated_mm_kernel, grid=(m // tm, n // tn),
        in_specs=[pl.BlockSpec((tm, k), lambda i, j: (i, 0)), pl.BlockSpec((tm, k), lambda i, j: (i, 0)),
                  pl.BlockSpec((k, tn), lambda i, j: (0, j)), pl.BlockSpec((k, tn), lambda i, j: (0, j)),
                  pl.BlockSpec((tm, tn), lambda i, j: (i, c0 + j)), pl.BlockSpec((tm, tn), lambda i, j: (i, c1 + j))],
        out_specs=pl.BlockSpec((tm, tn), lambda i, j: (i, j)),
        out_shape=jax.ShapeDtypeStruct((m, n), BF),
        compiler_params=_cp(("parallel", "parallel")), name="gated_matmul")(a_in, b_in, wa, wb, proj, proj)


def _ln_kernel(x_ref, h_ref, g_ref, b_ref, y_ref, yb_ref):
    v = ALPHA * x_ref[...] + h_ref[...]
    mu = jnp.mean(v, axis=-1, keepdims=True)
    c = v - mu
    var = jnp.mean(c * c, axis=-1, keepdims=True)
    y = c * lax.rsqrt(var + LN_EPS) * g_ref[...] + b_ref[...]
    y_ref[...] = y
    yb_ref[...] = y.astype(BF)


def _residual_ln(x, h, gain, bias, tm):
    m, n = x.shape
    row = pl.BlockSpec((tm, n), lambda i: (i, 0))
    vec = pl.BlockSpec((1, n), lambda i: (0, 0))
    return pl.pallas_call(
        _ln_kernel, grid=(m // tm,), in_specs=[row, row, vec, vec], out_specs=[row, row],
        out_shape=[jax.ShapeDtypeStruct((m, n), F32), jax.ShapeDtypeStruct((m, n), BF)],
        compiler_params=_cp(("parallel",)), name="residual_ln")(x, h, gain.reshape(1, n), bias.reshape(1, n))


def _gelu_tanh(x):
    return 0.5 * x * (1.0 + jnp.tanh(0.7978845608028654 * (x + 0.044715 * (x * x * x))))


def _compress_kernel(pt_ref, x_ref, pe_ref, w1_ref, w2_ref, o_ref, slab_ref, *, pg):
    p = pl.program_id(2)
    for cg in range(2 * GROUPS):
        slab_ref[cg, pl.ds(pl.multiple_of(p * PAGE, PAGE), PAGE), :] = (
            x_ref[:, cg * HD:(cg + 1) * HD] + pe_ref[:, cg * HD:(cg + 1) * HD])

    @pl.when(p == pg - 1)
    def _():
        nblk = pg * (PAGE // CMP)
        for c in range(2):
            acc = jnp.zeros((GROUPS * nblk, HD), F32)
            for tok in range(CMP):
                parts = [slab_ref[c * GROUPS + g, pl.ds(tok, nblk, stride=CMP), :] for g in range(GROUPS)]
                lhs = jnp.concatenate(parts, axis=0).astype(BF)
                acc = acc + _dot(lhs, w1_ref[c, tok])
            out = _dot(_gelu_tanh(acc).astype(BF), w2_ref[c])
            for g in range(GROUPS):
                o_ref[c, g] = out[g * nblk:(g + 1) * nblk]


def _compress(src2d, table, col_blk, pe_page, w1, w2, pg):
    b, n_pages = table.shape
    ns = n_pages // pg
    nblk = pg * (PAGE // CMP)
    kern = functools.partial(_compress_kernel, pg=pg)
    gs = pltpu.PrefetchScalarGridSpec(
        num_scalar_prefetch=1, grid=(b, ns, pg),
        in_specs=[pl.BlockSpec((PAGE, 4 * HD), lambda bi, s, p, pt: (pt[bi, s * pg + p], col_blk)),
                  pl.BlockSpec((PAGE, 4 * HD), lambda bi, s, p, pt: (0, 0)),
                  pl.BlockSpec((2, CMP, HD, HD), lambda bi, s, p, pt: (0, 0, 0, 0)),
                  pl.BlockSpec((2, HD, HD), lambda bi, s, p, pt: (0, 0, 0))],
        out_specs=pl.BlockSpec((None, 2, GROUPS, nblk, HD), lambda bi, s, p, pt: (bi, 0, 0, s, 0)),
        scratch_shapes=[pltpu.VMEM((2 * GROUPS, pg * PAGE, HD), F32)])
    return pl.pallas_call(
        kern, grid_spec=gs, out_shape=jax.ShapeDtypeStruct((b, 2, GROUPS, ns * nblk, HD), F32),
        compiler_params=_cp(("parallel", "arbitrary", "arbitrary")), name="compress")(table, src2d, pe_page, w1, w2)


def _sortable(x):
    b = lax.bitcast_convert_type(x, jnp.int32)
    return jnp.where(b < 0, b ^ jnp.int32(0x7FFFFFFF), b)


def _count(pred):
    return jnp.sum(jnp.where(pred, 1.0, 0.0), axis=-1, keepdims=True)


def _topk_mask(key_ref, k, idx, idx_bits):
    rows = key_ref.shape[0]
    kf = float(k)
    zero = jnp.zeros((rows, 1), jnp.int32)
    t0 = jnp.where(_count(key_ref[...] >= zero) >= kf, zero, jnp.full((rows, 1), INT_MIN, jnp.int32))

    def value_bit(i, t):
        cand = t | jnp.left_shift(jnp.int32(1), jnp.int32(30) - i)
        return jnp.where(_count(key_ref[...] >= cand) >= kf, cand, t)

    t = lax.fori_loop(0, 31, value_bit, t0)
    keys = key_ref[...]
    need = kf - _count(keys > t)

    def index_bit(i, c):
        cand = c | jnp.left_shift(jnp.int32(1), jnp.int32(idx_bits - 1) - i)
        below = _count((key_ref[...] == t) & (idx < cand))
        return jnp.where(below < need, cand, c)

    c0 = lax.fori_loop(0, idx_bits, index_bit, zero)
    return (keys > t) | ((keys == t) & (idx <= c0))


def _masked_softmax(s, mask):
    s = jnp.where(mask, s, NEG)
    e = jnp.where(mask, jnp.exp(s - jnp.max(s, axis=-1, keepdims=True)), 0.0)
    return e / jnp.maximum(jnp.sum(e, axis=-1, keepdims=True), 1e-30)


def _nsa_select_kernel(slopes_ref, q_ref, kc_ref, vc_ref, ocmp_ref, sel_ref, key_scr, *, qb, nb, nbs, nbs_pad, t0):
    g = pl.program_id(1)
    i = pl.program_id(2)
    tq = t0 + i * qb + lax.broadcasted_iota(jnp.int32, (qb, 1), 0)
    blk_end = lax.broadcasted_iota(jnp.int32, (1, nb), 1) * CMP + (CMP - 1)
    valid = blk_end <= tq
    distf = (tq - blk_end).astype(F32)
    kc = kc_ref[...].astype(BF)
    vc = vc_ref[...].astype(BF)
    imp = jnp.zeros((qb, nb), F32)
    for r in range(RH):
        q = q_ref[:, r * HD:(r + 1) * HD].astype(BF)
        s = _dot_nt(q, kc) * SCALE - slopes_ref[g * RH + r] * distf
        p = _masked_softmax(s, valid)
        ocmp_ref[:, r * HD:(r + 1) * HD] = _dot(p.astype(BF), vc)
        imp = imp + p
    if nbs_pad > nb:
        imp = jnp.concatenate([imp, jnp.zeros((qb, nbs_pad - nb), F32)], axis=1)
    j = lax.broadcasted_iota(jnp.int32, (1, nbs_pad), 1)
    cur = jnp.right_shift(tq, CMP_SHIFT)
    forced = (j == 0) | (j == cur) | (j == cur - 1)
    score = jnp.where(j <= cur, jnp.where(forced, SEL_FORCE, imp), -SEL_FORCE)
    key_scr[...] = _sortable(score)
    sel = _topk_mask(key_scr, min(N_SEL, nbs), j, max(1, (nbs_pad - 1).bit_length()))
    sel_ref[...] = jnp.where(sel, 1.0, 0.0).astype(BF)


def _nsa_select(proj, kcvc, slopes, bsz, t, nbs, t0):
    qb = min(t, LANE)
    nq = t // qb
    nb = kcvc.shape[3]
    nbs_pad = nbs if nbs % LANE == 0 or nbs == nb else -(-nbs // LANE) * LANE
    kern = functools.partial(_nsa_select_kernel, qb=qb, nb=nb, nbs=nbs, nbs_pad=nbs_pad, t0=t0)
    wq = RH * HD
    return pl.pallas_call(
        kern, grid=(bsz, GROUPS, nq),
        in_specs=[pl.BlockSpec(memory_space=pltpu.SMEM),
                  pl.BlockSpec((qb, wq), lambda b, g, i: (b * nq + i, C_QN // wq + g)),
                  pl.BlockSpec((None, None, None, nb, HD), lambda b, g, i: (b, 0, g, 0, 0)),
                  pl.BlockSpec((None, None, None, nb, HD), lambda b, g, i: (b, 1, g, 0, 0))],
        out_specs=[pl.BlockSpec((qb, wq), lambda b, g, i: (b * nq + i, g)),
                   pl.BlockSpec((None, None, qb, nbs_pad), lambda b, g, i: (b, g, i, 0))],
        out_shape=[jax.ShapeDtypeStruct((bsz * t, NSA_H * HD), F32),
                   jax.ShapeDtypeStruct((bsz, GROUPS, t, nbs_pad), BF)],
        scratch_shapes=[pltpu.VMEM((qb, nbs_pad), jnp.int32)],
        compiler_params=_cp(("parallel", "parallel", "parallel")), name="nsa_select")(slopes, proj, kcvc, kcvc)


def _fa_init(m_scr, l_scr, acc_scr):
    m_scr[...] = jnp.full(m_scr.shape, NEG, F32)
    l_scr[...] = jnp.zeros(l_scr.shape, F32)
    acc_scr[...] = jnp.zeros(acc_scr.shape, F32)


def _fa_step(q_ref, k, v, ok, distf, slopes_ref, g, m_scr, l_scr, acc_scr):
    for r in range(RH):
        q = q_ref[:, r * HD:(r + 1) * HD].astype(BF)
        s = _dot_nt(q, k) * SCALE - slopes_ref[g * RH + r] * distf
        s = jnp.where(ok, s, NEG)
        m_prev = m_scr[r]
        m_new = jnp.maximum(m_prev, jnp.max(s, axis=-1, keepdims=True))
        a = jnp.exp(m_prev - m_new)
        p = jnp.where(ok, jnp.exp(s - m_new), 0.0)
        l_scr[r] = a * l_scr[r] + jnp.sum(p, axis=-1, keepdims=True)
        acc_scr[r] = a * acc_scr[r] + _dot(p.astype(BF), v)
        m_scr[r] = m_new


def _fa_finish(o_ref, l_scr, acc_scr):
    for r in range(RH):
        o_ref[:, r * HD:(r + 1) * HD] = acc_scr[r] / jnp.maximum(l_scr[r], 1e-30)


def _block_mask(mode, mask_val, kpos, nbs_pad):
    if mode == "sel":
        n = lax.broadcasted_iota(jnp.int32, (nbs_pad, 1), 0)
        expand = jnp.where(n == jnp.right_shift(kpos, CMP_SHIFT), 1.0, 0.0).astype(BF)
        return _dot(mask_val, expand) > 0.5
    if mode == "dsa":
        return mask_val > 0.5
    return None


def _fa_prompt_kernel(*refs, mode, qb, nbs_pad):
    if mode == "win":
        slopes_ref, q_ref, k_ref, v_ref, o_ref, m_scr, l_scr, acc_scr = refs
        mask_ref = None
    else:
        slopes_ref, q_ref, k_ref, v_ref, mask_ref, o_ref, m_scr, l_scr, acc_scr = refs
    g = pl.program_id(1)
    i = pl.program_id(2)
    _fa_init(m_scr, l_scr, acc_scr)
    tq = i * qb + lax.broadcasted_iota(jnp.int32, (qb, 1), 0)
    j_lo = jnp.maximum(i - WINDOW // qb, 0) if mode == "win" else 0

    def body(j, carry):
        off = pl.multiple_of(j * qb, qb)
        k = k_ref[pl.ds(off, qb), :].astype(BF)
        v = v_ref[pl.ds(off, qb), :].astype(BF)
        kpos = j * qb + lax.broadcasted_iota(jnp.int32, (1, qb), 1)
        dist = tq - kpos
        ok = dist >= 0
        if mode == "win":
            ok = ok & (dist < WINDOW)
        elif mode == "sel":
            ok = ok & _block_mask(mode, mask_ref[...], kpos, nbs_pad)
        else:
            ok = ok & _block_mask(mode, mask_ref[:, pl.ds(off, qb)], kpos, nbs_pad)
        _fa_step(q_ref, k, v, ok, dist.astype(F32), slopes_ref, g, m_scr, l_scr, acc_scr)
        return carry

    lax.fori_loop(j_lo, i + 1, body, 0)
    _fa_finish(o_ref, l_scr, acc_scr)


def _fa_prompt(mode, proj, slopes, bsz, t, q_col, k_col, v_col, mask=None):
    qb = LANE
    nq = t // qb
    wq = RH * HD
    nbs_pad = mask.shape[-1] if mode == "sel" else 0
    in_specs = [pl.BlockSpec(memory_space=pltpu.SMEM),
                pl.BlockSpec((qb, wq), lambda b, g, i: (b * nq + i, q_col // wq + g)),
                pl.BlockSpec((t, HD), lambda b, g, i: (b, k_col // HD + g)),
                pl.BlockSpec((t, HD), lambda b, g, i: (b, v_col // HD + g))]
    args = [slopes, proj, proj, proj]
    if mode == "sel":
        in_specs.append(pl.BlockSpec((None, None, qb, nbs_pad), lambda b, g, i: (b, g, i, 0)))
        args.append(mask)
    elif mode == "dsa":
        in_specs.append(pl.BlockSpec((None, qb, t), lambda b, g, i: (b, i, 0)))
        args.append(mask)
    kern = functools.partial(_fa_prompt_kernel, mode=mode, qb=qb, nbs_pad=nbs_pad)
    return pl.pallas_call(
        kern, grid=(bsz, GROUPS, nq), in_specs=in_specs,
        out_specs=pl.BlockSpec((qb, wq), lambda b, g, i: (b * nq + i, g)),
        out_shape=jax.ShapeDtypeStruct((bsz * t, GROUPS * wq), F32),
        scratch_shapes=[pltpu.VMEM((RH, qb, 1), F32), pltpu.VMEM((RH, qb, 1), F32), pltpu.VMEM((RH, qb, HD), F32)],
        compiler_params=_cp(("parallel", "parallel", "parallel")), name="fa_prompt_" + mode)(*args)


def _fa_sample_kernel(*refs, mode, t, nkb, t0, kpos0, nbs_pad):
    if mode == "win":
        pt_ref, slopes_ref, q_ref, k_ref, v_ref, kn_ref, vn_ref, o_ref, m_scr, l_scr, acc_scr = refs
        mask_ref = maskn_ref = None
    elif mode == "sel":
        pt_ref, slopes_ref, q_ref, k_ref, v_ref, kn_ref, vn_ref, mask_ref, o_ref, m_scr, l_scr, acc_scr = refs
        maskn_ref = mask_ref
    else:
        pt_ref, slopes_ref, q_ref, k_ref, v_ref, kn_ref, vn_ref, mask_ref, maskn_ref, o_ref, m_scr, l_scr, acc_scr = refs
    g = pl.program_id(1)
    j = pl.program_id(2)

    @pl.when(j == 0)
    def _():
        _fa_init(m_scr, l_scr, acc_scr)

    tq = t0 + lax.broadcasted_iota(jnp.int32, (t, 1), 0)
    lane = lax.broadcasted_iota(jnp.int32, (1, PAGE), 1)

    def step(k, v, kpos, extra_ok, mref):
        dist = tq - kpos
        ok = dist >= 0
        if extra_ok is not None:
            ok = ok & extra_ok
        if mode == "win":
            ok = ok & (dist < WINDOW)
        else:
            ok = ok & _block_mask(mode, mref[...], kpos, nbs_pad)
        _fa_step(q_ref, k, v, ok, dist.astype(F32), slopes_ref, g, m_scr, l_scr, acc_scr)

    step(k_ref[...].astype(BF), v_ref[...].astype(BF), kpos0 + j * PAGE + lane, None, mask_ref)

    @pl.when(j == nkb - 1)
    def _():
        pad = jnp.zeros((PAGE - t, HD), F32)
        kn = jnp.concatenate([kn_ref[...], pad], axis=0).astype(BF)
        vn = jnp.concatenate([vn_ref[...], pad], axis=0).astype(BF)
        step(kn, vn, t0 + lane, lane < t, maskn_ref)
        _fa_finish(o_ref, l_scr, acc_scr)


def _fa_sample(mode, proj, past2d, table, slopes, t, t0, kpos0, q_col, kp_blk, vp_blk, kn_col, vn_col, mask=None):
    bsz, nkb = table.shape
    wq = RH * HD
    nbs_pad = mask.shape[-1] if mode == "sel" else 0
    in_specs = [pl.BlockSpec(memory_space=pltpu.SMEM),
                pl.BlockSpec((t, wq), lambda b, g, j, pt: (b, q_col // wq + g)),
                pl.BlockSpec((PAGE, HD), lambda b, g, j, pt: (pt[b, j], kp_blk + g)),
                pl.BlockSpec((PAGE, HD), lambda b, g, j, pt: (pt[b, j], vp_blk + g)),
                pl.BlockSpec((t, HD), lambda b, g, j, pt: (b, kn_col // HD + g)),
                pl.BlockSpec((t, HD), lambda b, g, j, pt: (b, vn_col // HD + g))]
    args = [slopes, proj, past2d, past2d, proj, proj]
    if mode == "sel":
        in_specs.append(pl.BlockSpec((None, None, t, nbs_pad), lambda b, g, j, pt: (b, g, 0, 0)))
        args.append(mask)
    elif mode == "dsa":
        in_specs.append(pl.BlockSpec((None, t, PAGE), lambda b, g, j, pt: (b, 0, j)))
        in_specs.append(pl.BlockSpec((None, t, PAGE), lambda b, g, j, pt: (b, 0, nkb)))
        args += [mask, mask]
    kern = functools.partial(_fa_sample_kernel, mode=mode, t=t, nkb=nkb, t0=t0, kpos0=kpos0, nbs_pad=nbs_pad)
    gs = pltpu.PrefetchScalarGridSpec(
        num_scalar_prefetch=1, grid=(bsz, GROUPS, nkb), in_specs=in_specs,
        out_specs=pl.BlockSpec((t, wq), lambda b, g, j, pt: (b, g)),
        scratch_shapes=[pltpu.VMEM((RH, t, 1), F32), pltpu.VMEM((RH, t, 1), F32), pltpu.VMEM((RH, t, HD), F32)])
    return pl.pallas_call(
        kern, grid_spec=gs, out_shape=jax.ShapeDtypeStruct((bsz * t, GROUPS * wq), F32),
        compiler_params=_cp(("parallel", "parallel", "arbitrary")), name="fa_sample_" + mode)(table, *args)


IDX_CHUNK = 256


def _dsa_index_prompt_kernel(q_ref, sq_ref, sk_ref, mask_ref, score_scr, key_scr, *, qb, t, top):
    i = pl.program_id(1)
    tq = i * qb + lax.broadcasted_iota(jnp.int32, (qb, 1), 0)
    w = sq_ref[:, SM_IDXW:SM_IDXW + IDX_H] * (IDX_H ** -0.5)
    score_scr[...] = jnp.zeros(score_scr.shape, F32)
    ch = min(IDX_CHUNK, t)

    def chunk(c, carry):
        off = pl.multiple_of(c * ch, ch)
        kk = sk_ref[pl.ds(off, ch), 0:IDX_D].astype(BF)
        acc = jnp.zeros((qb, ch), F32)
        for h in range(IDX_H):
            qh = q_ref[:, h * IDX_D:(h + 1) * IDX_D].astype(BF)
            acc = acc + w[:, h:h + 1] * jnp.maximum(_dot_nt(qh, kk) * (IDX_D ** -0.5), 0.0)
        score_scr[:, pl.ds(off, ch)] = acc
        return carry

    lax.fori_loop(0, ((i + 1) * qb + ch - 1) // ch, chunk, 0)
    s_pos = lax.broadcasted_iota(jnp.int32, (1, t), 1)
    causal = s_pos <= tq
    key_scr[...] = _sortable(jnp.where(causal, score_scr[...], NEG))
    sel = _topk_mask(key_scr, top, s_pos, max(1, (t - 1).bit_length()))
    mask_ref[...] = jnp.where(sel & causal, 1.0, 0.0).astype(BF)


def _dsa_index_prompt(proj, bsz, t):
    qb = LANE
    nq = t // qb
    top = min(IDX_TOPK, t // 4)
    kern = functools.partial(_dsa_index_prompt_kernel, qb=qb, t=t, top=top)
    wq = IDX_H * IDX_D
    return pl.pallas_call(
        kern, grid=(bsz, nq),
        in_specs=[pl.BlockSpec((qb, wq), lambda b, i: (b * nq + i, C_QI // wq)),
                  pl.BlockSpec((qb, LANE), lambda b, i: (b * nq + i, C_SM // LANE)),
                  pl.BlockSpec((t, LANE), lambda b, i: (b, C_SM // LANE))],
        out_specs=pl.BlockSpec((None, qb, t), lambda b, i: (b, i, 0)),
        out_shape=jax.ShapeDtypeStruct((bsz, t, t), BF),
        scratch_shapes=[pltpu.VMEM((qb, t), F32), pltpu.VMEM((qb, t), jnp.int32)],
        compiler_params=_cp(("parallel", "parallel")), name="dsa_index_prompt")(proj, proj, proj)


def _dsa_index_sample_kernel(pt_ref, qf_ref, wcol_ref, kp_ref, sn_ref, mask_ref, score_scr, key_scr,
                             *, t, nkb, t0, lpad, top):
    j = pl.program_id(1)

    @pl.when(j == 0)
    def _():
        score_scr[...] = jnp.full(score_scr.shape, NEG, F32)

    qf = qf_ref[...].astype(BF)
    wcol = wcol_ref[...] * (IDX_H ** -0.5)

    def scores(kk):
        r = jnp.maximum(_dot_nt(qf, kk) * (IDX_D ** -0.5), 0.0) * wcol
        sc = r[0:t]
        for h in range(1, IDX_H):
            sc = sc + r[h * t:(h + 1) * t]
        return sc

    score_scr[:, pl.ds(pl.multiple_of(j * PAGE, PAGE), PAGE)] = scores(kp_ref[...].astype(BF))

    @pl.when(j == nkb - 1)
    def _():
        tq = t0 + lax.broadcasted_iota(jnp.int32, (t, 1), 0)
        lane = lax.broadcasted_iota(jnp.int32, (1, PAGE), 1)
        kn = jnp.concatenate([sn_ref[:, 0:IDX_D], jnp.zeros((PAGE - t, IDX_D), F32)], axis=0).astype(BF)
        ok = (lane < t) & (t0 + lane <= tq)
        score_scr[:, nkb * PAGE:(nkb + 1) * PAGE] = jnp.where(ok, scores(kn), NEG)
        s_pos = lax.broadcasted_iota(jnp.int32, (1, lpad), 1)
        key_scr[...] = _sortable(score_scr[...])
        sel = _topk_mask(key_scr, top, s_pos, max(1, (lpad - 1).bit_length()))
        mask_ref[...] = jnp.where(sel & (s_pos <= tq) & (s_pos < t0 + t), 1.0, 0.0).astype(BF)


def _dsa_index_sample(qf, wcol, idx2d, table, proj, t, t0):
    bsz, nkb = table.shape
    lpad = (nkb + 1) * PAGE
    top = min(IDX_TOPK, (t0 + t) // 4)
    rows = IDX_H * t
    kern = functools.partial(_dsa_index_sample_kernel, t=t, nkb=nkb, t0=t0, lpad=lpad, top=top)
    gs = pltpu.PrefetchScalarGridSpec(
        num_scalar_prefetch=1, grid=(bsz, nkb),
        in_specs=[pl.BlockSpec((rows, IDX_D), lambda b, j, pt: (b, 0)),
                  pl.BlockSpec((rows, 1), lambda b, j, pt: (b, 0)),
                  pl.BlockSpec((PAGE, IDX_D), lambda b, j, pt: (pt[b, j], 0)),
                  pl.BlockSpec((t, LANE), lambda b, j, pt: (b, C_SM // LANE))],
        out_specs=pl.BlockSpec((None, t, lpad), lambda b, j, pt: (b, 0, 0)),
        scratch_shapes=[pltpu.VMEM((t, lpad), F32), pltpu.VMEM((t, lpad), jnp.int32)])
    return pl.pallas_call(
        kern, grid_spec=gs, out_shape=jax.ShapeDtypeStruct((bsz, t, lpad), BF),
        compiler_params=_cp(("parallel", "arbitrary")), name="dsa_index_sample")(table, qf, wcol, idx2d, proj)


def _silu(z):
    return z * jax.nn.sigmoid(z)


def _combine_nsa_kernel(oc_ref, os_ref, ow_ref, gate_ref, z_ref, o_ref):
    gate = jax.nn.sigmoid(gate_ref[:, 0:3 * NSA_H])
    for h in range(NSA_H):
        sl = slice(h * HD, (h + 1) * HD)
        o = (gate[:, 3 * h:3 * h + 1] * oc_ref[:, sl] + gate[:, 3 * h + 1:3 * h + 2] * os_ref[:, sl]
             + gate[:, 3 * h + 2:3 * h + 3] * ow_ref[:, sl])
        o_ref[:, sl] = (o * _silu(z_ref[:, sl])).astype(BF)


def _combine_nsa(o_cmp, o_sel, o_win, proj, tm):
    m, n = o_cmp.shape
    row = pl.BlockSpec((tm, n), lambda i: (i, 0))
    return pl.pallas_call(
        _combine_nsa_kernel, grid=(m // tm,),
        in_specs=[row, row, row, pl.BlockSpec((tm, LANE), lambda i: (i, C_SM // LANE + 1)),
                  pl.BlockSpec((tm, n), lambda i: (i, C_ZN // n))],
        out_specs=row, out_shape=jax.ShapeDtypeStruct((m, n), BF),
        compiler_params=_cp(("parallel",)), name="combine_nsa")(o_cmp, o_sel, o_win, proj, proj)


def _combine_dsa_kernel(o_ref_in, z_ref, o_ref):
    o_ref[...] = (o_ref_in[...] * _silu(z_ref[...])).astype(BF)


def _combine_dsa(o, proj, tm):
    m, n = o.shape
    row = pl.BlockSpec((tm, n), lambda i: (i, 0))
    return pl.pallas_call(
        _combine_dsa_kernel, grid=(m // tm,),
        in_specs=[row, pl.BlockSpec((tm, n), lambda i: (i, C_ZD // n))],
        out_specs=row, out_shape=jax.ShapeDtypeStruct((m, n), BF),
        compiler_params=_cp(("parallel",)), name="combine_dsa")(o, proj)


def _prep_w_in(w):
    seg, off = {}, 0
    for name, width in IN_SPLITS:
        seg[name] = w[:, off:off + width]
        off += width
    k = w.shape[0]
    small = jnp.concatenate([seg['idx_k'], seg['idx_w'], jnp.zeros((k, 32), w.dtype), seg['nsa_gate'],
                             jnp.zeros((k, PROJ_W - C_SM - 176), w.dtype)], axis=1)
    cols = [seg['nsa_q'], seg['nsa_z'], seg['dsa_q'], seg['dsa_z'], seg['idx_q'], seg['merge'],
            seg['nsa_kv'], seg['dsa_kv'], small]
    return jnp.concatenate(cols, axis=1).astype(BF)


def _slopes(n):
    return jnp.exp2(-8.0 * jnp.arange(1, n + 1, dtype=F32) / n)


def _tail(proj, xf, xb_unused, wts, a_in, b_in, tm_mm, tm_ln):
    hm = _gated_matmul(a_in, b_in, wts['pa'], wts['pd'], proj, tm_mm, 512)
    h = _matmul(hm, wts['out'], tm_mm, 512)
    return _residual_ln(xf, h, wts['gain'], wts['bias'], tm_ln)


def _layer_prompt(xf, xb, bsz, t, wts):
    tm = min(1024, bsz * t)
    proj = _matmul(xb, wts['in'], tm, 512)
    n_pages = t // PAGE
    table = jnp.arange(bsz * n_pages, dtype=jnp.int32).reshape(bsz, n_pages)
    kcvc = _compress(proj, table, C_KVN // (4 * HD), wts['pe_page'], wts['w1'], wts['w2'], n_pages)
    sl_n, sl_d = _slopes(NSA_H), _slopes(DSA_H)
    o_cmp, selm = _nsa_select(proj, kcvc, sl_n, bsz, t, -(-t // CMP), 0)
    o_sel = _fa_prompt("sel", proj, sl_n, bsz, t, C_QN, C_KVN + 2 * GROUPS * HD, C_KVN + 3 * GROUPS * HD, selm)
    o_win = _fa_prompt("win", proj, sl_n, bsz, t, C_QN, C_KVN + 4 * GROUPS * HD, C_KVN + 5 * GROUPS * HD)
    dmask = _dsa_index_prompt(proj, bsz, t)
    o_dsa = _fa_prompt("dsa", proj, sl_d, bsz, t, C_QD, C_KVD, C_KVD + GROUPS * HD, dmask)
    tme = min(256, bsz * t)
    a_in = _combine_nsa(o_cmp, o_sel, o_win, proj, tme)
    b_in = _combine_dsa(o_dsa, proj, tme)
    y, yb = _tail(proj, xf, xb, wts, a_in, b_in, tm, tme)
    return y, yb, proj


def _layer_sample(xf, xb, bsz, t, wts, nsa_pool, win_state, dsa_pool, idx_pool, page_table):
    rows = bsz * t
    proj = _matmul(xb, wts['in'], rows, 512)
    n_pages = page_table.shape[1]
    t0 = n_pages * PAGE
    nsa2d = nsa_pool.reshape(-1, 4 * GROUPS * HD)
    dsa2d = dsa_pool.reshape(-1, 2 * GROUPS * HD)
    idx2d = idx_pool.reshape(-1, IDX_D)
    wbuf = win_state.shape[1]
    win2d = win_state.reshape(-1, 2 * GROUPS * HD)
    win_table = jnp.arange(bsz * (wbuf // PAGE), dtype=jnp.int32).reshape(bsz, wbuf // PAGE)
    kcvc = _compress(nsa2d, page_table, 0, wts['pe_page'], wts['w1'], wts['w2'], min(64, n_pages))
    sl_n, sl_d = _slopes(NSA_H), _slopes(DSA_H)
    o_cmp, selm = _nsa_select(proj, kcvc, sl_n, bsz, t, -(-(t0 + t) // CMP), t0)
    o_sel = _fa_sample("sel", proj, nsa2d, page_table, sl_n, t, t0, 0, C_QN, 2 * GROUPS, 3 * GROUPS,
                       C_KVN + 2 * GROUPS * HD, C_KVN + 3 * GROUPS * HD, selm)
    o_win = _fa_sample("win", proj, win2d, win_table, sl_n, t, t0, t0 - wbuf, C_QN, 0, GROUPS,
                       C_KVN + 4 * GROUPS * HD, C_KVN + 5 * GROUPS * HD)
    qi = proj[:, C_QI:C_QI + IDX_H * IDX_D].reshape(bsz, t, IDX_H, IDX_D).transpose(0, 2, 1, 3)
    qf = qi.reshape(bsz * IDX_H * t, IDX_D)
    wi = proj[:, C_SM + SM_IDXW:C_SM + SM_IDXW + IDX_H].reshape(bsz, t, IDX_H).transpose(0, 2, 1)
    wcol = wi.reshape(bsz * IDX_H * t, 1)
    dmask = _dsa_index_sample(qf, wcol, idx2d, page_table, proj, t, t0)
    o_dsa = _fa_sample("dsa", proj, dsa2d, page_table, sl_d, t, t0, 0, C_QD, 0, GROUPS,
                       C_KVD, C_KVD + GROUPS * HD, dmask)
    a_in = _combine_nsa(o_cmp, o_sel, o_win, proj, rows)
    b_in = _combine_dsa(o_dsa, proj, rows)
    y, yb = _tail(proj, xf, xb, wts, a_in, b_in, rows, rows)
    return y, yb, proj


def _layer_weights(w_in, cmp_pe, cmp_w1, cmp_w2, w_proj_nsa, w_proj_dsa, w_out, ln_gain, ln_bias):
    pe = jnp.concatenate([cmp_pe, cmp_pe], axis=1)
    pe_page = jnp.concatenate([pe[0], pe[0], pe[1], pe[1]], axis=1)
    return {'in': _prep_w_in(w_in), 'pe_page': pe_page,
            'w1': cmp_w1.reshape(2, CMP, HD, HD).astype(BF), 'w2': cmp_w2.astype(BF),
            'pa': w_proj_nsa.astype(BF), 'pd': w_proj_dsa.astype(BF), 'out': w_out.astype(BF),
            'gain': ln_gain, 'bias': ln_bias}


def _new_state(proj, bsz, t):
    nsa_kv = proj[:, C_KVN:C_KVN + 4 * GROUPS * HD].reshape(bsz, t, 4, GROUPS, HD)
    win = proj[:, C_KVN + 4 * GROUPS * HD:C_KVN + 6 * GROUPS * HD].reshape(bsz, t, 2, GROUPS, HD)
    dsa_kv = proj[:, C_KVD:C_KVD + 2 * GROUPS * HD].reshape(bsz, t, 2, GROUPS, HD)
    idx_k = proj[:, C_SM:C_SM + IDX_D].reshape(bsz, t, IDX_D)
    return nsa_kv, win, dsa_kv, idx_k


def kernel(x_prompt, x_sample, cache_nsa_kv, state_nsa_win, cache_dsa_kv, cache_dsa_idx, page_table,
           w_in, cmp_pe, cmp_w1, cmp_w2, w_proj_nsa, w_proj_dsa, w_out, ln_gain, ln_bias):
    bp, tp, _ = x_prompt.shape
    bs, ts, _ = x_sample.shape
    yp, ys = x_prompt.reshape(bp * tp, D_MODEL), x_sample.reshape(bs * ts, D_MODEL)
    ypb, ysb = yp.astype(BF), ys.astype(BF)
    outs = [[] for _ in range(8)]
    for l in range(DEPTH):
        wts = _layer_weights(w_in[l], cmp_pe[l], cmp_w1[l], cmp_w2[l], w_proj_nsa[l], w_proj_dsa[l], w_out[l],
                             ln_gain[l], ln_bias[l])
        yp, ypb, proj_p = _layer_prompt(yp, ypb, bp, tp, wts)
        ys, ysb, proj_s = _layer_sample(ys, ysb, bs, ts, wts, cache_nsa_kv[l], state_nsa_win[l], cache_dsa_kv[l],
                                        cache_dsa_idx[l], page_table)
        nkv_p, win_p, dkv_p, idx_p = _new_state(proj_p, bp, tp)
        nkv_s, win_s, dkv_s, idx_s = _new_state(proj_s, bs, ts)
        win_all = jnp.concatenate([state_nsa_win[l], win_s], axis=1)
        keep_p, keep_s = min(WINDOW, tp), min(WINDOW, win_all.shape[1])
        for lst, val in zip(outs, (nkv_p, nkv_s, win_p[:, tp - keep_p:], win_all[:, win_all.shape[1] - keep_s:],
                                   dkv_p, dkv_s, idx_p, idx_s)):
            lst.append(val)
    return (yp.reshape(bp, tp, D_MODEL), ys.reshape(bs, ts, D_MODEL)) + tuple(jnp.stack(o) for o in outs)
```

```python
import functools

import jax
import jax.numpy as jnp
from jax import lax
from jax.experimental import pallas as pl
from jax.experimental.pallas import tpu as pltpu

D_MODEL = 4096
DEPTH = 2
PAGE = 128
HD = 128
NSA_H = 16
DSA_H = 16
GROUPS = 2
RH = NSA_H // GROUPS
CMP = 64
CMP_SHIFT = 6
N_SEL = 16
WINDOW = 512
IDX_H = 32
IDX_D = 64
IDX_TOPK = 256
LN_EPS = 1e-5
ALPHA = (2 * DEPTH) ** 0.25
NEG = -1e30
SEL_FORCE = 1e4
SCALE = HD ** -0.5
LOG2E = 1.4426950408889634
INT_MIN = -(2 ** 31)

IN_SPLITS = (('nsa_q', 2048), ('nsa_kv', 1536), ('nsa_gate', 48), ('nsa_z', 2048), ('dsa_q', 2048),
             ('dsa_kv', 512), ('idx_q', 2048), ('idx_w', 32), ('idx_k', 64), ('dsa_z', 2048), ('merge', 8192))

C_QN, C_ZN, C_QD, C_ZD, C_QI, C_MG, C_KVN, C_KVD = 0, 2048, 4096, 6144, 8192, 10240, 18432, 19968
PROJ_W = 20480
PROJ_LAYOUT = (('nsa_q', C_QN), ('nsa_z', C_ZN), ('dsa_q', C_QD), ('dsa_z', C_ZD), ('idx_q', C_QI),
               ('merge', C_MG), ('nsa_kv', C_KVN), ('dsa_kv', C_KVD))
SM_W = 256
SM_IDXW = 64
LANE = 128
KV_W = 2 * GROUPS * HD

VMEM_LIMIT = 48 * 1024 * 1024
BF = jnp.bfloat16
F32 = jnp.float32


def _cp(sem):
    return pltpu.CompilerParams(dimension_semantics=sem, vmem_limit_bytes=VMEM_LIMIT)


def _dot_nt(a, b):
    return lax.dot_general(a, b, (((1,), (1,)), ((), ())), preferred_element_type=F32)


def _dot(a, b):
    return jnp.dot(a, b, preferred_element_type=F32)


def _mm_kernel(x_ref, w_ref, o_ref):
    o_ref[...] = _dot(x_ref[...], w_ref[...]).astype(o_ref.dtype)


def _matmul(x, w, tm, tn, out_dtype=F32):
    m, k = x.shape
    n = w.shape[1]
    return pl.pallas_call(
        _mm_kernel, grid=(m // tm, n // tn),
        in_specs=[pl.BlockSpec((tm, k), lambda i, j: (i, 0)), pl.BlockSpec((k, tn), lambda i, j: (0, j))],
        out_specs=pl.BlockSpec((tm, tn), lambda i, j: (i, j)),
        out_shape=jax.ShapeDtypeStruct((m, n), out_dtype),
        compiler_params=_cp(("parallel", "parallel")), name="matmul")(x, w)


def _gated_mm_kernel(a_ref, b_ref, wa_ref, wb_ref, m0_ref, m1_ref, o_ref):
    a = _dot(a_ref[...], wa_ref[...])
    b = _dot(b_ref[...], wb_ref[...])
    o_ref[...] = (jax.nn.sigmoid(m0_ref[...]) * a + jax.nn.sigmoid(m1_ref[...]) * b).astype(o_ref.dtype)


def _gated_matmul(a_in, b_in, wa, wb, proj, tm, tn):
    m, k = a_in.shape
    n = wa.shape[1]
    c0, c1 = C_MG // tn, (C_MG + D_MODEL) // tn
    return pl.pallas_call(
        _gated_mm_kernel, grid=(m // tm, n // tn),
        in_specs=[pl.BlockSpec((tm, k), lambda i, j: (i, 0)), pl.BlockSpec((tm, k), lambda i, j: (i, 0)),
                  pl.BlockSpec((k, tn), lambda i, j: (0, j)), pl.BlockSpec((k, tn), lambda i, j: (0, j)),
                  pl.BlockSpec((tm, tn), lambda i, j: (i, c0 + j)), pl.BlockSpec((tm, tn), lambda i, j: (i, c1 + j))],
        out_specs=pl.BlockSpec((tm, tn), lambda i, j: (i, j)),
        out_shape=jax.ShapeDtypeStruct((m, n), BF),
        compiler_params=_cp(("parallel", "parallel")), name="gated_matmul")(a_in, b_in, wa, wb, proj, proj)


def _ln_kernel(x_ref, h_ref, g_ref, b_ref, y_ref, yb_ref):
    v = ALPHA * x_ref[...] + h_ref[...]
    mu = jnp.mean(v, axis=-1, keepdims=True)
    c = v - mu
    var = jnp.mean(c * c, axis=-1, keepdims=True)
    y = c * lax.rsqrt(var + LN_EPS) * g_ref[...] + b_ref[...]
    y_ref[...] = y
    yb_ref[...] = y.astype(BF)


def _residual_ln(x, h, gain, bias, tm):
    m, n = x.shape
    row = pl.BlockSpec((tm, n), lambda i: (i, 0))
    vec = pl.BlockSpec((1, n), lambda i: (0, 0))
    return pl.pallas_call(
        _ln_kernel, grid=(m // tm,), in_specs=[row, row, vec, vec], out_specs=[row, row],
        out_shape=[jax.ShapeDtypeStruct((m, n), F32), jax.ShapeDtypeStruct((m, n), BF)],
        compiler_params=_cp(("parallel",)), name="residual_ln")(x, h, gain.reshape(1, n), bias.reshape(1, n))


def _gelu_tanh(x):
    return 0.5 * x * (1.0 + jnp.tanh(0.7978845608028654 * (x + 0.044715 * (x * x * x))))


def _compress_kernel(pt_ref, x_ref, pe_ref, w1_ref, w2_ref, o_ref, slab_ref, *, pg):
    p = pl.program_id(2)
    for cg in range(2 * GROUPS):
        slab_ref[cg, pl.ds(pl.multiple_of(p * PAGE, PAGE), PAGE), :] = (
            x_ref[:, cg * HD:(cg + 1) * HD] + pe_ref[:, cg * HD:(cg + 1) * HD])

    @pl.when(p == pg - 1)
    def _():
        nblk = pg * (PAGE // CMP)
        for c in range(2):
            acc = jnp.zeros((GROUPS * nblk, HD), F32)
            for tok in range(CMP):
                parts = [slab_ref[c * GROUPS + g, pl.ds(tok, nblk, stride=CMP), :] for g in range(GROUPS)]
                lhs = jnp.concatenate(parts, axis=0).astype(BF)
                acc = acc + _dot(lhs, w1_ref[c, tok])
            out = _dot(_gelu_tanh(acc).astype(BF), w2_ref[c])
            for g in range(GROUPS):
                o_ref[c, g] = out[g * nblk:(g + 1) * nblk]


def _compress(src2d, table, col_blk, pe_page, w1, w2, pg):
    b, n_pages = table.shape
    ns = n_pages // pg
    nblk = pg * (PAGE // CMP)
    kern = functools.partial(_compress_kernel, pg=pg)
    gs = pltpu.PrefetchScalarGridSpec(
        num_scalar_prefetch=1, grid=(b, ns, pg),
        in_specs=[pl.BlockSpec((PAGE, 4 * HD), lambda bi, s, p, pt: (pt[bi, s * pg + p], col_blk)),
                  pl.BlockSpec((PAGE, 4 * HD), lambda bi, s, p, pt: (0, 0)),
                  pl.BlockSpec((2, CMP, HD, HD), lambda bi, s, p, pt: (0, 0, 0, 0)),
                  pl.BlockSpec((2, HD, HD), lambda bi, s, p, pt: (0, 0, 0))],
        out_specs=pl.BlockSpec((None, 2, GROUPS, nblk, HD), lambda bi, s, p, pt: (bi, 0, 0, s, 0)),
        scratch_shapes=[pltpu.VMEM((2 * GROUPS, pg * PAGE, HD), F32)])
    return pl.pallas_call(
        kern, grid_spec=gs, out_shape=jax.ShapeDtypeStruct((b, 2, GROUPS, ns * nblk, HD), F32),
        compiler_params=_cp(("parallel", "arbitrary", "arbitrary")), name="compress")(table, src2d, pe_page, w1, w2)


def _sortable(x):
    b = lax.bitcast_convert_type(x, jnp.int32)
    return jnp.where(b < 0, b ^ jnp.int32(0x7FFFFFFF), b)


def _count(pred):
    return jnp.sum(jnp.where(pred, 1.0, 0.0), axis=-1, keepdims=True)


def _topk_mask(key_ref, k, idx, idx_bits):
    rows = key_ref.shape[0]
    kf = float(k)
    zero = jnp.zeros((rows, 1), jnp.int32)
    t0 = jnp.where(_count(key_ref[...] >= zero) >= kf, zero, jnp.full((rows, 1), INT_MIN, jnp.int32))

    def value_bit(i, t):
        cand = t | jnp.left_shift(jnp.int32(1), jnp.int32(30) - i)
        return jnp.where(_count(key_ref[...] >= cand) >= kf, cand, t)

    t = lax.fori_loop(0, 31, value_bit, t0)
    keys = key_ref[...]
    need = kf - _count(keys > t)

    def index_bit(i, c):
        cand = c | jnp.left_shift(jnp.int32(1), jnp.int32(idx_bits - 1) - i)
        below = _count((key_ref[...] == t) & (idx < cand))
        return jnp.where(below < need, cand, c)

    c0 = lax.fori_loop(0, idx_bits, index_bit, zero)
    return (keys > t) | ((keys == t) & (idx <= c0))


def _masked_softmax(s, mask):
    s = jnp.where(mask, s, NEG)
    e = jnp.where(mask, jnp.exp(s - jnp.max(s, axis=-1, keepdims=True)), 0.0)
    return e / jnp.maximum(jnp.sum(e, axis=-1, keepdims=True), 1e-30)


def _nsa_select_kernel(slopes_ref, q_ref, kc_ref, vc_ref, ocmp_ref, sel_ref, key_scr,
                       *, qb, nb, nbs, nbs_pad, t0, key_level):
    g = pl.program_id(1)
    i = pl.program_id(2)
    tq = t0 + i * qb + lax.broadcasted_iota(jnp.int32, (qb, 1), 0)
    blk_end = lax.broadcasted_iota(jnp.int32, (1, nb), 1) * CMP + (CMP - 1)
    valid = blk_end <= tq
    distf = (tq - blk_end).astype(F32)
    kc = kc_ref[...].astype(BF)
    vc = vc_ref[...].astype(BF)
    imp = jnp.zeros((qb, nb), F32)
    for r in range(RH):
        q = q_ref[:, r * HD:(r + 1) * HD].astype(BF)
        s = _dot_nt(q, kc) * SCALE - slopes_ref[g * RH + r] * distf
        p = _masked_softmax(s, valid)
        ocmp_ref[:, r * HD:(r + 1) * HD] = _dot(p.astype(BF), vc)
        imp = imp + p
    if nbs_pad > nb:
        imp = jnp.concatenate([imp, jnp.zeros((qb, nbs_pad - nb), F32)], axis=1)
    j = lax.broadcasted_iota(jnp.int32, (1, nbs_pad), 1)
    cur = jnp.right_shift(tq, CMP_SHIFT)
    forced = (j == 0) | (j == cur) | (j == cur - 1)
    score = jnp.where(j <= cur, jnp.where(forced, SEL_FORCE, imp), -SEL_FORCE)
    key_scr[...] = _sortable(score)
    sel = _topk_mask(key_scr, min(N_SEL, nbs), j, max(1, (nbs_pad - 1).bit_length()))
    selb = jnp.where(sel, 1.0, 0.0).astype(BF)
    n_col = lax.broadcasted_iota(jnp.int32, (nbs_pad, 1), 0)
    if key_level:
        lane = lax.broadcasted_iota(jnp.int32, (1, LANE), 1)

        def chunk(c, carry):
            expand = jnp.where(n_col == jnp.right_shift(c * LANE + lane, CMP_SHIFT), 1.0, 0.0).astype(BF)
            sel_ref[:, pl.ds(pl.multiple_of(c * LANE, LANE), LANE)] = ((_dot(selb, expand) - 1.0) * (-NEG)).astype(BF)
            return carry

        lax.fori_loop(0, sel_ref.shape[-1] // LANE, chunk, 0)
    else:
        eye = jnp.where(n_col == j, 1.0, 0.0).astype(BF)
        bias_t = (_dot_nt(eye, selb) - 1.0) * (-NEG)
        for n in range(nbs):
            sel_ref[n] = jnp.broadcast_to(bias_t[n:n + 1, :], (8, qb))


def _nsa_select(proj, kcvc, slopes, bsz, t, nbs, t0, key_len):
    qb = min(t, LANE)
    nq = t // qb
    nb = kcvc.shape[3]
    nbs_pad = nbs if nbs == nb else -(-nbs // LANE) * LANE
    kern = functools.partial(_nsa_select_kernel, qb=qb, nb=nb, nbs=nbs, nbs_pad=nbs_pad, t0=t0,
                             key_level=key_len > 0)
    wq = RH * HD
    if key_len:
        sel_spec = pl.BlockSpec((None, None, qb, key_len), lambda b, g, i: (b, g, i, 0))
        sel_shape = jax.ShapeDtypeStruct((bsz, GROUPS, t, key_len), BF)
    else:
        sel_spec = pl.BlockSpec((None, None, nbs, 8, qb), lambda b, g, i: (b, g, 0, 0, i))
        sel_shape = jax.ShapeDtypeStruct((bsz, GROUPS, nbs, 8, t), F32)
    return pl.pallas_call(
        kern, grid=(bsz, GROUPS, nq),
        in_specs=[pl.BlockSpec(memory_space=pltpu.SMEM),
                  pl.BlockSpec((qb, wq), lambda b, g, i: (b * nq + i, C_QN // wq + g)),
                  pl.BlockSpec((None, None, None, nb, HD), lambda b, g, i: (b, 0, g, 0, 0)),
                  pl.BlockSpec((None, None, None, nb, HD), lambda b, g, i: (b, 1, g, 0, 0))],
        out_specs=[pl.BlockSpec((qb, wq), lambda b, g, i: (b * nq + i, g)), sel_spec],
        out_shape=[jax.ShapeDtypeStruct((bsz * t, NSA_H * HD), F32), sel_shape],
        scratch_shapes=[pltpu.VMEM((qb, nbs_pad), jnp.int32)],
        compiler_params=_cp(("parallel", "parallel", "parallel")), name="nsa_select")(slopes, proj, kcvc, kcvc)


def _fa_prompt_kernel(*refs, mode, qb, kb):
    if mode == "win":
        slopes_ref, q_ref, k_ref, v_ref, o_ref = refs[:5]
        mask_ref = None
    else:
        slopes_ref, q_ref, k_ref, v_ref, mask_ref, o_ref = refs[:6]
    qs_scr, bias_scr, s_scr, p_scr, mb_scr, m_scr, a_scr, acc_scr = refs[-8:]
    g = pl.program_id(1)
    i = pl.program_id(2)
    k_local = lax.broadcasted_iota(jnp.int32, (kb, qb), 0)
    q_local = lax.broadcasted_iota(jnp.int32, (kb, qb), 1)
    for r in range(RH):
        qs_scr[r * qb:(r + 1) * qb, :] = (q_ref[:, r * HD:(r + 1) * HD] * (SCALE * LOG2E)).astype(BF)
        bias_scr[:, r * qb:(r + 1) * qb] = (slopes_ref[g * RH + r] * LOG2E) * k_local.astype(F32)
    m_scr[...] = jnp.full(m_scr.shape, NEG, F32)
    acc_scr[...] = jnp.zeros(acc_scr.shape, F32)
    j_hi = ((i + 1) * qb - 1) // kb
    j_lo = jnp.maximum(i * qb - (WINDOW - 1), 0) // kb if mode == "win" else 0

    def body(j, carry):
        off = pl.multiple_of(j * kb, kb)
        s_scr[...] = _dot_nt(k_ref[pl.ds(off, kb), :].astype(BF), qs_scr[...])
        dist = (i * qb + q_local) - (j * kb + k_local)
        ok = dist >= 0
        if mode == "win":
            ok = ok & (dist < WINDOW)
        mb = jnp.where(ok, 0.0, NEG)
        if mode == "sel":
            tiles = mask_ref[pl.ds(j * (kb // CMP), kb // CMP)]
            mb = mb + jnp.concatenate([jnp.tile(tiles[n], (CMP // 8, 1)) for n in range(kb // CMP)], axis=0)
        elif mode == "dsa":
            mb = mb + mask_ref[pl.ds(off, kb), :].astype(F32)
        mb_scr[...] = mb
        cbase = (j * kb - i * qb).astype(F32)
        for r in range(RH):
            sl = slice(r * qb, (r + 1) * qb)
            c = (slopes_ref[g * RH + r] * LOG2E) * cbase
            x = s_scr[:, sl] + bias_scr[:, sl] + mb_scr[...]
            m_prev = m_scr[:, sl]
            m_new = jnp.maximum(m_prev, jnp.max(x, axis=0, keepdims=True) + c)
            p_scr[:, sl] = jnp.exp2(x - (m_new - c)).astype(BF)
            a_scr[:, sl] = jnp.exp2(m_prev - m_new)
            m_scr[:, sl] = m_new
        vt = jnp.concatenate([v_ref[pl.ds(off, kb), :].T, jnp.ones((16, kb), F32)], axis=0).astype(BF)
        acc_scr[...] = acc_scr[...] * a_scr[...] + _dot(vt, p_scr[...])
        return carry

    lax.fori_loop(j_lo, j_hi + 1, body, 0)
    for r in range(RH):
        sl = slice(r * qb, (r + 1) * qb)
        o_t = jnp.where(m_scr[:, sl] > 0.5 * NEG,
                        acc_scr[0:HD, sl] / jnp.maximum(acc_scr[HD:HD + 1, sl], 1e-30), 0.0)
        o_ref[:, r * HD:(r + 1) * HD] = o_t.T


def _fa_prompt(mode, proj, slopes, bsz, t, q_col, k_col, v_col, mask=None):
    qb = LANE
    kb = min(2 * LANE, t)
    nq = t // qb
    wq = RH * HD
    in_specs = [pl.BlockSpec(memory_space=pltpu.SMEM),
                pl.BlockSpec((qb, wq), lambda b, g, i: (b * nq + i, q_col // wq + g)),
                pl.BlockSpec((t, HD), lambda b, g, i: (b, k_col // HD + g)),
                pl.BlockSpec((t, HD), lambda b, g, i: (b, v_col // HD + g))]
    args = [slopes, proj, proj, proj]
    if mode == "sel":
        in_specs.append(pl.BlockSpec((None, None, t // CMP, 8, qb), lambda b, g, i: (b, g, 0, 0, i)))
        args.append(mask)
    elif mode == "dsa":
        in_specs.append(pl.BlockSpec((None, t, qb), lambda b, g, i: (b, 0, i)))
        args.append(mask)
    kern = functools.partial(_fa_prompt_kernel, mode=mode, qb=qb, kb=kb)
    lanes = RH * qb
    return pl.pallas_call(
        kern, grid=(bsz, GROUPS, nq), in_specs=in_specs,
        out_specs=pl.BlockSpec((qb, wq), lambda b, g, i: (b * nq + i, g)),
        out_shape=jax.ShapeDtypeStruct((bsz * t, GROUPS * wq), F32),
        scratch_shapes=[pltpu.VMEM((lanes, HD), BF), pltpu.VMEM((kb, lanes), F32), pltpu.VMEM((kb, lanes), F32),
                        pltpu.VMEM((kb, lanes), BF), pltpu.VMEM((kb, qb), F32), pltpu.VMEM((1, lanes), F32),
                        pltpu.VMEM((1, lanes), F32), pltpu.VMEM((HD + 16, lanes), F32)],
        compiler_params=_cp(("parallel", "parallel", "parallel")), name="fa_prompt_" + mode)(*args)


PPS = 4


def _fa_sample_kernel(*refs, mode, t, nsteps, t0, kpos0):
    pt_ref, slope_ref, q_ref = refs[:3]
    page_refs = refs[3:3 + PPS]
    new_ref = refs[3 + PPS]
    rest = refs[4 + PPS:]
    if mode == "win":
        mask_ref = maskn_ref = None
        o_ref, qs_scr, m_scr, l_scr, acc_scr = rest
    else:
        mask_ref, maskn_ref, o_ref, qs_scr, m_scr, l_scr, acc_scr = rest
    j = pl.program_id(1)
    rows = RH * t

    @pl.when(j == 0)
    def _():
        for g in range(GROUPS):
            qs_scr[g] = jnp.concatenate(
                [q_ref[:, (g * RH + r) * HD:(g * RH + r + 1) * HD] for r in range(RH)], axis=0).astype(BF)
        m_scr[...] = jnp.full(m_scr.shape, NEG, F32)
        l_scr[...] = jnp.zeros(l_scr.shape, F32)
        acc_scr[...] = jnp.zeros(acc_scr.shape, F32)

    tq = t0 + jnp.concatenate([lax.broadcasted_iota(jnp.int32, (t, 1), 0)] * RH, axis=0)

    def step(kv, kpos, extra_ok, mref):
        dist = tq - kpos
        ok = dist >= 0
        if mode == "win":
            ok = ok & (dist < WINDOW)
        if extra_ok is not None:
            ok = ok & extra_ok
        distf = dist.astype(F32)
        for g in range(GROUPS):
            k = jnp.concatenate([x[:, g * HD:(g + 1) * HD] for x in kv], axis=0).astype(BF)
            v = jnp.concatenate([x[:, (GROUPS + g) * HD:(GROUPS + g + 1) * HD] for x in kv], axis=0).astype(BF)
            s = _dot_nt(qs_scr[g], k) * SCALE - slope_ref[g] * distf
            s = jnp.where(ok, s, NEG)
            if mref is not None:
                mval = mref[g] if mode == "sel" else mref[...]
                s = s + jnp.tile(mval.astype(F32), (RH, 1))
            m_prev = m_scr[g]
            m_new = jnp.maximum(m_prev, jnp.max(s, axis=-1, keepdims=True))
            a = jnp.exp(m_prev - m_new)
            p = jnp.exp(s - m_new)
            l_scr[g] = a * l_scr[g] + jnp.sum(p, axis=-1, keepdims=True)
            acc_scr[g] = a * acc_scr[g] + _dot(p.astype(BF), v)
            m_scr[g] = m_new

    lane = lax.broadcasted_iota(jnp.int32, (1, PPS * PAGE), 1)
    step([r[...] for r in page_refs], kpos0 + j * (PPS * PAGE) + lane, None, mask_ref)

    @pl.when(j == nsteps - 1)
    def _():
        lane1 = lax.broadcasted_iota(jnp.int32, (1, PAGE), 1)
        kv_new = jnp.concatenate([new_ref[...], jnp.zeros((PAGE - t, KV_W), F32)], axis=0)
        step([kv_new], t0 + lane1, lane1 < t, maskn_ref)
        for g in range(GROUPS):
            o = jnp.where(m_scr[g] > 0.5 * NEG, acc_scr[g] / jnp.maximum(l_scr[g], 1e-30), 0.0)
            for r in range(RH):
                o_ref[:, (g * RH + r) * HD:(g * RH + r + 1) * HD] = o[r * t:(r + 1) * t]


def _fa_sample(mode, proj, past2d, table, slopes, t, t0, kpos0, q_col, past_blk, new_col, mask=None):
    bsz, nkb = table.shape
    nsteps = nkb // PPS
    rows = RH * t
    wq = GROUPS * RH * HD
    slope_col = jnp.repeat(slopes.reshape(GROUPS, RH), t, axis=1).reshape(GROUPS, rows, 1)
    in_specs = [pl.BlockSpec((GROUPS, rows, 1), lambda b, j, pt: (0, 0, 0)),
                pl.BlockSpec((t, wq), lambda b, j, pt: (b, q_col // wq))]
    for p in range(PPS):
        in_specs.append(pl.BlockSpec((PAGE, KV_W), lambda b, j, pt, p=p: (pt[b, j * PPS + p], past_blk)))
    in_specs.append(pl.BlockSpec((t, KV_W), lambda b, j, pt: (b, new_col // KV_W)))
    args = [slope_col, proj] + [past2d] * PPS + [proj]
    if mode == "sel":
        in_specs.append(pl.BlockSpec((None, GROUPS, t, PPS * PAGE), lambda b, j, pt: (b, 0, 0, j)))
        in_specs.append(pl.BlockSpec((None, GROUPS, t, PAGE), lambda b, j, pt: (b, 0, 0, nkb)))
        args += [mask, mask]
    elif mode == "dsa":
        in_specs.append(pl.BlockSpec((None, t, PPS * PAGE), lambda b, j, pt: (b, 0, j)))
        in_specs.append(pl.BlockSpec((None, t, PAGE), lambda b, j, pt: (b, 0, nkb)))
        args += [mask, mask]
    kern = functools.partial(_fa_sample_kernel, mode=mode, t=t, nsteps=nsteps, t0=t0, kpos0=kpos0)
    gs = pltpu.PrefetchScalarGridSpec(
        num_scalar_prefetch=1, grid=(bsz, nsteps), in_specs=in_specs,
        out_specs=pl.BlockSpec((t, wq), lambda b, j, pt: (b, 0)),
        scratch_shapes=[pltpu.VMEM((GROUPS, rows, HD), BF), pltpu.VMEM((GROUPS, rows, 1), F32),
                        pltpu.VMEM((GROUPS, rows, 1), F32), pltpu.VMEM((GROUPS, rows, HD), F32)])
    return pl.pallas_call(
        kern, grid_spec=gs, out_shape=jax.ShapeDtypeStruct((bsz * t, wq), F32),
        compiler_params=_cp(("parallel", "arbitrary")), name="fa_sample_" + mode)(table, *args)


IDX_CHUNK = 256


def _dsa_index_prompt_kernel(q_ref, sq_ref, sk_ref, mask_ref, score_scr, key_scr, *, qb, t, top):
    i = pl.program_id(1)
    tq = i * qb + lax.broadcasted_iota(jnp.int32, (qb, 1), 0)
    w = sq_ref[:, SM_IDXW:SM_IDXW + IDX_H] * (IDX_H ** -0.5)
    score_scr[...] = jnp.zeros(score_scr.shape, F32)
    ch = min(IDX_CHUNK, t)

    def chunk(c, carry):
        off = pl.multiple_of(c * ch, ch)
        kk = sk_ref[pl.ds(off, ch), 0:IDX_D].astype(BF)
        acc = jnp.zeros((qb, ch), F32)
        for h in range(IDX_H):
            qh = q_ref[:, h * IDX_D:(h + 1) * IDX_D].astype(BF)
            acc = acc + w[:, h:h + 1] * jnp.maximum(_dot_nt(qh, kk) * (IDX_D ** -0.5), 0.0)
        score_scr[:, pl.ds(off, ch)] = acc
        return carry

    lax.fori_loop(0, ((i + 1) * qb + ch - 1) // ch, chunk, 0)
    s_pos = lax.broadcasted_iota(jnp.int32, (1, t), 1)
    causal = s_pos <= tq
    key_scr[...] = _sortable(jnp.where(causal, score_scr[...], NEG))
    sel = _topk_mask(key_scr, top, s_pos, max(1, (t - 1).bit_length()))
    bias = jnp.where(sel & causal, 0.0, NEG)
    for c in range(t // LANE):
        mask_ref[c * LANE:(c + 1) * LANE, :] = bias[:, c * LANE:(c + 1) * LANE].T.astype(BF)


def _dsa_index_prompt(proj, sm, bsz, t):
    qb = LANE
    nq = t // qb
    top = min(IDX_TOPK, t // 4)
    kern = functools.partial(_dsa_index_prompt_kernel, qb=qb, t=t, top=top)
    wq = IDX_H * IDX_D
    return pl.pallas_call(
        kern, grid=(bsz, nq),
        in_specs=[pl.BlockSpec((qb, wq), lambda b, i: (b * nq + i, C_QI // wq)),
                  pl.BlockSpec((qb, LANE), lambda b, i: (b * nq + i, 0)),
                  pl.BlockSpec((t, LANE), lambda b, i: (b, 0))],
        out_specs=pl.BlockSpec((None, t, qb), lambda b, i: (b, 0, i)),
        out_shape=jax.ShapeDtypeStruct((bsz, t, t), BF),
        scratch_shapes=[pltpu.VMEM((qb, t), F32), pltpu.VMEM((qb, t), jnp.int32)],
        compiler_params=_cp(("parallel", "parallel")), name="dsa_index_prompt")(proj, sm, sm)


def _dsa_index_sample_kernel(pt_ref, qf_ref, wcol_ref, kp_ref, sn_ref, mask_ref, score_scr, key_scr,
                             *, t, nkb, t0, lpad, top):
    j = pl.program_id(1)

    @pl.when(j == 0)
    def _():
        score_scr[...] = jnp.full(score_scr.shape, NEG, F32)

    qf = qf_ref[...].astype(BF)
    wcol = wcol_ref[...] * (IDX_H ** -0.5)

    def scores(kk):
        r = jnp.maximum(_dot_nt(qf, kk) * (IDX_D ** -0.5), 0.0) * wcol
        sc = r[0:t]
        for h in range(1, IDX_H):
            sc = sc + r[h * t:(h + 1) * t]
        return sc

    score_scr[:, pl.ds(pl.multiple_of(j * PAGE, PAGE), PAGE)] = scores(kp_ref[...].astype(BF))

    @pl.when(j == nkb - 1)
    def _():
        tq = t0 + lax.broadcasted_iota(jnp.int32, (t, 1), 0)
        lane = lax.broadcasted_iota(jnp.int32, (1, PAGE), 1)
        kn = jnp.concatenate([sn_ref[:, 0:IDX_D], jnp.zeros((PAGE - t, IDX_D), F32)], axis=0).astype(BF)
        ok = (lane < t) & (t0 + lane <= tq)
        score_scr[:, nkb * PAGE:(nkb + 1) * PAGE] = jnp.where(ok, scores(kn), NEG)
        s_pos = lax.broadcasted_iota(jnp.int32, (1, lpad), 1)
        key_scr[...] = _sortable(score_scr[...])
        sel = _topk_mask(key_scr, top, s_pos, max(1, (lpad - 1).bit_length()))
        mask_ref[...] = jnp.where(sel & (s_pos <= tq) & (s_pos < t0 + t), 0.0, NEG).astype(BF)


def _dsa_index_sample(qf, wcol, idx2d, table, sm, t, t0):
    bsz, nkb = table.shape
    lpad = (nkb + 1) * PAGE
    top = min(IDX_TOPK, (t0 + t) // 4)
    rows = IDX_H * t
    kern = functools.partial(_dsa_index_sample_kernel, t=t, nkb=nkb, t0=t0, lpad=lpad, top=top)
    gs = pltpu.PrefetchScalarGridSpec(
        num_scalar_prefetch=1, grid=(bsz, nkb),
        in_specs=[pl.BlockSpec((rows, IDX_D), lambda b, j, pt: (b, 0)),
                  pl.BlockSpec((rows, 1), lambda b, j, pt: (b, 0)),
                  pl.BlockSpec((PAGE, IDX_D), lambda b, j, pt: (pt[b, j], 0)),
                  pl.BlockSpec((t, LANE), lambda b, j, pt: (b, 0))],
        out_specs=pl.BlockSpec((None, t, lpad), lambda b, j, pt: (b, 0, 0)),
        scratch_shapes=[pltpu.VMEM((t, lpad), F32), pltpu.VMEM((t, lpad), jnp.int32)])
    return pl.pallas_call(
        kern, grid_spec=gs, out_shape=jax.ShapeDtypeStruct((bsz, t, lpad), BF),
        compiler_params=_cp(("parallel", "arbitrary")), name="dsa_index_sample")(table, qf, wcol, idx2d, sm)


def _silu(z):
    return z * jax.nn.sigmoid(z)


def _combine_nsa_kernel(oc_ref, os_ref, ow_ref, gate_ref, z_ref, o_ref):
    gate = jax.nn.sigmoid(gate_ref[:, 0:3 * NSA_H])
    for h in range(NSA_H):
        sl = slice(h * HD, (h + 1) * HD)
        o = (gate[:, 3 * h:3 * h + 1] * oc_ref[:, sl] + gate[:, 3 * h + 1:3 * h + 2] * os_ref[:, sl]
             + gate[:, 3 * h + 2:3 * h + 3] * ow_ref[:, sl])
        o_ref[:, sl] = (o * _silu(z_ref[:, sl])).astype(BF)


def _combine_nsa(o_cmp, o_sel, o_win, proj, sm, tm):
    m, n = o_cmp.shape
    row = pl.BlockSpec((tm, n), lambda i: (i, 0))
    return pl.pallas_call(
        _combine_nsa_kernel, grid=(m // tm,),
        in_specs=[row, row, row, pl.BlockSpec((tm, LANE), lambda i: (i, 1)),
                  pl.BlockSpec((tm, n), lambda i: (i, C_ZN // n))],
        out_specs=row, out_shape=jax.ShapeDtypeStruct((m, n), BF),
        compiler_params=_cp(("parallel",)), name="combine_nsa")(o_cmp, o_sel, o_win, sm, proj)


def _combine_dsa_kernel(o_ref_in, z_ref, o_ref):
    o_ref[...] = (o_ref_in[...] * _silu(z_ref[...])).astype(BF)


def _combine_dsa(o, proj, tm):
    m, n = o.shape
    row = pl.BlockSpec((tm, n), lambda i: (i, 0))
    return pl.pallas_call(
        _combine_dsa_kernel, grid=(m // tm,),
        in_specs=[row, pl.BlockSpec((tm, n), lambda i: (i, C_ZD // n))],
        out_specs=row, out_shape=jax.ShapeDtypeStruct((m, n), BF),
        compiler_params=_cp(("parallel",)), name="combine_dsa")(o, proj)


PREP_TN = 512
PREP_TK = 1024


def _prep_tables():
    src, off = {}, 0
    for name, width in IN_SPLITS:
        src[name] = off
        off += width
    width = dict(IN_SPLITS)
    shifts, base, cls = [], [], []
    for name, _ in PROJ_LAYOUT:
        for c in range(width[name] // PREP_TN):
            s = src[name] + c * PREP_TN
            if s % LANE not in shifts:
                shifts.append(s % LANE)
            base.append(s // LANE)
            cls.append(shifts.index(s % LANE))
    return tuple(shifts), base, cls


def _prep_kernel(base_ref, cls_ref, *refs, shifts):
    o_ref = refs[-1]
    j = pl.program_id(1)
    for k, s in enumerate(shifts):
        @pl.when(cls_ref[j] == k)
        def _():
            win = jnp.concatenate([r[...] for r in refs[:-1]], axis=1)
            o_ref[...] = win[:, s:s + PREP_TN].astype(BF)


def _prep_w_in(w_in, l):
    shifts, base, cls = _prep_tables()
    k = w_in.shape[1]
    nwin = PREP_TN // LANE + 1
    kern = functools.partial(_prep_kernel, shifts=shifts)
    gs = pltpu.PrefetchScalarGridSpec(
        num_scalar_prefetch=2, grid=(k // PREP_TK, len(base)),
        in_specs=[pl.BlockSpec((None, PREP_TK, LANE), lambda i, j, bs, cs, m=m: (l, i, bs[j] + m))
                  for m in range(nwin)],
        out_specs=pl.BlockSpec((PREP_TK, PREP_TN), lambda i, j, bs, cs: (i, j)))
    return pl.pallas_call(
        kern, grid_spec=gs, out_shape=jax.ShapeDtypeStruct((k, PROJ_W), BF),
        compiler_params=_cp(("parallel", "parallel")), name="prep_w_in")(
            jnp.asarray(base, jnp.int32), jnp.asarray(cls, jnp.int32), *([w_in] * nwin))


def _prep_w_small(w):
    seg, off = {}, 0
    for name, width in IN_SPLITS:
        seg[name] = (off, width)
        off += width
    cut = lambda name: w[:, seg[name][0]:seg[name][0] + seg[name][1]]
    k = w.shape[0]
    return jnp.concatenate([cut('idx_k'), cut('idx_w'), jnp.zeros((k, 32), w.dtype), cut('nsa_gate'),
                            jnp.zeros((k, SM_W - 176), w.dtype)], axis=1).astype(BF)


def _slopes(n):
    return jnp.exp2(-8.0 * jnp.arange(1, n + 1, dtype=F32) / n)


def _tail(proj, xf, wts, a_in, b_in, tm_mm, tm_ln):
    hm = _gated_matmul(a_in, b_in, wts['pa'], wts['pd'], proj, tm_mm, 512)
    h = _matmul(hm, wts['out'], tm_mm, 512)
    return _residual_ln(xf, h, wts['gain'], wts['bias'], tm_ln)


def _layer_prompt(xf, xb, bsz, t, wts):
    tm = min(1024, bsz * t)
    proj = _matmul(xb, wts['in'], tm, 512)
    sm = _matmul(xb, wts['in_small'], tm, SM_W)
    n_pages = t // PAGE
    table = jnp.arange(bsz * n_pages, dtype=jnp.int32).reshape(bsz, n_pages)
    kcvc = _compress(proj, table, C_KVN // (4 * HD), wts['pe_page'], wts['w1'], wts['w2'], n_pages)
    sl_n, sl_d = _slopes(NSA_H), _slopes(DSA_H)
    o_cmp, selm = _nsa_select(proj, kcvc, sl_n, bsz, t, t // CMP, 0, 0)
    o_sel = _fa_prompt("sel", proj, sl_n, bsz, t, C_QN, C_KVN + KV_W, C_KVN + KV_W + GROUPS * HD, selm)
    o_win = _fa_prompt("win", proj, sl_n, bsz, t, C_QN, C_KVN + 2 * KV_W, C_KVN + 2 * KV_W + GROUPS * HD)
    dmask = _dsa_index_prompt(proj, sm, bsz, t)
    o_dsa = _fa_prompt("dsa", proj, sl_d, bsz, t, C_QD, C_KVD, C_KVD + GROUPS * HD, dmask)
    tme = min(256, bsz * t)
    a_in = _combine_nsa(o_cmp, o_sel, o_win, proj, sm, tme)
    b_in = _combine_dsa(o_dsa, proj, tme)
    y, yb = _tail(proj, xf, wts, a_in, b_in, tm, tme)
    return y, yb, proj, sm


def _layer_sample(xf, xb, bsz, t, wts, nsa2d, win2d, dsa2d, idx2d, page_table, win_table):
    rows = bsz * t
    proj = _matmul(xb, wts['in'], rows, 512)
    sm = _matmul(xb, wts['in_small'], rows, SM_W)
    n_pages = page_table.shape[1]
    t0 = n_pages * PAGE
    lpad = (n_pages + 1) * PAGE
    wbuf = win_table.shape[1] * PAGE
    kcvc = _compress(nsa2d, page_table, 0, wts['pe_page'], wts['w1'], wts['w2'], min(64, n_pages))
    sl_n, sl_d = _slopes(NSA_H), _slopes(DSA_H)
    o_cmp, selm = _nsa_select(proj, kcvc, sl_n, bsz, t, -(-(t0 + t) // CMP), t0, lpad)
    o_sel = _fa_sample("sel", proj, nsa2d, page_table, sl_n, t, t0, 0, C_QN, 1, C_KVN + KV_W, selm)
    o_win = _fa_sample("win", proj, win2d, win_table, sl_n, t, t0, t0 - wbuf, C_QN, 0, C_KVN + 2 * KV_W)
    qi = proj[:, C_QI:C_QI + IDX_H * IDX_D].reshape(bsz, t, IDX_H, IDX_D).transpose(0, 2, 1, 3)
    qf = qi.reshape(bsz * IDX_H * t, IDX_D)
    wi = sm[:, SM_IDXW:SM_IDXW + IDX_H].reshape(bsz, t, IDX_H).transpose(0, 2, 1)
    wcol = wi.reshape(bsz * IDX_H * t, 1)
    dmask = _dsa_index_sample(qf, wcol, idx2d, page_table, sm, t, t0)
    o_dsa = _fa_sample("dsa", proj, dsa2d, page_table, sl_d, t, t0, 0, C_QD, 0, C_KVD, dmask)
    a_in = _combine_nsa(o_cmp, o_sel, o_win, proj, sm, rows)
    b_in = _combine_dsa(o_dsa, proj, rows)
    y, yb = _tail(proj, xf, wts, a_in, b_in, rows, rows)
    return y, yb, proj, sm


def _layer_weights(l, w_in, cmp_pe, cmp_w1, cmp_w2, w_proj_nsa, w_proj_dsa, w_out, ln_gain, ln_bias):
    cmp_pe, cmp_w1, cmp_w2, w_proj_nsa, w_proj_dsa, w_out, ln_gain, ln_bias = (
        a[l] for a in (cmp_pe, cmp_w1, cmp_w2, w_proj_nsa, w_proj_dsa, w_out, ln_gain, ln_bias))
    pe = jnp.concatenate([cmp_pe, cmp_pe], axis=1)
    pe_page = jnp.concatenate([pe[0], pe[0], pe[1], pe[1]], axis=1)
    return {'in': _prep_w_in(w_in, l), 'in_small': _prep_w_small(w_in[l]), 'pe_page': pe_page,
            'w1': cmp_w1.reshape(2, CMP, HD, HD).astype(BF), 'w2': cmp_w2.astype(BF),
            'pa': w_proj_nsa.astype(BF), 'pd': w_proj_dsa.astype(BF), 'out': w_out.astype(BF),
            'gain': ln_gain, 'bias': ln_bias}


def _new_state(proj, sm, bsz, t):
    nsa_kv = proj[:, C_KVN:C_KVN + 2 * KV_W].reshape(bsz, t, 4, GROUPS, HD)
    win = proj[:, C_KVN + 2 * KV_W:C_KVN + 3 * KV_W].reshape(bsz, t, 2, GROUPS, HD)
    dsa_kv = proj[:, C_KVD:C_KVD + KV_W].reshape(bsz, t, 2, GROUPS, HD)
    idx_k = sm[:, 0:IDX_D].reshape(bsz, t, IDX_D)
    return nsa_kv, win, dsa_kv, idx_k


def kernel(x_prompt, x_sample, cache_nsa_kv, state_nsa_win, cache_dsa_kv, cache_dsa_idx, page_table,
           w_in, cmp_pe, cmp_w1, cmp_w2, w_proj_nsa, w_proj_dsa, w_out, ln_gain, ln_bias):
    bp, tp, _ = x_prompt.shape
    bs, ts, _ = x_sample.shape
    n_pool = cache_nsa_kv.shape[1]
    wpages = state_nsa_win.shape[2] // PAGE
    nsa2d = cache_nsa_kv.reshape(-1, 2 * KV_W)
    dsa2d = cache_dsa_kv.reshape(-1, KV_W)
    idx2d = cache_dsa_idx.reshape(-1, IDX_D)
    win2d = state_nsa_win.reshape(-1, KV_W)
    win_table = jnp.arange(bs * wpages, dtype=jnp.int32).reshape(bs, wpages)
    yp, ys = x_prompt.reshape(bp * tp, D_MODEL), x_sample.reshape(bs * ts, D_MODEL)
    ypb, ysb = yp.astype(BF), ys.astype(BF)
    outs = [[] for _ in range(8)]
    for l in range(DEPTH):
        wts = _layer_weights(l, w_in, cmp_pe, cmp_w1, cmp_w2, w_proj_nsa, w_proj_dsa, w_out, ln_gain, ln_bias)
        yp, ypb, proj_p, sm_p = _layer_prompt(yp, ypb, bp, tp, wts)
        ys, ysb, proj_s, sm_s = _layer_sample(ys, ysb, bs, ts, wts, nsa2d, win2d, dsa2d, idx2d,
                                        page_table + l * n_pool, win_table + l * bs * wpages)
        nkv_p, win_p, dkv_p, idx_p = _new_state(proj_p, sm_p, bp, tp)
        nkv_s, win_s, dkv_s, idx_s = _new_state(proj_s, sm_s, bs, ts)
        win_all = jnp.concatenate([state_nsa_win[l], win_s], axis=1)
        keep_p, keep_s = min(WINDOW, tp), min(WINDOW, win_all.shape[1])
        for lst, val in zip(outs, (nkv_p, nkv_s, win_p[:, tp - keep_p:], win_all[:, win_all.shape[1] - keep_s:],
                                   dkv_p, dkv_s, idx_p, idx_s)):
            lst.append(val)
    return (yp.reshape(bp, tp, D_MODEL), ys.reshape(bs, ts, D_MODEL)) + tuple(jnp.stack(o) for o in outs)
```

```python
import functools

import jax
import jax.numpy as jnp
from jax import lax
from jax.experimental import pallas as pl
from jax.experimental.pallas import tpu as pltpu

D_MODEL = 4096
DEPTH = 2
PAGE = 128
HD = 128
NSA_H = 16
DSA_H = 16
GROUPS = 2
RH = NSA_H // GROUPS
CMP = 64
CMP_SHIFT = 6
N_SEL = 16
WINDOW = 512
IDX_H = 32
IDX_D = 64
IDX_TOPK = 256
LN_EPS = 1e-5
ALPHA = (2 * DEPTH) ** 0.25
NEG = -1e30
SEL_FORCE = 1e4
SCALE = HD ** -0.5
LOG2E = 1.4426950408889634
INT_MIN = -(2 ** 31)

IN_SPLITS = (('nsa_q', 2048), ('nsa_kv', 1536), ('nsa_gate', 48), ('nsa_z', 2048), ('dsa_q', 2048),
             ('dsa_kv', 512), ('idx_q', 2048), ('idx_w', 32), ('idx_k', 64), ('dsa_z', 2048), ('merge', 8192))

C_QN, C_ZN, C_QD, C_ZD, C_QI, C_MG, C_KVN, C_KVD = 0, 2048, 4096, 6144, 8192, 10240, 18432, 19968
PROJ_W = 20480
PROJ_LAYOUT = (('nsa_q', C_QN), ('nsa_z', C_ZN), ('dsa_q', C_QD), ('dsa_z', C_ZD), ('idx_q', C_QI),
               ('merge', C_MG), ('nsa_kv', C_KVN), ('dsa_kv', C_KVD))
SM_W = 256
SM_IDXW = 64
LANE = 128
KV_W = 2 * GROUPS * HD

VMEM_LIMIT = 48 * 1024 * 1024
BF = jnp.bfloat16
F32 = jnp.float32


def _cp(sem):
    return pltpu.CompilerParams(dimension_semantics=sem, vmem_limit_bytes=VMEM_LIMIT)


def _dot_nt(a, b):
    return lax.dot_general(a, b, (((1,), (1,)), ((), ())), preferred_element_type=F32)


def _dot(a, b):
    return jnp.dot(a, b, preferred_element_type=F32)


def _mm_kernel(x_ref, w_ref, o_ref):
    o_ref[...] = _dot(x_ref[...], w_ref[...]).astype(o_ref.dtype)


def _matmul(x, w, tm, tn, out_dtype=F32):
    m, k = x.shape
    n = w.shape[1]
    return pl.pallas_call(
        _mm_kernel, grid=(m // tm, n // tn),
        in_specs=[pl.BlockSpec((tm, k), lambda i, j: (i, 0)), pl.BlockSpec((k, tn), lambda i, j: (0, j))],
        out_specs=pl.BlockSpec((tm, tn), lambda i, j: (i, j)),
        out_shape=jax.ShapeDtypeStruct((m, n), out_dtype),
        compiler_params=_cp(("parallel", "parallel")), name="matmul")(x, w)


def _gated_mm_kernel(a_ref, b_ref, wa_ref, wb_ref, m0_ref, m1_ref, o_ref):
    a = _dot(a_ref[...], wa_ref[...])
    b = _dot(b_ref[...], wb_ref[...])
    o_ref[...] = (jax.nn.sigmoid(m0_ref[...]) * a + jax.nn.sigmoid(m1_ref[...]) * b).astype(o_ref.dtype)


def _gated_matmul(a_in, b_in, wa, wb, proj, tm, tn):
    m, k = a_in.shape
    n = wa.shape[1]
    c0, c1 = C_MG // tn, (C_MG + D_MODEL) // tn
    return pl.pallas_call(
        _gated_mm_kernel, grid=(m // tm, n // tn),
        in_specs=[pl.BlockSpec((tm, k), lambda i, j: (i, 0)), pl.BlockSpec((tm, k), lambda i, j: (i, 0)),
                  pl.BlockSpec((k, tn), lambda i, j: (0, j)), pl.BlockSpec((k, tn), lambda i, j: (0, j)),
                  pl.BlockSpec((tm, tn), lambda i, j: (i, c0 + j)), pl.BlockSpec((tm, tn), lambda i, j: (i, c1 + j))],
        out_specs=pl.BlockSpec((tm, tn), lambda i, j: (i, j)),
        out_shape=jax.ShapeDtypeStruct((m, n), BF),
        compiler_params=_cp(("parallel", "parallel")), name="gated_matmul")(a_in, b_in, wa, wb, proj, proj)


def _ln_kernel(x_ref, h_ref, g_ref, b_ref, y_ref, yb_ref):
    v = ALPHA * x_ref[...] + h_ref[...]
    mu = jnp.mean(v, axis=-1, keepdims=True)
    c = v - mu
    var = jnp.mean(c * c, axis=-1, keepdims=True)
    y = c * lax.rsqrt(var + LN_EPS) * g_ref[...] + b_ref[...]
    y_ref[...] = y
    yb_ref[...] = y.astype(BF)


def _residual_ln(x, h, gain, bias, tm):
    m, n = x.shape
    row = pl.BlockSpec((tm, n), lambda i: (i, 0))
    vec = pl.BlockSpec((1, n), lambda i: (0, 0))
    return pl.pallas_call(
        _ln_kernel, grid=(m // tm,), in_specs=[row, row, vec, vec], out_specs=[row, row],
        out_shape=[jax.ShapeDtypeStruct((m, n), F32), jax.ShapeDtypeStruct((m, n), BF)],
        compiler_params=_cp(("parallel",)), name="residual_ln")(x, h, gain.reshape(1, n), bias.reshape(1, n))


def _gelu_tanh(x):
    return 0.5 * x * (1.0 + jnp.tanh(0.7978845608028654 * (x + 0.044715 * (x * x * x))))


def _compress_kernel(pt_ref, *refs, pg, pps, rpt):
    page_refs = refs[:pps]
    pe_ref, w1_ref, w2_ref, o_ref, slab_ref = refs[pps:]
    p = pl.program_id(2)
    for k, x_ref in enumerate(page_refs):
        for cg in range(2 * GROUPS):
            x = x_ref[pl.ds(cg, PAGE, stride=rpt), :] if rpt else x_ref[:, cg * HD:(cg + 1) * HD]
            slab_ref[cg, pl.ds(pl.multiple_of((p * pps + k) * PAGE, PAGE), PAGE), :] = (
                x + pe_ref[:, cg * HD:(cg + 1) * HD])

    @pl.when(p == pg // pps - 1)
    def _():
        nblk = pg * (PAGE // CMP)
        for c in range(2):
            acc = jnp.zeros((GROUPS * nblk, HD), F32)
            for tok in range(CMP):
                parts = [slab_ref[c * GROUPS + g, pl.ds(tok, nblk, stride=CMP), :] for g in range(GROUPS)]
                lhs = jnp.concatenate(parts, axis=0).astype(BF)
                acc = acc + _dot(lhs, w1_ref[c, tok])
            out = _dot(_gelu_tanh(acc).astype(BF), w2_ref[c])
            for g in range(GROUPS):
                o_ref[c, g] = out[g * nblk:(g + 1) * nblk]


def _compress(src, table, col_blk, rpt, pe_page, w1, w2, pg, pps):
    b, n_pages = table.shape
    ns = n_pages // pg
    nblk = pg * (PAGE // CMP)
    kern = functools.partial(_compress_kernel, pg=pg, pps=pps, rpt=rpt)
    if rpt:
        page_specs = [pl.BlockSpec((PAGE * rpt, HD), lambda bi, s, p, pt, k=k: (pt[bi, s * pg + p * pps + k], 0))
                      for k in range(pps)]
    else:
        page_specs = [pl.BlockSpec((PAGE, 4 * HD), lambda bi, s, p, pt, k=k: (pt[bi, s * pg + p * pps + k], col_blk))
                      for k in range(pps)]
    gs = pltpu.PrefetchScalarGridSpec(
        num_scalar_prefetch=1, grid=(b, ns, pg // pps),
        in_specs=page_specs + [pl.BlockSpec((PAGE, 4 * HD), lambda bi, s, p, pt: (0, 0)),
                               pl.BlockSpec((2, CMP, HD, HD), lambda bi, s, p, pt: (0, 0, 0, 0)),
                               pl.BlockSpec((2, HD, HD), lambda bi, s, p, pt: (0, 0, 0))],
        out_specs=pl.BlockSpec((None, 2, GROUPS, nblk, HD), lambda bi, s, p, pt: (bi, 0, 0, s, 0)),
        scratch_shapes=[pltpu.VMEM((2 * GROUPS, pg * PAGE, HD), F32)])
    return pl.pallas_call(
        kern, grid_spec=gs, out_shape=jax.ShapeDtypeStruct((b, 2, GROUPS, ns * nblk, HD), F32),
        compiler_params=_cp(("parallel", "arbitrary", "arbitrary")), name="compress")(
            table, *([src] * pps), pe_page, w1, w2)


def _sortable(x):
    b = lax.bitcast_convert_type(x, jnp.int32)
    return jnp.where(b < 0, b ^ jnp.int32(0x7FFFFFFF), b)


def _count(pred):
    return jnp.sum(jnp.where(pred, 1.0, 0.0), axis=-1, keepdims=True)


def _topk_mask(key_ref, k, idx, idx_bits):
    rows = key_ref.shape[0]
    kf = float(k)
    zero = jnp.zeros((rows, 1), jnp.int32)
    t0 = jnp.where(_count(key_ref[...] >= zero) >= kf, zero, jnp.full((rows, 1), INT_MIN, jnp.int32))

    def value_bit(i, t):
        cand = t | jnp.left_shift(jnp.int32(1), jnp.int32(30) - i)
        return jnp.where(_count(key_ref[...] >= cand) >= kf, cand, t)

    t = lax.fori_loop(0, 31, value_bit, t0)
    keys = key_ref[...]
    need = kf - _count(keys > t)

    def index_bit(i, c):
        cand = c | jnp.left_shift(jnp.int32(1), jnp.int32(idx_bits - 1) - i)
        below = _count((key_ref[...] == t) & (idx < cand))
        return jnp.where(below < need, cand, c)

    c0 = lax.fori_loop(0, idx_bits, index_bit, zero)
    return (keys > t) | ((keys == t) & (idx <= c0))


def _masked_softmax(s, mask):
    s = jnp.where(mask, s, NEG)
    e = jnp.where(mask, jnp.exp(s - jnp.max(s, axis=-1, keepdims=True)), 0.0)
    return e / jnp.maximum(jnp.sum(e, axis=-1, keepdims=True), 1e-30)


def _nsa_select_kernel(slopes_ref, q_ref, kc_ref, vc_ref, ocmp_ref, sel_ref, key_scr,
                       *, qb, nb, nbs, nbs_pad, t0):
    g = pl.program_id(1)
    i = pl.program_id(2)
    tq = t0 + i * qb + lax.broadcasted_iota(jnp.int32, (qb, 1), 0)
    blk_end = lax.broadcasted_iota(jnp.int32, (1, nb), 1) * CMP + (CMP - 1)
    valid = blk_end <= tq
    distf = (tq - blk_end).astype(F32)
    kc = kc_ref[...].astype(BF)
    vc = vc_ref[...].astype(BF)
    imp = jnp.zeros((qb, nb), F32)
    for r in range(RH):
        q = q_ref[:, r * HD:(r + 1) * HD].astype(BF)
        s = _dot_nt(q, kc) * SCALE - slopes_ref[g * RH + r] * distf
        p = _masked_softmax(s, valid)
        ocmp_ref[:, r * HD:(r + 1) * HD] = _dot(p.astype(BF), vc)
        imp = imp + p
    if nbs_pad > nb:
        imp = jnp.concatenate([imp, jnp.zeros((qb, nbs_pad - nb), F32)], axis=1)
    j = lax.broadcasted_iota(jnp.int32, (1, nbs_pad), 1)
    cur = jnp.right_shift(tq, CMP_SHIFT)
    forced = (j == 0) | (j == cur) | (j == cur - 1)
    score = jnp.where(j <= cur, jnp.where(forced, SEL_FORCE, imp), -SEL_FORCE)
    key_scr[...] = _sortable(score)
    sel = _topk_mask(key_scr, min(N_SEL, nbs), j, max(1, (nbs_pad - 1).bit_length()))
    selb = jnp.where(sel, 1.0, 0.0).astype(BF)
    n_col = lax.broadcasted_iota(jnp.int32, (nbs_pad, 1), 0)
    lane = lax.broadcasted_iota(jnp.int32, (1, LANE), 1)

    def chunk(c, carry):
        expand = jnp.where(n_col == jnp.right_shift(c * LANE + lane, CMP_SHIFT), 1.0, 0.0).astype(BF)
        sel_ref[:, pl.ds(pl.multiple_of(c * LANE, LANE), LANE)] = ((_dot(selb, expand) - 1.0) * (-NEG)).astype(BF)
        return carry

    lax.fori_loop(0, sel_ref.shape[-1] // LANE, chunk, 0)


def _nsa_select_t_kernel(slopes_ref, q_ref, kc_ref, vc_ref, ocmp_ref, sel_ref, *, qb, nb, n_sel, t0):
    g = pl.program_id(1)
    i = pl.program_id(2)
    n_col = lax.broadcasted_iota(jnp.int32, (nb, qb), 0)
    tq = t0 + i * qb + lax.broadcasted_iota(jnp.int32, (nb, qb), 1)
    blk_end = n_col * CMP + (CMP - 1)
    valid = blk_end <= tq
    distf = (tq - blk_end).astype(F32)
    kc = kc_ref[...].astype(BF)
    vct = vc_ref[...].T.astype(BF)
    imp = jnp.zeros((nb, qb), F32)
    for r in range(RH):
        q = q_ref[:, r * HD:(r + 1) * HD].astype(BF)
        s = jnp.where(valid, _dot_nt(kc, q) * SCALE - slopes_ref[g * RH + r] * distf, NEG)
        e = jnp.where(valid, jnp.exp(s - jnp.max(s, axis=0, keepdims=True)), 0.0)
        p = e / jnp.maximum(jnp.sum(e, axis=0, keepdims=True), 1e-30)
        ocmp_ref[:, r * HD:(r + 1) * HD] = _dot(vct, p.astype(BF)).T
        imp = imp + p
    cur = jnp.right_shift(tq, CMP_SHIFT)
    forced = (n_col == 0) | (n_col == cur) | (n_col == cur - 1)
    score = jnp.where(n_col <= cur, jnp.where(forced, SEL_FORCE, imp), -SEL_FORCE)
    rank = jnp.zeros((nb, qb), F32)
    for a in range(nb):
        row = score[a:a + 1, :]
        rank = rank + jnp.where((row > score) | ((row == score) & (n_col > a)), 1.0, 0.0)
    bias_t = jnp.where(rank < float(n_sel), 0.0, NEG)
    for n in range(nb):
        sel_ref[n] = jnp.broadcast_to(bias_t[n:n + 1, :], (8, qb))


def _nsa_select(proj, kcvc, slopes, bsz, t, nbs, t0, key_len):
    qb = min(t, LANE)
    nq = t // qb
    nb = kcvc.shape[3]
    nbs_pad = nbs if nbs == nb else -(-nbs // LANE) * LANE
    if key_len:
        kern = functools.partial(_nsa_select_kernel, qb=qb, nb=nb, nbs=nbs, nbs_pad=nbs_pad, t0=t0)
        scratch = [pltpu.VMEM((qb, nbs_pad), jnp.int32)]
    else:
        assert nbs == nb
        kern = functools.partial(_nsa_select_t_kernel, qb=qb, nb=nb, n_sel=min(N_SEL, nbs), t0=t0)
        scratch = []
    wq = RH * HD
    if key_len:
        sel_spec = pl.BlockSpec((None, None, qb, key_len), lambda b, g, i: (b, g, i, 0))
        sel_shape = jax.ShapeDtypeStruct((bsz, GROUPS, t, key_len), BF)
    else:
        sel_spec = pl.BlockSpec((None, None, nbs, 8, qb), lambda b, g, i: (b, g, 0, 0, i))
        sel_shape = jax.ShapeDtypeStruct((bsz, GROUPS, nbs, 8, t), F32)
    return pl.pallas_call(
        kern, grid=(bsz, GROUPS, nq),
        in_specs=[pl.BlockSpec(memory_space=pltpu.SMEM),
                  pl.BlockSpec((qb, wq), lambda b, g, i: (b * nq + i, C_QN // wq + g)),
                  pl.BlockSpec((None, None, None, nb, HD), lambda b, g, i: (b, 0, g, 0, 0)),
                  pl.BlockSpec((None, None, None, nb, HD), lambda b, g, i: (b, 1, g, 0, 0))],
        out_specs=[pl.BlockSpec((qb, wq), lambda b, g, i: (b * nq + i, g)), sel_spec],
        out_shape=[jax.ShapeDtypeStruct((bsz * t, NSA_H * HD), F32), sel_shape],
        scratch_shapes=scratch,
        compiler_params=_cp(("parallel", "parallel", "parallel")), name="nsa_select")(slopes, proj, kcvc, kcvc)


def _fa_prompt_kernel(*refs, mode, qb, kb):
    if mode == "win":
        slopes_ref, q_ref, k_ref, v_ref, o_ref = refs[:5]
        mask_ref = None
    else:
        slopes_ref, q_ref, k_ref, v_ref, mask_ref, o_ref = refs[:6]
    qs_scr, bias_scr, s_scr, p_scr, mb_scr, m_scr, a_scr, acc_scr = refs[-8:]
    g = pl.program_id(1)
    i = pl.program_id(2)
    k_local = lax.broadcasted_iota(jnp.int32, (kb, qb), 0)
    q_local = lax.broadcasted_iota(jnp.int32, (kb, qb), 1)
    for r in range(RH):
        qs_scr[r * qb:(r + 1) * qb, :] = (q_ref[:, r * HD:(r + 1) * HD] * (SCALE * LOG2E)).astype(BF)

    @pl.when(i == 0)
    def _():
        for r in range(RH):
            bias_scr[:, r * qb:(r + 1) * qb] = (slopes_ref[g * RH + r] * LOG2E) * k_local.astype(F32)

    m_scr[...] = jnp.full(m_scr.shape, NEG, F32)
    acc_scr[...] = jnp.zeros(acc_scr.shape, F32)
    j_hi = ((i + 1) * qb - 1) // kb
    j_lo = jnp.maximum(i * qb - (WINDOW - 1), 0) // kb if mode == "win" else 0

    def body(j, carry):
        off = pl.multiple_of(j * kb, kb)
        s_scr[...] = _dot_nt(k_ref[pl.ds(off, kb), :].astype(BF), qs_scr[...])
        dist = (i * qb + q_local) - (j * kb + k_local)
        ok = dist >= 0
        if mode == "win":
            ok = ok & (dist < WINDOW)
        mb = jnp.where(ok, 0.0, NEG)
        if mode == "sel":
            tiles = mask_ref[pl.ds(j * (kb // CMP), kb // CMP)]
            mb = mb + jnp.concatenate([jnp.tile(tiles[n], (CMP // 8, 1)) for n in range(kb // CMP)], axis=0)
        elif mode == "dsa":
            mb = mb + mask_ref[pl.ds(off, kb), :].astype(F32)
        mb_scr[...] = mb
        cbase = (j * kb - i * qb).astype(F32)
        for r in range(RH):
            sl = slice(r * qb, (r + 1) * qb)
            c = (slopes_ref[g * RH + r] * LOG2E) * cbase
            x = s_scr[:, sl] + bias_scr[:, sl] + mb_scr[...]
            m_prev = m_scr[:, sl]
            m_new = jnp.maximum(m_prev, jnp.max(x, axis=0, keepdims=True) + c)
            p_scr[:, sl] = jnp.exp2(x - (m_new - c)).astype(BF)
            a_scr[:, sl] = jnp.exp2(m_prev - m_new)
            m_scr[:, sl] = m_new
        vt = jnp.concatenate([v_ref[pl.ds(off, kb), :].T, jnp.ones((16, kb), F32)], axis=0).astype(BF)
        acc_scr[...] = acc_scr[...] * a_scr[...] + _dot(vt, p_scr[...])
        return carry

    lax.fori_loop(j_lo, j_hi + 1, body, 0)
    for r in range(RH):
        sl = slice(r * qb, (r + 1) * qb)
        o_t = jnp.where(m_scr[:, sl] > 0.5 * NEG,
                        acc_scr[0:HD, sl] / jnp.maximum(acc_scr[HD:HD + 1, sl], 1e-30), 0.0)
        o_ref[:, r * HD:(r + 1) * HD] = o_t.T


def _fa_prompt(mode, proj, slopes, bsz, t, q_col, k_col, v_col, mask=None):
    qb = LANE
    kb = min(2 * LANE, t)
    nq = t // qb
    wq = RH * HD
    in_specs = [pl.BlockSpec(memory_space=pltpu.SMEM),
                pl.BlockSpec((qb, wq), lambda b, g, i: (b * nq + i, q_col // wq + g)),
                pl.BlockSpec((t, HD), lambda b, g, i: (b, k_col // HD + g)),
                pl.BlockSpec((t, HD), lambda b, g, i: (b, v_col // HD + g))]
    args = [slopes, proj, proj, proj]
    if mode == "sel":
        in_specs.append(pl.BlockSpec((None, None, t // CMP, 8, qb), lambda b, g, i: (b, g, 0, 0, i)))
        args.append(mask)
    elif mode == "dsa":
        in_specs.append(pl.BlockSpec((None, t, qb), lambda b, g, i: (b, 0, i)))
        args.append(mask)
    kern = functools.partial(_fa_prompt_kernel, mode=mode, qb=qb, kb=kb)
    lanes = RH * qb
    return pl.pallas_call(
        kern, grid=(bsz, GROUPS, nq), in_specs=in_specs,
        out_specs=pl.BlockSpec((qb, wq), lambda b, g, i: (b * nq + i, g)),
        out_shape=jax.ShapeDtypeStruct((bsz * t, GROUPS * wq), F32),
        scratch_shapes=[pltpu.VMEM((lanes, HD), BF), pltpu.VMEM((kb, lanes), F32), pltpu.VMEM((kb, lanes), F32),
                        pltpu.VMEM((kb, lanes), BF), pltpu.VMEM((kb, qb), F32), pltpu.VMEM((1, lanes), F32),
                        pltpu.VMEM((1, lanes), F32), pltpu.VMEM((HD + 16, lanes), F32)],
        compiler_params=_cp(("parallel", "parallel", "arbitrary")), name="fa_prompt_" + mode)(*args)


PPS = 4


def _fa_sample_kernel(*refs, mode, t, nsteps, t0, kpos0, rpt, kcomp):
    pt_ref, slope_ref, q_ref = refs[:3]
    page_refs = refs[3:3 + PPS]
    new_ref = refs[3 + PPS]
    rest = refs[4 + PPS:]
    if mode == "win":
        mask_ref = maskn_ref = None
        o_ref, qs_scr, m_scr, l_scr, acc_scr = rest
    else:
        mask_ref, maskn_ref, o_ref, qs_scr, m_scr, l_scr, acc_scr = rest
    j = pl.program_id(1)
    rows = RH * t

    @pl.when(j == 0)
    def _():
        for g in range(GROUPS):
            qs_scr[g] = jnp.concatenate(
                [q_ref[:, (g * RH + r) * HD:(g * RH + r + 1) * HD] for r in range(RH)], axis=0).astype(BF)
        m_scr[...] = jnp.full(m_scr.shape, NEG, F32)
        l_scr[...] = jnp.zeros(l_scr.shape, F32)
        acc_scr[...] = jnp.zeros(acc_scr.shape, F32)

    tq = t0 + jnp.concatenate([lax.broadcasted_iota(jnp.int32, (t, 1), 0)] * RH, axis=0)

    def step(kparts, vparts, kpos, extra_ok, mref):
        dist = tq - kpos
        ok = dist >= 0
        if mode == "win":
            ok = ok & (dist < WINDOW)
        if extra_ok is not None:
            ok = ok & extra_ok
        distf = dist.astype(F32)
        for g in range(GROUPS):
            k = jnp.concatenate(kparts[g], axis=0).astype(BF)
            v = jnp.concatenate(vparts[g], axis=0).astype(BF)
            s = _dot_nt(qs_scr[g], k) * SCALE - slope_ref[g] * distf
            s = jnp.where(ok, s, NEG)
            if mref is not None:
                mval = mref[g] if mode == "sel" else mref[...]
                s = s + jnp.tile(mval.astype(F32), (RH, 1))
            m_prev = m_scr[g]
            m_new = jnp.maximum(m_prev, jnp.max(s, axis=-1, keepdims=True))
            a = jnp.exp(m_prev - m_new)
            p = jnp.exp(s - m_new)
            l_scr[g] = a * l_scr[g] + jnp.sum(p, axis=-1, keepdims=True)
            acc_scr[g] = a * acc_scr[g] + _dot(p.astype(BF), v)
            m_scr[g] = m_new

    lane = lax.broadcasted_iota(jnp.int32, (1, PPS * PAGE), 1)
    step([[r[pl.ds(kcomp * GROUPS + g, PAGE, stride=rpt), :] for r in page_refs] for g in range(GROUPS)],
         [[r[pl.ds((kcomp + 1) * GROUPS + g, PAGE, stride=rpt), :] for r in page_refs] for g in range(GROUPS)],
         kpos0 + j * (PPS * PAGE) + lane, None, mask_ref)

    @pl.when(j == nsteps - 1)
    def _():
        lane1 = lax.broadcasted_iota(jnp.int32, (1, PAGE), 1)
        kv_new = jnp.concatenate([new_ref[...], jnp.zeros((PAGE - t, KV_W), F32)], axis=0)
        step([[kv_new[:, g * HD:(g + 1) * HD]] for g in range(GROUPS)],
             [[kv_new[:, (GROUPS + g) * HD:(GROUPS + g + 1) * HD]] for g in range(GROUPS)],
             t0 + lane1, lane1 < t, maskn_ref)
        for g in range(GROUPS):
            o = jnp.where(m_scr[g] > 0.5 * NEG, acc_scr[g] / jnp.maximum(l_scr[g], 1e-30), 0.0)
            for r in range(RH):
                o_ref[:, (g * RH + r) * HD:(g * RH + r + 1) * HD] = o[r * t:(r + 1) * t]


def _fa_sample(mode, proj, past, table, slopes, t, t0, kpos0, q_col, rpt, kcomp, new_col, mask=None):
    bsz, nkb = table.shape
    nsteps = nkb // PPS
    rows = RH * t
    wq = GROUPS * RH * HD
    slope_col = jnp.repeat(slopes.reshape(GROUPS, RH), t, axis=1).reshape(GROUPS, rows, 1)
    in_specs = [pl.BlockSpec((GROUPS, rows, 1), lambda b, j, pt: (0, 0, 0)),
                pl.BlockSpec((t, wq), lambda b, j, pt: (b, q_col // wq))]
    for p in range(PPS):
        in_specs.append(pl.BlockSpec((PAGE * rpt, HD), lambda b, j, pt, p=p: (pt[b, j * PPS + p], 0)))
    in_specs.append(pl.BlockSpec((t, KV_W), lambda b, j, pt: (b, new_col // KV_W)))
    args = [slope_col, proj] + [past] * PPS + [proj]
    if mode == "sel":
        in_specs.append(pl.BlockSpec((None, GROUPS, t, PPS * PAGE), lambda b, j, pt: (b, 0, 0, j)))
        in_specs.append(pl.BlockSpec((None, GROUPS, t, PAGE), lambda b, j, pt: (b, 0, 0, nkb)))
        args += [mask, mask]
    elif mode == "dsa":
        in_specs.append(pl.BlockSpec((None, t, PPS * PAGE), lambda b, j, pt: (b, 0, j)))
        in_specs.append(pl.BlockSpec((None, t, PAGE), lambda b, j, pt: (b, 0, nkb)))
        args += [mask, mask]
    kern = functools.partial(_fa_sample_kernel, mode=mode, t=t, nsteps=nsteps, t0=t0, kpos0=kpos0, rpt=rpt, kcomp=kcomp)
    gs = pltpu.PrefetchScalarGridSpec(
        num_scalar_prefetch=1, grid=(bsz, nsteps), in_specs=in_specs,
        out_specs=pl.BlockSpec((t, wq), lambda b, j, pt: (b, 0)),
        scratch_shapes=[pltpu.VMEM((GROUPS, rows, HD), BF), pltpu.VMEM((GROUPS, rows, 1), F32),
                        pltpu.VMEM((GROUPS, rows, 1), F32), pltpu.VMEM((GROUPS, rows, HD), F32)])
    return pl.pallas_call(
        kern, grid_spec=gs, out_shape=jax.ShapeDtypeStruct((bsz * t, wq), F32),
        compiler_params=_cp(("parallel", "arbitrary")), name="fa_sample_" + mode)(table, *args)


IDX_CHUNK = 256


def _dsa_index_prompt_kernel(q_ref, sq_ref, sk_ref, mask_ref, score_scr, key_scr, *, qb, t, top):
    i = pl.program_id(1)
    tq = i * qb + lax.broadcasted_iota(jnp.int32, (qb, 1), 0)
    w = sq_ref[:, SM_IDXW:SM_IDXW + IDX_H] * (IDX_H ** -0.5)
    score_scr[...] = jnp.zeros(score_scr.shape, F32)
    ch = min(IDX_CHUNK, t)

    def chunk(c, carry):
        off = pl.multiple_of(c * ch, ch)
        kk = sk_ref[pl.ds(off, ch), 0:IDX_D].astype(BF)
        acc = jnp.zeros((qb, ch), F32)
        for h in range(IDX_H):
            qh = q_ref[:, h * IDX_D:(h + 1) * IDX_D].astype(BF)
            acc = acc + w[:, h:h + 1] * jnp.maximum(_dot_nt(qh, kk) * (IDX_D ** -0.5), 0.0)
        score_scr[:, pl.ds(off, ch)] = acc
        return carry

    lax.fori_loop(0, ((i + 1) * qb + ch - 1) // ch, chunk, 0)
    s_pos = lax.broadcasted_iota(jnp.int32, (1, t), 1)
    causal = s_pos <= tq
    key_scr[...] = _sortable(jnp.where(causal, score_scr[...], NEG))
    sel = _topk_mask(key_scr, top, s_pos, max(1, (t - 1).bit_length()))
    bias = jnp.where(sel & causal, 0.0, NEG)
    for c in range(t // LANE):
        mask_ref[c * LANE:(c + 1) * LANE, :] = bias[:, c * LANE:(c + 1) * LANE].T.astype(BF)


def _dsa_index_prompt(proj, sm, bsz, t):
    qb = LANE
    nq = t // qb
    top = min(IDX_TOPK, t // 4)
    kern = functools.partial(_dsa_index_prompt_kernel, qb=qb, t=t, top=top)
    wq = IDX_H * IDX_D
    return pl.pallas_call(
        kern, grid=(bsz, nq),
        in_specs=[pl.BlockSpec((qb, wq), lambda b, i: (b * nq + i, C_QI // wq)),
                  pl.BlockSpec((qb, LANE), lambda b, i: (b * nq + i, 0)),
                  pl.BlockSpec((t, LANE), lambda b, i: (b, 0))],
        out_specs=pl.BlockSpec((None, t, qb), lambda b, i: (b, 0, i)),
        out_shape=jax.ShapeDtypeStruct((bsz, t, t), BF),
        scratch_shapes=[pltpu.VMEM((qb, t), F32), pltpu.VMEM((qb, t), jnp.int32)],
        compiler_params=_cp(("parallel", "parallel")), name="dsa_index_prompt")(proj, sm, sm)


def _dsa_index_sample_kernel(pt_ref, qf_ref, wcol_ref, *refs, t, nsteps, t0, lpad, top):
    page_refs = refs[:PPS]
    sn_ref, mask_ref, score_scr, key_scr = refs[PPS:]
    j = pl.program_id(1)

    @pl.when(j == 0)
    def _():
        score_scr[...] = jnp.full(score_scr.shape, NEG, F32)

    qf = qf_ref[...].astype(BF)
    wcol = wcol_ref[...] * (IDX_H ** -0.5)

    def scores(kk):
        r = jnp.maximum(_dot_nt(qf, kk) * (IDX_D ** -0.5), 0.0) * wcol
        sc = r[0:t]
        for h in range(1, IDX_H):
            sc = sc + r[h * t:(h + 1) * t]
        return sc

    kpages = jnp.concatenate([r[...] for r in page_refs], axis=0).astype(BF)
    score_scr[:, pl.ds(pl.multiple_of(j * (PPS * PAGE), PPS * PAGE), PPS * PAGE)] = scores(kpages)

    @pl.when(j == nsteps - 1)
    def _():
        tq = t0 + lax.broadcasted_iota(jnp.int32, (t, 1), 0)
        lane = lax.broadcasted_iota(jnp.int32, (1, PAGE), 1)
        kn = jnp.concatenate([sn_ref[:, 0:IDX_D], jnp.zeros((PAGE - t, IDX_D), F32)], axis=0).astype(BF)
        ok = (lane < t) & (t0 + lane <= tq)
        score_scr[:, lpad - PAGE:lpad] = jnp.where(ok, scores(kn), NEG)
        s_pos = lax.broadcasted_iota(jnp.int32, (1, lpad), 1)
        key_scr[...] = _sortable(score_scr[...])
        sel = _topk_mask(key_scr, top, s_pos, max(1, (lpad - 1).bit_length()))
        mask_ref[...] = jnp.where(sel & (s_pos <= tq) & (s_pos < t0 + t), 0.0, NEG).astype(BF)


def _dsa_index_sample(qf, wcol, idx2d, table, sm, t, t0):
    bsz, nkb = table.shape
    nsteps = nkb // PPS
    lpad = (nkb + 1) * PAGE
    top = min(IDX_TOPK, (t0 + t) // 4)
    rows = IDX_H * t
    kern = functools.partial(_dsa_index_sample_kernel, t=t, nsteps=nsteps, t0=t0, lpad=lpad, top=top)
    gs = pltpu.PrefetchScalarGridSpec(
        num_scalar_prefetch=1, grid=(bsz, nsteps),
        in_specs=[pl.BlockSpec((rows, IDX_D), lambda b, j, pt: (b, 0)),
                  pl.BlockSpec((rows, 1), lambda b, j, pt: (b, 0))]
        + [pl.BlockSpec((PAGE, IDX_D), lambda b, j, pt, p=p: (pt[b, j * PPS + p], 0)) for p in range(PPS)]
        + [pl.BlockSpec((t, LANE), lambda b, j, pt: (b, 0))],
        out_specs=pl.BlockSpec((None, t, lpad), lambda b, j, pt: (b, 0, 0)),
        scratch_shapes=[pltpu.VMEM((t, lpad), F32), pltpu.VMEM((t, lpad), jnp.int32)])
    return pl.pallas_call(
        kern, grid_spec=gs, out_shape=jax.ShapeDtypeStruct((bsz, t, lpad), BF),
        compiler_params=_cp(("parallel", "arbitrary")), name="dsa_index_sample")(
            table, qf, wcol, *([idx2d] * PPS), sm)


def _silu(z):
    return z * jax.nn.sigmoid(z)


def _combine_nsa_kernel(oc_ref, os_ref, ow_ref, gate_ref, z_ref, o_ref):
    gate = jax.nn.sigmoid(gate_ref[:, 0:3 * NSA_H])
    for h in range(NSA_H):
        sl = slice(h * HD, (h + 1) * HD)
        o = (gate[:, 3 * h:3 * h + 1] * oc_ref[:, sl] + gate[:, 3 * h + 1:3 * h + 2] * os_ref[:, sl]
             + gate[:, 3 * h + 2:3 * h + 3] * ow_ref[:, sl])
        o_ref[:, sl] = (o * _silu(z_ref[:, sl])).astype(BF)


def _combine_nsa(o_cmp, o_sel, o_win, proj, sm, tm):
    m, n = o_cmp.shape
    row = pl.BlockSpec((tm, n), lambda i: (i, 0))
    return pl.pallas_call(
        _combine_nsa_kernel, grid=(m // tm,),
        in_specs=[row, row, row, pl.BlockSpec((tm, LANE), lambda i: (i, 1)),
                  pl.BlockSpec((tm, n), lambda i: (i, C_ZN // n))],
        out_specs=row, out_shape=jax.ShapeDtypeStruct((m, n), BF),
        compiler_params=_cp(("parallel",)), name="combine_nsa")(o_cmp, o_sel, o_win, sm, proj)


def _combine_dsa_kernel(o_ref_in, z_ref, o_ref):
    o_ref[...] = (o_ref_in[...] * _silu(z_ref[...])).astype(BF)


def _combine_dsa(o, proj, tm):
    m, n = o.shape
    row = pl.BlockSpec((tm, n), lambda i: (i, 0))
    return pl.pallas_call(
        _combine_dsa_kernel, grid=(m // tm,),
        in_specs=[row, pl.BlockSpec((tm, n), lambda i: (i, C_ZD // n))],
        out_specs=row, out_shape=jax.ShapeDtypeStruct((m, n), BF),
        compiler_params=_cp(("parallel",)), name="combine_dsa")(o, proj)


PREP_TN = 512
PREP_TK = 1024


def _prep_tables():
    src, off = {}, 0
    for name, width in IN_SPLITS:
        src[name] = off
        off += width
    width = dict(IN_SPLITS)
    shifts, base, cls = [], [], []
    for name, _ in PROJ_LAYOUT:
        for c in range(width[name] // PREP_TN):
            s = src[name] + c * PREP_TN
            if s % LANE not in shifts:
                shifts.append(s % LANE)
            base.append(s // LANE)
            cls.append(shifts.index(s % LANE))
    return tuple(shifts), base, cls


def _prep_kernel(base_ref, cls_ref, *refs, shifts):
    o_ref = refs[-1]
    j = pl.program_id(1)
    for k, s in enumerate(shifts):
        @pl.when(cls_ref[j] == k)
        def _():
            win = jnp.concatenate([r[...] for r in refs[:-1]], axis=1)
            o_ref[...] = win[:, s:s + PREP_TN].astype(BF)


def _prep_w_in(w_in, l):
    shifts, base, cls = _prep_tables()
    k = w_in.shape[1]
    nwin = PREP_TN // LANE + 1
    kern = functools.partial(_prep_kernel, shifts=shifts)
    gs = pltpu.PrefetchScalarGridSpec(
        num_scalar_prefetch=2, grid=(k // PREP_TK, len(base)),
        in_specs=[pl.BlockSpec((None, PREP_TK, LANE), lambda i, j, bs, cs, m=m: (l, i, bs[j] + m))
                  for m in range(nwin)],
        out_specs=pl.BlockSpec((PREP_TK, PREP_TN), lambda i, j, bs, cs: (i, j)))
    return pl.pallas_call(
        kern, grid_spec=gs, out_shape=jax.ShapeDtypeStruct((k, PROJ_W), BF),
        compiler_params=_cp(("parallel", "parallel")), name="prep_w_in")(
            jnp.asarray(base, jnp.int32), jnp.asarray(cls, jnp.int32), *([w_in] * nwin))


def _prep_w_small(w):
    seg, off = {}, 0
    for name, width in IN_SPLITS:
        seg[name] = (off, width)
        off += width
    cut = lambda name: w[:, seg[name][0]:seg[name][0] + seg[name][1]]
    k = w.shape[0]
    return jnp.concatenate([cut('idx_k'), cut('idx_w'), jnp.zeros((k, 32), w.dtype), cut('nsa_gate'),
                            jnp.zeros((k, SM_W - 176), w.dtype)], axis=1).astype(BF)


def _slopes(n):
    return jnp.exp2(-8.0 * jnp.arange(1, n + 1, dtype=F32) / n)


def _tail(proj, xf, wts, a_in, b_in, tm_mm, tm_ln):
    hm = _gated_matmul(a_in, b_in, wts['pa'], wts['pd'], proj, tm_mm, 512)
    h = _matmul(hm, wts['out'], tm_mm, 512)
    return _residual_ln(xf, h, wts['gain'], wts['bias'], tm_ln)


def _layer_prompt(xf, xb, bsz, t, wts):
    tm = min(1024, bsz * t)
    proj = _matmul(xb, wts['in'], tm, 512)
    sm = _matmul(xb, wts['in_small'], tm, SM_W)
    n_pages = t // PAGE
    table = jnp.arange(bsz * n_pages, dtype=jnp.int32).reshape(bsz, n_pages)
    kcvc = _compress(proj, table, C_KVN // (4 * HD), 0, wts['pe_page'], wts['w1'], wts['w2'], n_pages, 1)
    sl_n, sl_d = _slopes(NSA_H), _slopes(DSA_H)
    o_cmp, selm = _nsa_select(proj, kcvc, sl_n, bsz, t, t // CMP, 0, 0)
    o_sel = _fa_prompt("sel", proj, sl_n, bsz, t, C_QN, C_KVN + KV_W, C_KVN + KV_W + GROUPS * HD, selm)
    o_win = _fa_prompt("win", proj, sl_n, bsz, t, C_QN, C_KVN + 2 * KV_W, C_KVN + 2 * KV_W + GROUPS * HD)
    dmask = _dsa_index_prompt(proj, sm, bsz, t)
    o_dsa = _fa_prompt("dsa", proj, sl_d, bsz, t, C_QD, C_KVD, C_KVD + GROUPS * HD, dmask)
    tme = min(256, bsz * t)
    a_in = _combine_nsa(o_cmp, o_sel, o_win, proj, sm, tme)
    b_in = _combine_dsa(o_dsa, proj, tme)
    y, yb = _tail(proj, xf, wts, a_in, b_in, tm, tme)
    return y, yb, proj, sm


def _layer_sample(xf, xb, bsz, t, wts, nsa_rows, win_rows, dsa_rows, idx2d, page_table, win_table):
    rows = bsz * t
    proj = _matmul(xb, wts['in'], rows, 512)
    sm = _matmul(xb, wts['in_small'], rows, SM_W)
    n_pages = page_table.shape[1]
    t0 = n_pages * PAGE
    lpad = (n_pages + 1) * PAGE
    wbuf = win_table.shape[1] * PAGE
    kcvc = _compress(nsa_rows, page_table, 0, 4 * GROUPS, wts['pe_page'], wts['w1'], wts['w2'], min(64, n_pages), PPS)
    sl_n, sl_d = _slopes(NSA_H), _slopes(DSA_H)
    o_cmp, selm = _nsa_select(proj, kcvc, sl_n, bsz, t, -(-(t0 + t) // CMP), t0, lpad)
    o_sel = _fa_sample("sel", proj, nsa_rows, page_table, sl_n, t, t0, 0, C_QN, 4 * GROUPS, 2, C_KVN + KV_W, selm)
    o_win = _fa_sample("win", proj, win_rows, win_table, sl_n, t, t0, t0 - wbuf, C_QN, 2 * GROUPS, 0,
                       C_KVN + 2 * KV_W)
    qi = proj[:, C_QI:C_QI + IDX_H * IDX_D].reshape(bsz, t, IDX_H, IDX_D).transpose(0, 2, 1, 3)
    qf = qi.reshape(bsz * IDX_H * t, IDX_D)
    wi = sm[:, SM_IDXW:SM_IDXW + IDX_H].reshape(bsz, t, IDX_H).transpose(0, 2, 1)
    wcol = wi.reshape(bsz * IDX_H * t, 1)
    dmask = _dsa_index_sample(qf, wcol, idx2d, page_table, sm, t, t0)
    o_dsa = _fa_sample("dsa", proj, dsa_rows, page_table, sl_d, t, t0, 0, C_QD, 2 * GROUPS, 0, C_KVD, dmask)
    a_in = _combine_nsa(o_cmp, o_sel, o_win, proj, sm, rows)
    b_in = _combine_dsa(o_dsa, proj, rows)
    y, yb = _tail(proj, xf, wts, a_in, b_in, rows, rows)
    return y, yb, proj, sm


def _layer_weights(l, w_in, cmp_pe, cmp_w1, cmp_w2, w_proj_nsa, w_proj_dsa, w_out, ln_gain, ln_bias):
    cmp_pe, cmp_w1, cmp_w2, w_proj_nsa, w_proj_dsa, w_out, ln_gain, ln_bias = (
        a[l] for a in (cmp_pe, cmp_w1, cmp_w2, w_proj_nsa, w_proj_dsa, w_out, ln_gain, ln_bias))
    pe = jnp.concatenate([cmp_pe, cmp_pe], axis=1)
    pe_page = jnp.concatenate([pe[0], pe[0], pe[1], pe[1]], axis=1)
    return {'in': _prep_w_in(w_in, l), 'in_small': _prep_w_small(w_in[l]), 'pe_page': pe_page,
            'w1': cmp_w1.reshape(2, CMP, HD, HD).astype(BF), 'w2': cmp_w2.astype(BF),
            'pa': w_proj_nsa.astype(BF), 'pd': w_proj_dsa.astype(BF), 'out': w_out.astype(BF),
            'gain': ln_gain, 'bias': ln_bias}


def _new_state(proj, sm, bsz, t):
    nsa_kv = proj[:, C_KVN:C_KVN + 2 * KV_W].reshape(bsz, t, 4, GROUPS, HD)
    win = proj[:, C_KVN + 2 * KV_W:C_KVN + 3 * KV_W].reshape(bsz, t, 2, GROUPS, HD)
    dsa_kv = proj[:, C_KVD:C_KVD + KV_W].reshape(bsz, t, 2, GROUPS, HD)
    idx_k = sm[:, 0:IDX_D].reshape(bsz, t, IDX_D)
    return nsa_kv, win, dsa_kv, idx_k


def kernel(x_prompt, x_sample, cache_nsa_kv, state_nsa_win, cache_dsa_kv, cache_dsa_idx, page_table,
           w_in, cmp_pe, cmp_w1, cmp_w2, w_proj_nsa, w_proj_dsa, w_out, ln_gain, ln_bias):
    bp, tp, _ = x_prompt.shape
    bs, ts, _ = x_sample.shape
    n_pool = cache_nsa_kv.shape[1]
    wpages = state_nsa_win.shape[2] // PAGE
    nsa_rows = cache_nsa_kv.reshape(-1, HD)
    dsa_rows = cache_dsa_kv.reshape(-1, HD)
    idx2d = cache_dsa_idx.reshape(-1, IDX_D)
    win_rows = state_nsa_win.reshape(-1, HD)
    win_table = jnp.arange(bs * wpages, dtype=jnp.int32).reshape(bs, wpages)
    yp, ys = x_prompt.reshape(bp * tp, D_MODEL), x_sample.reshape(bs * ts, D_MODEL)
    ypb, ysb = yp.astype(BF), ys.astype(BF)
    outs = [[] for _ in range(8)]
    for l in range(DEPTH):
        wts = _layer_weights(l, w_in, cmp_pe, cmp_w1, cmp_w2, w_proj_nsa, w_proj_dsa, w_out, ln_gain, ln_bias)
        yp, ypb, proj_p, sm_p = _layer_prompt(yp, ypb, bp, tp, wts)
        ys, ysb, proj_s, sm_s = _layer_sample(ys, ysb, bs, ts, wts, nsa_rows, win_rows, dsa_rows, idx2d,
                                        page_table + l * n_pool, win_table + l * bs * wpages)
        nkv_p, win_p, dkv_p, idx_p = _new_state(proj_p, sm_p, bp, tp)
        nkv_s, win_s, dkv_s, idx_s = _new_state(proj_s, sm_s, bs, ts)
        win_all = jnp.concatenate([state_nsa_win[l], win_s], axis=1)
        keep_p, keep_s = min(WINDOW, tp), min(WINDOW, win_all.shape[1])
        for lst, val in zip(outs, (nkv_p, nkv_s, win_p[:, tp - keep_p:], win_all[:, win_all.shape[1] - keep_s:],
                                   dkv_p, dkv_s, idx_p, idx_s)):
            lst.append(val)
    return (yp.reshape(bp, tp, D_MODEL), ys.reshape(bs, ts, D_MODEL)) + tuple(jnp.stack(o) for o in outs)
```

```python
import functools

import jax
import jax.numpy as jnp
from jax import lax
from jax.experimental import pallas as pl
from jax.experimental.pallas import tpu as pltpu

D_MODEL = 4096
DEPTH = 2
PAGE = 128
HD = 128
NSA_H = 16
DSA_H = 16
GROUPS = 2
RH = NSA_H // GROUPS
CMP = 64
CMP_SHIFT = 6
N_SEL = 16
WINDOW = 512
IDX_H = 32
IDX_D = 64
IDX_TOPK = 256
LN_EPS = 1e-5
ALPHA = (2 * DEPTH) ** 0.25
NEG = -1e30
SEL_FORCE = 1e4
SCALE = HD ** -0.5
LOG2E = 1.4426950408889634
INT_MIN = -(2 ** 31)

IN_SPLITS = (('nsa_q', 2048), ('nsa_kv', 1536), ('nsa_gate', 48), ('nsa_z', 2048), ('dsa_q', 2048),
             ('dsa_kv', 512), ('idx_q', 2048), ('idx_w', 32), ('idx_k', 64), ('dsa_z', 2048), ('merge', 8192))

C_QN, C_ZN, C_QD, C_ZD, C_QI, C_MG, C_KVN, C_KVD = 0, 2048, 4096, 6144, 8192, 10240, 18432, 19968
PROJ_W = 20480
PROJ_LAYOUT = (('nsa_q', C_QN), ('nsa_z', C_ZN), ('dsa_q', C_QD), ('dsa_z', C_ZD), ('idx_q', C_QI),
               ('merge', C_MG), ('nsa_kv', C_KVN), ('dsa_kv', C_KVD))
SM_W = 256
SM_IDXW = 64
LANE = 128
KV_W = 2 * GROUPS * HD

VMEM_LIMIT = 48 * 1024 * 1024
BF = jnp.bfloat16
F32 = jnp.float32


def _cp(sem):
    return pltpu.CompilerParams(dimension_semantics=sem, vmem_limit_bytes=VMEM_LIMIT)


def _dot_nt(a, b):
    return lax.dot_general(a, b, (((1,), (1,)), ((), ())), preferred_element_type=F32)


def _dot(a, b):
    return jnp.dot(a, b, preferred_element_type=F32)


def _mm_kernel(x_ref, w_ref, o_ref):
    o_ref[...] = _dot(x_ref[...], w_ref[...]).astype(o_ref.dtype)


def _matmul(x, w, tm, tn, out_dtype=F32):
    m, k = x.shape
    n = w.shape[1]
    return pl.pallas_call(
        _mm_kernel, grid=(m // tm, n // tn),
        in_specs=[pl.BlockSpec((tm, k), lambda i, j: (i, 0)), pl.BlockSpec((k, tn), lambda i, j: (0, j))],
        out_specs=pl.BlockSpec((tm, tn), lambda i, j: (i, j)),
        out_shape=jax.ShapeDtypeStruct((m, n), out_dtype),
        compiler_params=_cp(("parallel", "parallel")), name="matmul")(x, w)


def _gated_mm_kernel(a_ref, b_ref, wa_ref, wb_ref, m0_ref, m1_ref, o_ref):
    a = _dot(a_ref[...], wa_ref[...])
    b = _dot(b_ref[...], wb_ref[...])
    o_ref[...] = (jax.nn.sigmoid(m0_ref[...]) * a + jax.nn.sigmoid(m1_ref[...]) * b).astype(o_ref.dtype)


def _gated_matmul(a_in, b_in, wa, wb, proj, tm, tn):
    m, k = a_in.shape
    n = wa.shape[1]
    c0, c1 = C_MG // tn, (C_MG + D_MODEL) // tn
    return pl.pallas_call(
        _gated_mm_kernel, grid=(m // tm, n // tn),
        in_specs=[pl.BlockSpec((tm, k), lambda i, j: (i, 0)), pl.BlockSpec((tm, k), lambda i, j: (i, 0)),
                  pl.BlockSpec((k, tn), lambda i, j: (0, j)), pl.BlockSpec((k, tn), lambda i, j: (0, j)),
                  pl.BlockSpec((tm, tn), lambda i, j: (i, c0 + j)), pl.BlockSpec((tm, tn), lambda i, j: (i, c1 + j))],
        out_specs=pl.BlockSpec((tm, tn), lambda i, j: (i, j)),
        out_shape=jax.ShapeDtypeStruct((m, n), BF),
        compiler_params=_cp(("parallel", "parallel")), name="gated_matmul")(a_in, b_in, wa, wb, proj, proj)


def _ln_kernel(x_ref, h_ref, g_ref, b_ref, y_ref, yb_ref):
    v = ALPHA * x_ref[...] + h_ref[...]
    mu = jnp.mean(v, axis=-1, keepdims=True)
    c = v - mu
    var = jnp.mean(c * c, axis=-1, keepdims=True)
    y = c * lax.rsqrt(var + LN_EPS) * g_ref[...] + b_ref[...]
    y_ref[...] = y
    yb_ref[...] = y.astype(BF)


def _residual_ln(x, h, gain, bias, tm):
    m, n = x.shape
    row = pl.BlockSpec((tm, n), lambda i: (i, 0))
    vec = pl.BlockSpec((1, n), lambda i: (0, 0))
    return pl.pallas_call(
        _ln_kernel, grid=(m // tm,), in_specs=[row, row, vec, vec], out_specs=[row, row],
        out_shape=[jax.ShapeDtypeStruct((m, n), F32), jax.ShapeDtypeStruct((m, n), BF)],
        compiler_params=_cp(("parallel",)), name="residual_ln")(x, h, gain.reshape(1, n), bias.reshape(1, n))


def _gelu_tanh(x):
    return 0.5 * x * (1.0 + jnp.tanh(0.7978845608028654 * (x + 0.044715 * (x * x * x))))


CMP_TOKC = 8


def _compress_kernel(pt_ref, *refs, pg, pps, rpt):
    page_refs = refs[:pps]
    pe_ref, w1_ref, w2_ref, o_ref, slab_ref = refs[pps:]
    p = pl.program_id(2)
    for k, x_ref in enumerate(page_refs):
        for cg in range(2 * GROUPS):
            x = x_ref[pl.ds(cg, PAGE, stride=rpt), :] if rpt else x_ref[:, cg * HD:(cg + 1) * HD]
            slab_ref[cg, pl.ds(pl.multiple_of((p * pps + k) * PAGE, PAGE), PAGE), :] = (
                x + pe_ref[:, cg * HD:(cg + 1) * HD])

    @pl.when(p == pg // pps - 1)
    def _():
        nblk = pg * (PAGE // CMP)
        for c in range(2):
            acc = jnp.zeros((GROUPS * nblk, HD), F32)
            for tok0 in range(0, CMP, CMP_TOKC):
                lhs = jnp.concatenate(
                    [jnp.concatenate([slab_ref[c * GROUPS + g, pl.ds(tok, nblk, stride=CMP), :]
                                      for g in range(GROUPS)], axis=0).astype(BF)
                     for tok in range(tok0, tok0 + CMP_TOKC)], axis=1)
                acc = acc + _dot(lhs, w1_ref[c, tok0 * HD:(tok0 + CMP_TOKC) * HD, :])
            out = _dot(_gelu_tanh(acc).astype(BF), w2_ref[c])
            for g in range(GROUPS):
                o_ref[c, g] = out[g * nblk:(g + 1) * nblk]


def _compress(src, table, col_blk, rpt, pe_page, w1, w2, pg, pps):
    b, n_pages = table.shape
    ns = n_pages // pg
    nblk = pg * (PAGE // CMP)
    kern = functools.partial(_compress_kernel, pg=pg, pps=pps, rpt=rpt)
    if rpt:
        page_specs = [pl.BlockSpec((PAGE * rpt, HD), lambda bi, s, p, pt, k=k: (pt[bi, s * pg + p * pps + k], 0))
                      for k in range(pps)]
    else:
        page_specs = [pl.BlockSpec((PAGE, 4 * HD), lambda bi, s, p, pt, k=k: (pt[bi, s * pg + p * pps + k], col_blk))
                      for k in range(pps)]
    gs = pltpu.PrefetchScalarGridSpec(
        num_scalar_prefetch=1, grid=(b, ns, pg // pps),
        in_specs=page_specs + [pl.BlockSpec((PAGE, 4 * HD), lambda bi, s, p, pt: (0, 0)),
                               pl.BlockSpec((2, CMP * HD, HD), lambda bi, s, p, pt: (0, 0, 0)),
                               pl.BlockSpec((2, HD, HD), lambda bi, s, p, pt: (0, 0, 0))],
        out_specs=pl.BlockSpec((None, 2, GROUPS, nblk, HD), lambda bi, s, p, pt: (bi, 0, 0, s, 0)),
        scratch_shapes=[pltpu.VMEM((2 * GROUPS, pg * PAGE, HD), F32)])
    return pl.pallas_call(
        kern, grid_spec=gs, out_shape=jax.ShapeDtypeStruct((b, 2, GROUPS, ns * nblk, HD), F32),
        compiler_params=_cp(("parallel", "arbitrary", "arbitrary")), name="compress")(
            table, *([src] * pps), pe_page, w1, w2)


def _sortable(x):
    b = lax.bitcast_convert_type(x, jnp.int32)
    return jnp.where(b < 0, b ^ jnp.int32(0x7FFFFFFF), b)


def _count(pred):
    return jnp.sum(jnp.where(pred, 1.0, 0.0), axis=-1, keepdims=True)


def _topk_mask(key_ref, k, idx, idx_bits):
    rows = key_ref.shape[0]
    kf = float(k)
    zero = jnp.zeros((rows, 1), jnp.int32)
    t0 = jnp.where(_count(key_ref[...] >= zero) >= kf, zero, jnp.full((rows, 1), INT_MIN, jnp.int32))

    def value_bit(i, t):
        cand = t | jnp.left_shift(jnp.int32(1), jnp.int32(30) - i)
        return jnp.where(_count(key_ref[...] >= cand) >= kf, cand, t)

    t = lax.fori_loop(0, 31, value_bit, t0)
    keys = key_ref[...]
    need = kf - _count(keys > t)

    def index_bit(i, c):
        cand = c | jnp.left_shift(jnp.int32(1), jnp.int32(idx_bits - 1) - i)
        below = _count((key_ref[...] == t) & (idx < cand))
        return jnp.where(below < need, cand, c)

    c0 = lax.fori_loop(0, idx_bits, index_bit, zero)
    return (keys > t) | ((keys == t) & (idx <= c0))


def _masked_softmax(s, mask):
    s = jnp.where(mask, s, NEG)
    e = jnp.where(mask, jnp.exp(s - jnp.max(s, axis=-1, keepdims=True)), 0.0)
    return e / jnp.maximum(jnp.sum(e, axis=-1, keepdims=True), 1e-30)


def _nsa_select_kernel(slopes_ref, q_ref, kc_ref, vc_ref, ocmp_ref, sel_ref, key_scr,
                       *, qb, nb, nbs, nbs_pad, t0, bps):
    g = pl.program_id(1)
    i = pl.program_id(2)
    tq = t0 + i * qb + lax.broadcasted_iota(jnp.int32, (qb, 1), 0)
    blk_end = lax.broadcasted_iota(jnp.int32, (1, nb), 1) * CMP + (CMP - 1)
    valid = blk_end <= tq
    distf = (tq - blk_end).astype(F32)
    kc = kc_ref[...].astype(BF)
    vc = vc_ref[...].astype(BF)
    imp = jnp.zeros((qb, nb), F32)
    for r in range(RH):
        q = q_ref[:, r * HD:(r + 1) * HD].astype(BF)
        s = _dot_nt(q, kc) * SCALE - slopes_ref[g * RH + r] * distf
        p = _masked_softmax(s, valid)
        ocmp_ref[:, r * HD:(r + 1) * HD] = _dot(p.astype(BF), vc)
        imp = imp + p
    if nbs_pad > nb:
        imp = jnp.concatenate([imp, jnp.zeros((qb, nbs_pad - nb), F32)], axis=1)
    j = lax.broadcasted_iota(jnp.int32, (1, nbs_pad), 1)
    cur = jnp.right_shift(tq, CMP_SHIFT)
    forced = (j == 0) | (j == cur) | (j == cur - 1)
    score = jnp.where(j <= cur, jnp.where(forced, SEL_FORCE, imp), -SEL_FORCE)
    key_scr[...] = _sortable(score)
    sel = _topk_mask(key_scr, min(N_SEL, nbs), j, max(1, (nbs_pad - 1).bit_length()))
    sel_f = jnp.where(sel, 1.0, 0.0)
    for s_i in range(sel_ref.shape[0]):
        sel_ref[s_i] = jnp.concatenate(
            [sel_f[:, s_i * bps:(s_i + 1) * bps], jnp.zeros((qb, LANE - bps), F32)], axis=1)


def _nsa_select_t_kernel(slopes_ref, q_ref, kc_ref, vc_ref, ocmp_ref, sel_ref, *, qb, nb, n_sel, t0):
    g = pl.program_id(1)
    i = pl.program_id(2)
    n_col = lax.broadcasted_iota(jnp.int32, (nb, qb), 0)
    tq = t0 + i * qb + lax.broadcasted_iota(jnp.int32, (nb, qb), 1)
    blk_end = n_col * CMP + (CMP - 1)
    valid = blk_end <= tq
    distf = (tq - blk_end).astype(F32)
    kc = kc_ref[...].astype(BF)
    vct = vc_ref[...].T.astype(BF)
    imp = jnp.zeros((nb, qb), F32)
    for r in range(RH):
        q = q_ref[:, r * HD:(r + 1) * HD].astype(BF)
        s = jnp.where(valid, _dot_nt(kc, q) * SCALE - slopes_ref[g * RH + r] * distf, NEG)
        e = jnp.where(valid, jnp.exp(s - jnp.max(s, axis=0, keepdims=True)), 0.0)
        p = e / jnp.maximum(jnp.sum(e, axis=0, keepdims=True), 1e-30)
        ocmp_ref[:, r * HD:(r + 1) * HD] = _dot(vct, p.astype(BF)).T
        imp = imp + p
    cur = jnp.right_shift(tq, CMP_SHIFT)
    forced = (n_col == 0) | (n_col == cur) | (n_col == cur - 1)
    score = jnp.where(n_col <= cur, jnp.where(forced, SEL_FORCE, imp), -SEL_FORCE)
    rank = jnp.zeros((nb, qb), F32)
    for a in range(nb):
        row = score[a:a + 1, :]
        rank = rank + jnp.where((row > score) | ((row == score) & (n_col > a)), 1.0, 0.0)
    bias_t = jnp.where(rank < float(n_sel), 0.0, NEG)
    for n in range(nb):
        sel_ref[n] = jnp.broadcast_to(bias_t[n:n + 1, :], (8, qb))


def _nsa_select(proj, kcvc, slopes, bsz, t, nbs, t0, past_pages):
    qb = min(t, LANE)
    nq = t // qb
    nb = kcvc.shape[3]
    nbs_pad = nbs if nbs == nb else -(-nbs // LANE) * LANE
    if past_pages:
        bps = _fa_pps(past_pages) * (PAGE // CMP)
        n_grp = past_pages * (PAGE // CMP) // bps + 1
        assert n_grp * bps <= nbs_pad and nq == 1
        kern = functools.partial(_nsa_select_kernel, qb=qb, nb=nb, nbs=nbs, nbs_pad=nbs_pad, t0=t0, bps=bps)
        scratch = [pltpu.VMEM((qb, nbs_pad), jnp.int32)]
    else:
        assert nbs == nb
        kern = functools.partial(_nsa_select_t_kernel, qb=qb, nb=nb, n_sel=min(N_SEL, nbs), t0=t0)
        scratch = []
    wq = RH * HD
    if past_pages:
        sel_spec = pl.BlockSpec((None, None, n_grp, qb, LANE), lambda b, g, i: (b, g, 0, 0, 0))
        sel_shape = jax.ShapeDtypeStruct((bsz, GROUPS, n_grp, t, LANE), F32)
    else:
        sel_spec = pl.BlockSpec((None, None, nbs, 8, qb), lambda b, g, i: (b, g, 0, 0, i))
        sel_shape = jax.ShapeDtypeStruct((bsz, GROUPS, nbs, 8, t), F32)
    return pl.pallas_call(
        kern, grid=(bsz, GROUPS, nq),
        in_specs=[pl.BlockSpec(memory_space=pltpu.SMEM),
                  pl.BlockSpec((qb, wq), lambda b, g, i: (b * nq + i, C_QN // wq + g)),
                  pl.BlockSpec((None, None, None, nb, HD), lambda b, g, i: (b, 0, g, 0, 0)),
                  pl.BlockSpec((None, None, None, nb, HD), lambda b, g, i: (b, 1, g, 0, 0))],
        out_specs=[pl.BlockSpec((qb, wq), lambda b, g, i: (b * nq + i, g)), sel_spec],
        out_shape=[jax.ShapeDtypeStruct((bsz * t, NSA_H * HD), F32), sel_shape],
        scratch_shapes=scratch,
        compiler_params=_cp(("parallel", "parallel", "parallel")), name="nsa_select")(slopes, proj, kcvc, kcvc)


def _fa_prompt_kernel(*refs, mode, qb, kb):
    if mode == "win":
        slopes_ref, q_ref, k_ref, v_ref, o_ref = refs[:5]
        mask_ref = None
    else:
        slopes_ref, q_ref, k_ref, v_ref, mask_ref, o_ref = refs[:6]
    qs_scr, bias_scr, s_scr, p_scr, mb_scr, m_scr, a_scr, acc_scr = refs[-8:]
    g = pl.program_id(1)
    i = pl.program_id(2)
    k_local = lax.broadcasted_iota(jnp.int32, (kb, qb), 0)
    q_local = lax.broadcasted_iota(jnp.int32, (kb, qb), 1)
    for r in range(RH):
        qs_scr[r * qb:(r + 1) * qb, :] = (q_ref[:, r * HD:(r + 1) * HD] * (SCALE * LOG2E)).astype(BF)

    @pl.when(i == 0)
    def _():
        for r in range(RH):
            bias_scr[:, r * qb:(r + 1) * qb] = (slopes_ref[g * RH + r] * LOG2E) * k_local.astype(F32)

    m_scr[...] = jnp.full(m_scr.shape, NEG, F32)
    acc_scr[...] = jnp.zeros(acc_scr.shape, F32)
    j_hi = ((i + 1) * qb - 1) // kb
    j_lo = jnp.maximum(i * qb - (WINDOW - 1), 0) // kb if mode == "win" else 0

    def body(j, carry):
        off = pl.multiple_of(j * kb, kb)
        s_scr[...] = _dot_nt(k_ref[pl.ds(off, kb), :].astype(BF), qs_scr[...])
        dist = (i * qb + q_local) - (j * kb + k_local)
        ok = dist >= 0
        if mode == "win":
            ok = ok & (dist < WINDOW)
        mb = jnp.where(ok, 0.0, NEG)
        if mode == "sel":
            tiles = mask_ref[pl.ds(j * (kb // CMP), kb // CMP)]
            mb = mb + jnp.concatenate([jnp.tile(tiles[n], (CMP // 8, 1)) for n in range(kb // CMP)], axis=0)
        elif mode == "dsa":
            mb = mb + mask_ref[pl.ds(off, kb), :].astype(F32)
        mb_scr[...] = mb
        cbase = (j * kb - i * qb).astype(F32)
        for r in range(RH):
            sl = slice(r * qb, (r + 1) * qb)
            c = (slopes_ref[g * RH + r] * LOG2E) * cbase
            x = s_scr[:, sl] + bias_scr[:, sl] + mb_scr[...]
            m_prev = m_scr[:, sl]
            m_new = jnp.maximum(m_prev, jnp.max(x, axis=0, keepdims=True) + c)
            p_scr[:, sl] = jnp.exp2(x - (m_new - c)).astype(BF)
            a_scr[:, sl] = jnp.exp2(m_prev - m_new)
            m_scr[:, sl] = m_new
        vt = jnp.concatenate([v_ref[pl.ds(off, kb), :].T, jnp.ones((16, kb), F32)], axis=0).astype(BF)
        acc_scr[...] = acc_scr[...] * a_scr[...] + _dot(vt, p_scr[...])
        return carry

    lax.fori_loop(j_lo, j_hi + 1, body, 0)
    for r in range(RH):
        sl = slice(r * qb, (r + 1) * qb)
        o_t = jnp.where(m_scr[:, sl] > 0.5 * NEG,
                        acc_scr[0:HD, sl] / jnp.maximum(acc_scr[HD:HD + 1, sl], 1e-30), 0.0)
        o_ref[:, r * HD:(r + 1) * HD] = o_t.T


def _fa_prompt(mode, proj, slopes, bsz, t, q_col, k_col, v_col, mask=None):
    qb = LANE
    kb = min(2 * LANE, t)
    nq = t // qb
    wq = RH * HD
    in_specs = [pl.BlockSpec(memory_space=pltpu.SMEM),
                pl.BlockSpec((qb, wq), lambda b, g, i: (b * nq + i, q_col // wq + g)),
                pl.BlockSpec((t, HD), lambda b, g, i: (b, k_col // HD + g)),
                pl.BlockSpec((t, HD), lambda b, g, i: (b, v_col // HD + g))]
    args = [slopes, proj, proj, proj]
    if mode == "sel":
        in_specs.append(pl.BlockSpec((None, None, t // CMP, 8, qb), lambda b, g, i: (b, g, 0, 0, i)))
        args.append(mask)
    elif mode == "dsa":
        in_specs.append(pl.BlockSpec((None, t, qb), lambda b, g, i: (b, 0, i)))
        args.append(mask)
    kern = functools.partial(_fa_prompt_kernel, mode=mode, qb=qb, kb=kb)
    lanes = RH * qb
    return pl.pallas_call(
        kern, grid=(bsz, GROUPS, nq), in_specs=in_specs,
        out_specs=pl.BlockSpec((qb, wq), lambda b, g, i: (b * nq + i, g)),
        out_shape=jax.ShapeDtypeStruct((bsz * t, GROUPS * wq), F32),
        scratch_shapes=[pltpu.VMEM((lanes, HD), BF), pltpu.VMEM((kb, lanes), F32), pltpu.VMEM((kb, lanes), F32),
                        pltpu.VMEM((kb, lanes), BF), pltpu.VMEM((kb, qb), F32), pltpu.VMEM((1, lanes), F32),
                        pltpu.VMEM((1, lanes), F32), pltpu.VMEM((HD + 16, lanes), F32)],
        compiler_params=_cp(("parallel", "parallel", "arbitrary")), name="fa_prompt_" + mode)(*args)


PPS = 4
FA_PPS = 8


def _fa_sample_kernel(*refs, mode, t, pps, nsteps, t0, kpos0, rpt, kcomp):
    pt_ref, slope_ref, q_ref = refs[:3]
    page_refs = refs[3:3 + pps]
    new_ref = refs[3 + pps]
    rest = refs[4 + pps:]
    if mode == "win":
        mask_ref = maskn_ref = None
        o_ref, qs_scr, m_scr, l_scr, acc_scr = rest
    else:
        mask_ref, maskn_ref, o_ref, qs_scr, m_scr, l_scr, acc_scr = rest
    j = pl.program_id(1)
    bps = pps * (PAGE // CMP)

    @pl.when(j == 0)
    def _():
        for g in range(GROUPS):
            qs_scr[g] = jnp.concatenate(
                [q_ref[:, (g * RH + r) * HD:(g * RH + r + 1) * HD] for r in range(RH)], axis=0).astype(BF)
        m_scr[...] = jnp.full(m_scr.shape, NEG, F32)
        l_scr[...] = jnp.zeros(l_scr.shape, F32)
        acc_scr[...] = jnp.zeros(acc_scr.shape, F32)

    tq = t0 + jnp.concatenate([lax.broadcasted_iota(jnp.int32, (t, 1), 0)] * RH, axis=0)

    def key_bias(mref, g, nk):
        if mode == "sel":
            n = lax.broadcasted_iota(jnp.int32, (bps, nk), 0)
            kk = lax.broadcasted_iota(jnp.int32, (bps, nk), 1)
            expand = jnp.where(n == jnp.right_shift(kk, CMP_SHIFT), 1.0, 0.0).astype(BF)
            bias = (_dot(mref[g][:, 0:bps].astype(BF), expand) - 1.0) * (-NEG)
        else:
            bias = mref[...].astype(F32)
        return jnp.tile(bias, (RH, 1))

    def step(kparts, vparts, kpos, extra_ok, mref):
        nk = kpos.shape[1]
        dist = tq - kpos
        ok = dist >= 0
        if mode == "win":
            ok = ok & (dist < WINDOW)
        if extra_ok is not None:
            ok = ok & extra_ok
        distf = dist.astype(F32)
        raw = [_dot_nt(qs_scr[g], jnp.concatenate(kparts[g], axis=0).astype(BF)) for g in range(GROUPS)]
        probs = []
        for g in range(GROUPS):
            s = jnp.where(ok, raw[g] * SCALE - slope_ref[g] * distf, NEG)
            if mref is not None:
                s = s + key_bias(mref, g, nk)
            m_prev = m_scr[g]
            m_new = jnp.maximum(m_prev, jnp.max(s, axis=-1, keepdims=True))
            a = jnp.exp(m_prev - m_new)
            p = jnp.exp(s - m_new)
            l_scr[g] = a * l_scr[g] + jnp.sum(p, axis=-1, keepdims=True)
            m_scr[g] = m_new
            probs.append((a, p.astype(BF)))
        for g in range(GROUPS):
            a, p = probs[g]
            acc_scr[g] = a * acc_scr[g] + _dot(p, jnp.concatenate(vparts[g], axis=0).astype(BF))

    lane = lax.broadcasted_iota(jnp.int32, (1, pps * PAGE), 1)
    step([[r[pl.ds(kcomp * GROUPS + g, PAGE, stride=rpt), :] for r in page_refs] for g in range(GROUPS)],
         [[r[pl.ds((kcomp + 1) * GROUPS + g, PAGE, stride=rpt), :] for r in page_refs] for g in range(GROUPS)],
         kpos0 + j * (pps * PAGE) + lane, None, mask_ref)

    @pl.when(j == nsteps - 1)
    def _():
        lane1 = lax.broadcasted_iota(jnp.int32, (1, PAGE), 1)
        kv_new = jnp.concatenate([new_ref[...], jnp.zeros((PAGE - t, KV_W), F32)], axis=0)
        step([[kv_new[:, g * HD:(g + 1) * HD]] for g in range(GROUPS)],
             [[kv_new[:, (GROUPS + g) * HD:(GROUPS + g + 1) * HD]] for g in range(GROUPS)],
             t0 + lane1, lane1 < t, maskn_ref)
        for g in range(GROUPS):
            o = jnp.where(m_scr[g] > 0.5 * NEG, acc_scr[g] / jnp.maximum(l_scr[g], 1e-30), 0.0)
            for r in range(RH):
                o_ref[:, (g * RH + r) * HD:(g * RH + r + 1) * HD] = o[r * t:(r + 1) * t]


def _fa_pps(n_pages):
    return FA_PPS if n_pages % FA_PPS == 0 else PPS


def _fa_sample(mode, proj, past, table, slopes, t, t0, kpos0, q_col, rpt, kcomp, new_col, mask=None):
    bsz, nkb = table.shape
    pps = _fa_pps(nkb)
    nsteps = nkb // pps
    rows = RH * t
    wq = GROUPS * RH * HD
    slope_col = jnp.repeat(slopes.reshape(GROUPS, RH), t, axis=1).reshape(GROUPS, rows, 1)
    in_specs = [pl.BlockSpec((GROUPS, rows, 1), lambda b, j, pt: (0, 0, 0)),
                pl.BlockSpec((t, wq), lambda b, j, pt: (b, q_col // wq))]
    for p in range(pps):
        in_specs.append(pl.BlockSpec((PAGE * rpt, HD), lambda b, j, pt, p=p: (pt[b, j * pps + p], 0)))
    in_specs.append(pl.BlockSpec((t, KV_W), lambda b, j, pt: (b, new_col // KV_W)))
    args = [slope_col, proj] + [past] * pps + [proj]
    if mode == "sel":
        in_specs.append(pl.BlockSpec((None, GROUPS, None, t, LANE), lambda b, j, pt: (b, 0, j, 0, 0)))
        in_specs.append(pl.BlockSpec((None, GROUPS, None, t, LANE), lambda b, j, pt: (b, 0, nsteps, 0, 0)))
        args += [mask, mask]
    elif mode == "dsa":
        in_specs.append(pl.BlockSpec((None, t, pps * PAGE), lambda b, j, pt: (b, 0, j)))
        in_specs.append(pl.BlockSpec((None, t, PAGE), lambda b, j, pt: (b, 0, nkb)))
        args += [mask, mask]
    kern = functools.partial(_fa_sample_kernel, mode=mode, t=t, pps=pps, nsteps=nsteps, t0=t0, kpos0=kpos0,
                             rpt=rpt, kcomp=kcomp)
    gs = pltpu.PrefetchScalarGridSpec(
        num_scalar_prefetch=1, grid=(bsz, nsteps), in_specs=in_specs,
        out_specs=pl.BlockSpec((t, wq), lambda b, j, pt: (b, 0)),
        scratch_shapes=[pltpu.VMEM((GROUPS, rows, HD), BF), pltpu.VMEM((GROUPS, rows, 1), F32),
                        pltpu.VMEM((GROUPS, rows, 1), F32), pltpu.VMEM((GROUPS, rows, HD), F32)])
    return pl.pallas_call(
        kern, grid_spec=gs, out_shape=jax.ShapeDtypeStruct((bsz * t, wq), F32),
        compiler_params=_cp(("parallel", "arbitrary")), name="fa_sample_" + mode)(table, *args)


IDX_CHUNK = 256


def _dsa_index_prompt_kernel(q_ref, sq_ref, sk_ref, mask_ref, qb_scr, wb_scr, kpad_scr, score_scr, key_scr,
                             *, qb, t, top):
    i = pl.program_id(1)
    tq = i * qb + lax.broadcasted_iota(jnp.int32, (qb, 1), 0)

    @pl.when(i == 0)
    def _():
        kk = sk_ref[:, 0:IDX_D]
        zero = jnp.zeros((t, IDX_D), F32)
        kpad_scr[0] = jnp.concatenate([kk, zero], axis=1).astype(BF)
        kpad_scr[1] = jnp.concatenate([zero, kk], axis=1).astype(BF)

    qb_scr[...] = (q_ref[...] * (IDX_D ** -0.5)).astype(BF)
    w = sq_ref[:, SM_IDXW:SM_IDXW + IDX_H] * (IDX_H ** -0.5)
    for h in range(IDX_H):
        wb_scr[h] = jnp.broadcast_to(w[:, h:h + 1], (qb, LANE))
    score_scr[...] = jnp.zeros(score_scr.shape, F32)
    ch = min(IDX_CHUNK, t)

    def chunk(c, carry):
        off = pl.multiple_of(c * ch, ch)
        acc = jnp.zeros((qb, ch), F32)
        for pair in range(IDX_H // 2):
            qp = qb_scr[:, pair * LANE:(pair + 1) * LANE]
            for e in range(2):
                lg = _dot_nt(qp, kpad_scr[e, pl.ds(off, ch), :])
                acc = acc + jnp.tile(wb_scr[2 * pair + e], (1, ch // LANE)) * jnp.maximum(lg, 0.0)
        score_scr[:, pl.ds(off, ch)] = acc
        return carry

    lax.fori_loop(0, ((i + 1) * qb + ch - 1) // ch, chunk, 0)
    s_pos = lax.broadcasted_iota(jnp.int32, (1, t), 1)
    causal = s_pos <= tq
    key_scr[...] = _sortable(jnp.where(causal, score_scr[...], NEG))
    sel = _topk_mask(key_scr, top, s_pos, max(1, (t - 1).bit_length()))
    bias = jnp.where(sel & causal, 0.0, NEG)
    for c in range(t // LANE):
        mask_ref[c * LANE:(c + 1) * LANE, :] = bias[:, c * LANE:(c + 1) * LANE].T.astype(BF)


def _dsa_index_prompt(proj, sm, bsz, t):
    qb = LANE
    nq = t // qb
    top = min(IDX_TOPK, t // 4)
    kern = functools.partial(_dsa_index_prompt_kernel, qb=qb, t=t, top=top)
    wq = IDX_H * IDX_D
    return pl.pallas_call(
        kern, grid=(bsz, nq),
        in_specs=[pl.BlockSpec((qb, wq), lambda b, i: (b * nq + i, C_QI // wq)),
                  pl.BlockSpec((qb, LANE), lambda b, i: (b * nq + i, 0)),
                  pl.BlockSpec((t, LANE), lambda b, i: (b, 0))],
        out_specs=pl.BlockSpec((None, t, qb), lambda b, i: (b, 0, i)),
        out_shape=jax.ShapeDtypeStruct((bsz, t, t), BF),
        scratch_shapes=[pltpu.VMEM((qb, wq), BF), pltpu.VMEM((IDX_H, qb, LANE), F32),
                        pltpu.VMEM((2, t, 2 * IDX_D), BF), pltpu.VMEM((qb, t), F32), pltpu.VMEM((qb, t), jnp.int32)],
        compiler_params=_cp(("parallel", "arbitrary")), name="dsa_index_prompt")(proj, sm, sm)


def _dsa_index_sample_kernel(pt_ref, qf_ref, wcol_ref, *refs, t, nsteps, t0, lpad, top):
    page_refs = refs[:PPS]
    sn_ref, mask_ref, score_scr, key_scr = refs[PPS:]
    j = pl.program_id(1)

    @pl.when(j == 0)
    def _():
        score_scr[...] = jnp.full(score_scr.shape, NEG, F32)

    qf = qf_ref[...].astype(BF)
    wcol = wcol_ref[...] * (IDX_H ** -0.5)

    def scores(logits):
        r = jnp.maximum(logits * (IDX_D ** -0.5), 0.0) * wcol
        sc = r[0:t]
        for h in range(1, IDX_H):
            sc = sc + r[h * t:(h + 1) * t]
        return sc

    kpages_t = jnp.concatenate([r[...] for r in page_refs], axis=1).astype(BF)
    score_scr[:, pl.ds(pl.multiple_of(j * (PPS * PAGE), PPS * PAGE), PPS * PAGE)] = scores(_dot(qf, kpages_t))

    @pl.when(j == nsteps - 1)
    def _():
        tq = t0 + lax.broadcasted_iota(jnp.int32, (t, 1), 0)
        lane = lax.broadcasted_iota(jnp.int32, (1, PAGE), 1)
        kn = jnp.concatenate([sn_ref[:, 0:IDX_D], jnp.zeros((PAGE - t, IDX_D), F32)], axis=0).astype(BF)
        ok = (lane < t) & (t0 + lane <= tq)
        score_scr[:, lpad - PAGE:lpad] = jnp.where(ok, scores(_dot_nt(qf, kn)), NEG)
        s_pos = lax.broadcasted_iota(jnp.int32, (1, lpad), 1)
        key_scr[...] = _sortable(score_scr[...])
        sel = _topk_mask(key_scr, top, s_pos, max(1, (lpad - 1).bit_length()))
        mask_ref[...] = jnp.where(sel & (s_pos <= tq) & (s_pos < t0 + t), 0.0, NEG).astype(BF)


def _dsa_index_sample(qf, wcol, idx_t, table, sm, t, t0):
    bsz, nkb = table.shape
    nsteps = nkb // PPS
    lpad = (nkb + 1) * PAGE
    top = min(IDX_TOPK, (t0 + t) // 4)
    rows = IDX_H * t
    kern = functools.partial(_dsa_index_sample_kernel, t=t, nsteps=nsteps, t0=t0, lpad=lpad, top=top)
    gs = pltpu.PrefetchScalarGridSpec(
        num_scalar_prefetch=1, grid=(bsz, nsteps),
        in_specs=[pl.BlockSpec((rows, IDX_D), lambda b, j, pt: (b, 0)),
                  pl.BlockSpec((rows, 1), lambda b, j, pt: (b, 0))]
        + [pl.BlockSpec((IDX_D, PAGE), lambda b, j, pt, p=p: (pt[b, j * PPS + p], 0)) for p in range(PPS)]
        + [pl.BlockSpec((t, LANE), lambda b, j, pt: (b, 0))],
        out_specs=pl.BlockSpec((None, t, lpad), lambda b, j, pt: (b, 0, 0)),
        scratch_shapes=[pltpu.VMEM((t, lpad), F32), pltpu.VMEM((t, lpad), jnp.int32)])
    return pl.pallas_call(
        kern, grid_spec=gs, out_shape=jax.ShapeDtypeStruct((bsz, t, lpad), BF),
        compiler_params=_cp(("parallel", "arbitrary")), name="dsa_index_sample")(
            table, qf, wcol, *([idx_t] * PPS), sm)


def _silu(z):
    return z * jax.nn.sigmoid(z)


def _combine_nsa_kernel(oc_ref, os_ref, ow_ref, gate_ref, z_ref, o_ref):
    gate = jax.nn.sigmoid(gate_ref[:, 0:3 * NSA_H])
    for h in range(NSA_H):
        sl = slice(h * HD, (h + 1) * HD)
        o = (gate[:, 3 * h:3 * h + 1] * oc_ref[:, sl] + gate[:, 3 * h + 1:3 * h + 2] * os_ref[:, sl]
             + gate[:, 3 * h + 2:3 * h + 3] * ow_ref[:, sl])
        o_ref[:, sl] = (o * _silu(z_ref[:, sl])).astype(BF)


def _combine_nsa(o_cmp, o_sel, o_win, proj, sm, tm):
    m, n = o_cmp.shape
    row = pl.BlockSpec((tm, n), lambda i: (i, 0))
    return pl.pallas_call(
        _combine_nsa_kernel, grid=(m // tm,),
        in_specs=[row, row, row, pl.BlockSpec((tm, LANE), lambda i: (i, 1)),
                  pl.BlockSpec((tm, n), lambda i: (i, C_ZN // n))],
        out_specs=row, out_shape=jax.ShapeDtypeStruct((m, n), BF),
        compiler_params=_cp(("parallel",)), name="combine_nsa")(o_cmp, o_sel, o_win, sm, proj)


def _combine_dsa_kernel(o_ref_in, z_ref, o_ref):
    o_ref[...] = (o_ref_in[...] * _silu(z_ref[...])).astype(BF)


def _combine_dsa(o, proj, tm):
    m, n = o.shape
    row = pl.BlockSpec((tm, n), lambda i: (i, 0))
    return pl.pallas_call(
        _combine_dsa_kernel, grid=(m // tm,),
        in_specs=[row, pl.BlockSpec((tm, n), lambda i: (i, C_ZD // n))],
        out_specs=row, out_shape=jax.ShapeDtypeStruct((m, n), BF),
        compiler_params=_cp(("parallel",)), name="combine_dsa")(o, proj)


PREP_TN = 512
PREP_TK = 1024


def _prep_tables():
    src, off = {}, 0
    for name, width in IN_SPLITS:
        src[name] = off
        off += width
    width = dict(IN_SPLITS)
    shifts, base, cls = [], [], []
    for name, _ in PROJ_LAYOUT:
        for c in range(width[name] // PREP_TN):
            s = src[name] + c * PREP_TN
            if s % LANE not in shifts:
                shifts.append(s % LANE)
            base.append(s // LANE)
            cls.append(shifts.index(s % LANE))
    return tuple(shifts), base, cls


def _prep_kernel(base_ref, cls_ref, *refs, shifts):
    o_ref = refs[-1]
    j = pl.program_id(1)
    for k, s in enumerate(shifts):
        @pl.when(cls_ref[j] == k)
        def _():
            win = jnp.concatenate([r[...] for r in refs[:-1]], axis=0)
            o_ref[...] = win[s:s + PREP_TN, :].T.astype(BF)


def _prep_w_in(w_in_t, l):
    shifts, base, cls = _prep_tables()
    assert all(sh % 8 == 0 for sh in shifts)
    k = w_in_t.shape[2]
    nwin = PREP_TN // LANE + 1
    kern = functools.partial(_prep_kernel, shifts=shifts)
    gs = pltpu.PrefetchScalarGridSpec(
        num_scalar_prefetch=2, grid=(k // PREP_TK, len(base)),
        in_specs=[pl.BlockSpec((None, LANE, PREP_TK), lambda i, j, bs, cs, m=m: (l, bs[j] + m, i))
                  for m in range(nwin)],
        out_specs=pl.BlockSpec((PREP_TK, PREP_TN), lambda i, j, bs, cs: (i, j)))
    return pl.pallas_call(
        kern, grid_spec=gs, out_shape=jax.ShapeDtypeStruct((k, PROJ_W), BF),
        compiler_params=_cp(("parallel", "parallel")), name="prep_w_in")(
            jnp.asarray(base, jnp.int32), jnp.asarray(cls, jnp.int32), *([w_in_t] * nwin))


def _prep_small_kernel(g_ref, a_ref, b_ref, o_ref, *, gate_off, w_off, k_off):
    tk = o_ref.shape[0]
    ab = jnp.concatenate([a_ref[...], b_ref[...]], axis=0)
    rows = jnp.concatenate([ab[k_off:k_off + IDX_D], ab[w_off:w_off + IDX_H], jnp.zeros((32, tk), F32),
                            g_ref[gate_off:gate_off + 3 * NSA_H], jnp.zeros((SM_W - 176, tk), F32)], axis=0)
    o_ref[...] = rows.T.astype(BF)


def _prep_w_small(w_in_t, l):
    src, off = {}, 0
    for name, width in IN_SPLITS:
        src[name] = off
        off += width
    gate_blk, ab_blk = src['nsa_gate'] // LANE, src['idx_w'] // LANE
    offs = dict(gate_off=src['nsa_gate'] - gate_blk * LANE, w_off=src['idx_w'] - ab_blk * LANE,
                k_off=src['idx_k'] - ab_blk * LANE)
    assert all(v % 8 == 0 for v in offs.values()) and offs['k_off'] + IDX_D <= 2 * LANE
    k = w_in_t.shape[2]
    kern = functools.partial(_prep_small_kernel, **offs)
    return pl.pallas_call(
        kern, grid=(k // PREP_TK,),
        in_specs=[pl.BlockSpec((None, LANE, PREP_TK), lambda i, blk=blk: (l, blk, i))
                  for blk in (gate_blk, ab_blk, ab_blk + 1)],
        out_specs=pl.BlockSpec((PREP_TK, SM_W), lambda i: (i, 0)),
        out_shape=jax.ShapeDtypeStruct((k, SM_W), BF),
        compiler_params=_cp(("parallel",)), name="prep_w_small")(w_in_t, w_in_t, w_in_t)


def _slopes(n):
    return jnp.exp2(-8.0 * jnp.arange(1, n + 1, dtype=F32) / n)


def _tail(proj, xf, wts, a_in, b_in, tm_mm, tm_ln):
    hm = _gated_matmul(a_in, b_in, wts['pa'], wts['pd'], proj, tm_mm, 512)
    h = _matmul(hm, wts['out'], tm_mm, 512)
    return _residual_ln(xf, h, wts['gain'], wts['bias'], tm_ln)


def _layer_prompt(xf, xb, bsz, t, wts):
    tm = min(1024, bsz * t)
    proj = _matmul(xb, wts['in'], tm, 512)
    sm = _matmul(xb, wts['in_small'], tm, SM_W)
    n_pages = t // PAGE
    table = jnp.arange(bsz * n_pages, dtype=jnp.int32).reshape(bsz, n_pages)
    kcvc = _compress(proj, table, C_KVN // (4 * HD), 0, wts['pe_page'], wts['w1'], wts['w2'], n_pages, 1)
    sl_n, sl_d = _slopes(NSA_H), _slopes(DSA_H)
    o_cmp, selm = _nsa_select(proj, kcvc, sl_n, bsz, t, t // CMP, 0, 0)
    o_sel = _fa_prompt("sel", proj, sl_n, bsz, t, C_QN, C_KVN + KV_W, C_KVN + KV_W + GROUPS * HD, selm)
    o_win = _fa_prompt("win", proj, sl_n, bsz, t, C_QN, C_KVN + 2 * KV_W, C_KVN + 2 * KV_W + GROUPS * HD)
    dmask = _dsa_index_prompt(proj, sm, bsz, t)
    o_dsa = _fa_prompt("dsa", proj, sl_d, bsz, t, C_QD, C_KVD, C_KVD + GROUPS * HD, dmask)
    tme = min(256, bsz * t)
    a_in = _combine_nsa(o_cmp, o_sel, o_win, proj, sm, tme)
    b_in = _combine_dsa(o_dsa, proj, tme)
    y, yb = _tail(proj, xf, wts, a_in, b_in, tm, tme)
    return y, yb, proj, sm


def _layer_sample(xf, xb, bsz, t, wts, nsa_rows, win_rows, dsa_rows, idx_t, page_table, win_table):
    rows = bsz * t
    proj = _matmul(xb, wts['in'], rows, 512)
    sm = _matmul(xb, wts['in_small'], rows, SM_W)
    n_pages = page_table.shape[1]
    t0 = n_pages * PAGE
    wbuf = win_table.shape[1] * PAGE
    kcvc = _compress(nsa_rows, page_table, 0, 4 * GROUPS, wts['pe_page'], wts['w1'], wts['w2'], min(64, n_pages), PPS)
    sl_n, sl_d = _slopes(NSA_H), _slopes(DSA_H)
    o_cmp, selm = _nsa_select(proj, kcvc, sl_n, bsz, t, -(-(t0 + t) // CMP), t0, n_pages)
    o_sel = _fa_sample("sel", proj, nsa_rows, page_table, sl_n, t, t0, 0, C_QN, 4 * GROUPS, 2, C_KVN + KV_W, selm)
    o_win = _fa_sample("win", proj, win_rows, win_table, sl_n, t, t0, t0 - wbuf, C_QN, 2 * GROUPS, 0,
                       C_KVN + 2 * KV_W)
    qi = proj[:, C_QI:C_QI + IDX_H * IDX_D].reshape(bsz, t, IDX_H, IDX_D).transpose(0, 2, 1, 3)
    qf = qi.reshape(bsz * IDX_H * t, IDX_D)
    wi = sm[:, SM_IDXW:SM_IDXW + IDX_H].reshape(bsz, t, IDX_H).transpose(0, 2, 1)
    wcol = wi.reshape(bsz * IDX_H * t, 1)
    dmask = _dsa_index_sample(qf, wcol, idx_t, page_table, sm, t, t0)
    o_dsa = _fa_sample("dsa", proj, dsa_rows, page_table, sl_d, t, t0, 0, C_QD, 2 * GROUPS, 0, C_KVD, dmask)
    a_in = _combine_nsa(o_cmp, o_sel, o_win, proj, sm, rows)
    b_in = _combine_dsa(o_dsa, proj, rows)
    y, yb = _tail(proj, xf, wts, a_in, b_in, rows, rows)
    return y, yb, proj, sm


def _layer_weights(l, w_in, cmp_pe, cmp_w1, cmp_w2, w_proj_nsa, w_proj_dsa, w_out, ln_gain, ln_bias):
    cmp_pe, cmp_w1, cmp_w2, w_proj_nsa, w_proj_dsa, w_out, ln_gain, ln_bias = (
        a[l] for a in (cmp_pe, cmp_w1, cmp_w2, w_proj_nsa, w_proj_dsa, w_out, ln_gain, ln_bias))
    pe = jnp.concatenate([cmp_pe, cmp_pe], axis=1)
    pe_page = jnp.concatenate([pe[0], pe[0], pe[1], pe[1]], axis=1)
    w_in_t = jnp.swapaxes(w_in, 1, 2)
    return {'in': _prep_w_in(w_in_t, l), 'in_small': _prep_w_small(w_in_t, l), 'pe_page': pe_page,
            'w1': cmp_w1.astype(BF), 'w2': cmp_w2.astype(BF),
            'pa': w_proj_nsa.astype(BF), 'pd': w_proj_dsa.astype(BF), 'out': w_out.astype(BF),
            'gain': ln_gain, 'bias': ln_bias}


def _new_state(proj, sm, bsz, t):
    nsa_kv = proj[:, C_KVN:C_KVN + 2 * KV_W].reshape(bsz, t, 4, GROUPS, HD)
    win = proj[:, C_KVN + 2 * KV_W:C_KVN + 3 * KV_W].reshape(bsz, t, 2, GROUPS, HD)
    dsa_kv = proj[:, C_KVD:C_KVD + KV_W].reshape(bsz, t, 2, GROUPS, HD)
    idx_k = sm[:, 0:IDX_D].reshape(bsz, t, IDX_D)
    return nsa_kv, win, dsa_kv, idx_k


def kernel(x_prompt, x_sample, cache_nsa_kv, state_nsa_win, cache_dsa_kv, cache_dsa_idx, page_table,
           w_in, cmp_pe, cmp_w1, cmp_w2, w_proj_nsa, w_proj_dsa, w_out, ln_gain, ln_bias):
    bp, tp, _ = x_prompt.shape
    bs, ts, _ = x_sample.shape
    n_pool = cache_nsa_kv.shape[1]
    wpages = state_nsa_win.shape[2] // PAGE
    nsa_rows = cache_nsa_kv.reshape(-1, HD)
    dsa_rows = cache_dsa_kv.reshape(-1, HD)
    idx_t = jnp.swapaxes(cache_dsa_idx, 2, 3).reshape(-1, PAGE)
    win_rows = state_nsa_win.reshape(-1, HD)
    win_table = jnp.arange(bs * wpages, dtype=jnp.int32).reshape(bs, wpages)
    yp, ys = x_prompt.reshape(bp * tp, D_MODEL), x_sample.reshape(bs * ts, D_MODEL)
    ypb, ysb = yp.astype(BF), ys.astype(BF)
    outs = [[] for _ in range(8)]
    for l in range(DEPTH):
        wts = _layer_weights(l, w_in, cmp_pe, cmp_w1, cmp_w2, w_proj_nsa, w_proj_dsa, w_out, ln_gain, ln_bias)
        yp, ypb, proj_p, sm_p = _layer_prompt(yp, ypb, bp, tp, wts)
        ys, ysb, proj_s, sm_s = _layer_sample(ys, ysb, bs, ts, wts, nsa_rows, win_rows, dsa_rows, idx_t,
                                              page_table + l * n_pool, win_table + l * bs * wpages)
        nkv_p, win_p, dkv_p, idx_p = _new_state(proj_p, sm_p, bp, tp)
        nkv_s, win_s, dkv_s, idx_s = _new_state(proj_s, sm_s, bs, ts)
        win_all = jnp.concatenate([state_nsa_win[l], win_s], axis=1)
        keep_p, keep_s = min(WINDOW, tp), min(WINDOW, win_all.shape[1])
        for lst, val in zip(outs, (nkv_p, nkv_s, win_p[:, tp - keep_p:], win_all[:, win_all.shape[1] - keep_s:],
                                   dkv_p, dkv_s, idx_p, idx_s)):
            lst.append(val)
    return (yp.reshape(bp, tp, D_MODEL), ys.reshape(bs, ts, D_MODEL)) + tuple(jnp.stack(o) for o in outs)
```

```python
import functools

import jax
import jax.numpy as jnp
from jax import lax
from jax.experimental import pallas as pl
from jax.experimental.pallas import tpu as pltpu

D_MODEL = 4096
DEPTH = 2
PAGE = 128
HD = 128
NSA_H = 16
DSA_H = 16
GROUPS = 2
RH = NSA_H // GROUPS
CMP = 64
CMP_SHIFT = 6
N_SEL = 16
WINDOW = 512
IDX_H = 32
IDX_D = 64
IDX_TOPK = 256
LN_EPS = 1e-5
ALPHA = (2 * DEPTH) ** 0.25
NEG = -1e30
SEL_FORCE = 1e4
SCALE = HD ** -0.5
LOG2E = 1.4426950408889634
INT_MIN = -(2 ** 31)

IN_SPLITS = (('nsa_q', 2048), ('nsa_kv', 1536), ('nsa_gate', 48), ('nsa_z', 2048), ('dsa_q', 2048),
             ('dsa_kv', 512), ('idx_q', 2048), ('idx_w', 32), ('idx_k', 64), ('dsa_z', 2048), ('merge', 8192))

C_QN, C_ZN, C_QD, C_ZD, C_QI, C_MG, C_KVN, C_KVD = 0, 2048, 4096, 6144, 8192, 10240, 18432, 19968
PROJ_W = 20480
PROJ_LAYOUT = (('nsa_q', C_QN), ('nsa_z', C_ZN), ('dsa_q', C_QD), ('dsa_z', C_ZD), ('idx_q', C_QI),
               ('merge', C_MG), ('nsa_kv', C_KVN), ('dsa_kv', C_KVD))
SM_W = 256
SM_IDXW = 64
LANE = 128
KV_W = 2 * GROUPS * HD

VMEM_LIMIT = 48 * 1024 * 1024
BF = jnp.bfloat16
F32 = jnp.float32


def _cp(sem):
    return pltpu.CompilerParams(dimension_semantics=sem, vmem_limit_bytes=VMEM_LIMIT)


def _dot_nt(a, b):
    return lax.dot_general(a, b, (((1,), (1,)), ((), ())), preferred_element_type=F32)


def _dot(a, b):
    return jnp.dot(a, b, preferred_element_type=F32)


def _mm_kernel(x_ref, w_ref, o_ref):
    o_ref[...] = _dot(x_ref[...], w_ref[...]).astype(o_ref.dtype)


def _matmul(x, w, tm, tn, out_dtype=F32):
    m, k = x.shape
    n = w.shape[1]
    return pl.pallas_call(
        _mm_kernel, grid=(m // tm, n // tn),
        in_specs=[pl.BlockSpec((tm, k), lambda i, j: (i, 0)), pl.BlockSpec((k, tn), lambda i, j: (0, j))],
        out_specs=pl.BlockSpec((tm, tn), lambda i, j: (i, j)),
        out_shape=jax.ShapeDtypeStruct((m, n), out_dtype),
        compiler_params=_cp(("parallel", "parallel")), name="matmul")(x, w)


def _gated_mm_kernel(a_ref, b_ref, wa_ref, wb_ref, m0_ref, m1_ref, o_ref):
    a = _dot(a_ref[...], wa_ref[...])
    b = _dot(b_ref[...], wb_ref[...])
    o_ref[...] = (jax.nn.sigmoid(m0_ref[...]) * a + jax.nn.sigmoid(m1_ref[...]) * b).astype(o_ref.dtype)


def _gated_matmul(a_in, b_in, wa, wb, proj, tm, tn):
    m, k = a_in.shape
    n = wa.shape[1]
    c0, c1 = C_MG // tn, (C_MG + D_MODEL) // tn
    return pl.pallas_call(
        _gated_mm_kernel, grid=(m // tm, n // tn),
        in_specs=[pl.BlockSpec((tm, k), lambda i, j: (i, 0)), pl.BlockSpec((tm, k), lambda i, j: (i, 0)),
                  pl.BlockSpec((k, tn), lambda i, j: (0, j)), pl.BlockSpec((k, tn), lambda i, j: (0, j)),
                  pl.BlockSpec((tm, tn), lambda i, j: (i, c0 + j)), pl.BlockSpec((tm, tn), lambda i, j: (i, c1 + j))],
        out_specs=pl.BlockSpec((tm, tn), lambda i, j: (i, j)),
        out_shape=jax.ShapeDtypeStruct((m, n), BF),
        compiler_params=_cp(("parallel", "parallel")), name="gated_matmul")(a_in, b_in, wa, wb, proj, proj)


def _ln_kernel(x_ref, h_ref, g_ref, b_ref, y_ref, yb_ref):
    v = ALPHA * x_ref[...] + h_ref[...]
    mu = jnp.mean(v, axis=-1, keepdims=True)
    c = v - mu
    var = jnp.mean(c * c, axis=-1, keepdims=True)
    y = c * lax.rsqrt(var + LN_EPS) * g_ref[...] + b_ref[...]
    y_ref[...] = y
    yb_ref[...] = y.astype(BF)


def _residual_ln(x, h, gain, bias, tm):
    m, n = x.shape
    row = pl.BlockSpec((tm, n), lambda i: (i, 0))
    vec = pl.BlockSpec((1, n), lambda i: (0, 0))
    return pl.pallas_call(
        _ln_kernel, grid=(m // tm,), in_specs=[row, row, vec, vec], out_specs=[row, row],
        out_shape=[jax.ShapeDtypeStruct((m, n), F32), jax.ShapeDtypeStruct((m, n), BF)],
        compiler_params=_cp(("parallel",)), name="residual_ln")(x, h, gain.reshape(1, n), bias.reshape(1, n))


def _gelu_tanh(x):
    return 0.5 * x * (1.0 + jnp.tanh(0.7978845608028654 * (x + 0.044715 * (x * x * x))))


CMP_TOKC = 8


def _compress_kernel(pt_ref, *refs, pg, pps, rpt):
    page_refs = refs[:pps]
    pe_ref, w1_ref, w2_ref, o_ref, slab_ref = refs[pps:]
    p = pl.program_id(2)
    for k, x_ref in enumerate(page_refs):
        for cg in range(2 * GROUPS):
            x = x_ref[pl.ds(cg, PAGE, stride=rpt), :] if rpt else x_ref[:, cg * HD:(cg + 1) * HD]
            slab_ref[cg, pl.ds(pl.multiple_of((p * pps + k) * PAGE, PAGE), PAGE), :] = (
                x + pe_ref[:, cg * HD:(cg + 1) * HD])

    @pl.when(p == pg // pps - 1)
    def _():
        nblk = pg * (PAGE // CMP)
        for c in range(2):
            acc = jnp.zeros((GROUPS * nblk, HD), F32)
            for tok0 in range(0, CMP, CMP_TOKC):
                lhs = jnp.concatenate(
                    [jnp.concatenate([slab_ref[c * GROUPS + g, pl.ds(tok, nblk, stride=CMP), :]
                                      for g in range(GROUPS)], axis=0).astype(BF)
                     for tok in range(tok0, tok0 + CMP_TOKC)], axis=1)
                acc = acc + _dot(lhs, w1_ref[c, tok0 * HD:(tok0 + CMP_TOKC) * HD, :])
            out = _dot(_gelu_tanh(acc).astype(BF), w2_ref[c])
            for g in range(GROUPS):
                o_ref[c, g] = out[g * nblk:(g + 1) * nblk]


def _compress(src, table, col_blk, rpt, pe_page, w1, w2, pg, pps):
    b, n_pages = table.shape
    ns = n_pages // pg
    nblk = pg * (PAGE // CMP)
    kern = functools.partial(_compress_kernel, pg=pg, pps=pps, rpt=rpt)
    if rpt:
        page_specs = [pl.BlockSpec((PAGE * rpt, HD), lambda bi, s, p, pt, k=k: (pt[bi, s * pg + p * pps + k], 0))
                      for k in range(pps)]
    else:
        page_specs = [pl.BlockSpec((PAGE, 4 * HD), lambda bi, s, p, pt, k=k: (pt[bi, s * pg + p * pps + k], col_blk))
                      for k in range(pps)]
    gs = pltpu.PrefetchScalarGridSpec(
        num_scalar_prefetch=1, grid=(b, ns, pg // pps),
        in_specs=page_specs + [pl.BlockSpec((PAGE, 4 * HD), lambda bi, s, p, pt: (0, 0)),
                               pl.BlockSpec((2, CMP * HD, HD), lambda bi, s, p, pt: (0, 0, 0)),
                               pl.BlockSpec((2, HD, HD), lambda bi, s, p, pt: (0, 0, 0))],
        out_specs=pl.BlockSpec((None, 2, GROUPS, nblk, HD), lambda bi, s, p, pt: (bi, 0, 0, s, 0)),
        scratch_shapes=[pltpu.VMEM((2 * GROUPS, pg * PAGE, HD), F32)])
    return pl.pallas_call(
        kern, grid_spec=gs, out_shape=jax.ShapeDtypeStruct((b, 2, GROUPS, ns * nblk, HD), F32),
        compiler_params=_cp(("parallel", "arbitrary", "arbitrary")), name="compress")(
            table, *([src] * pps), pe_page, w1, w2)


def _sortable(x):
    b = lax.bitcast_convert_type(x, jnp.int32)
    return jnp.where(b < 0, b ^ jnp.int32(0x7FFFFFFF), b)


def _count(pred):
    return jnp.sum(jnp.where(pred, 1.0, 0.0), axis=-1, keepdims=True)


def _topk_mask(key_ref, k, idx, idx_bits):
    rows = key_ref.shape[0]
    kf = float(k)
    zero = jnp.zeros((rows, 1), jnp.int32)
    t0 = jnp.where(_count(key_ref[...] >= zero) >= kf, zero, jnp.full((rows, 1), INT_MIN, jnp.int32))

    def value_bit(i, t):
        cand = t | jnp.left_shift(jnp.int32(1), jnp.int32(30) - i)
        return jnp.where(_count(key_ref[...] >= cand) >= kf, cand, t)

    t = lax.fori_loop(0, 31, value_bit, t0)
    keys = key_ref[...]
    surplus = jnp.max(_count(keys >= t)) > kf

    def tie_cut(_):
        need = kf - _count(key_ref[...] > t)

        def index_bit(i, c):
            cand = c | jnp.left_shift(jnp.int32(1), jnp.int32(idx_bits - 1) - i)
            below = _count((key_ref[...] == t) & (idx < cand))
            return jnp.where(below < need, cand, c)

        return lax.fori_loop(0, idx_bits, index_bit, zero)

    c0 = lax.cond(surplus, tie_cut, lambda _: jnp.full((rows, 1), 2 ** idx_bits - 1, jnp.int32), None)
    return (keys > t) | ((keys == t) & (idx <= c0))


def _masked_softmax(s, mask):
    s = jnp.where(mask, s, NEG)
    e = jnp.where(mask, jnp.exp(s - jnp.max(s, axis=-1, keepdims=True)), 0.0)
    return e / jnp.maximum(jnp.sum(e, axis=-1, keepdims=True), 1e-30)


def _nsa_select_kernel(slopes_ref, q_ref, kc_ref, vc_ref, ocmp_ref, sel_ref, key_scr,
                       *, qb, nb, nbs, nbs_pad, t0, bps):
    g = pl.program_id(1)
    i = pl.program_id(2)
    tq = t0 + i * qb + lax.broadcasted_iota(jnp.int32, (qb, 1), 0)
    blk_end = lax.broadcasted_iota(jnp.int32, (1, nb), 1) * CMP + (CMP - 1)
    valid = blk_end <= tq
    distf = (tq - blk_end).astype(F32)
    kc = kc_ref[...].astype(BF)
    vc = vc_ref[...].astype(BF)
    qs = jnp.concatenate([q_ref[:, r * HD:(r + 1) * HD] for r in range(RH)], axis=0).astype(BF)
    slope = jnp.concatenate([jnp.full((qb, 1), slopes_ref[g * RH + r], F32) for r in range(RH)], axis=0)
    s = _dot_nt(qs, kc) * SCALE - slope * jnp.tile(distf, (RH, 1))
    p = _masked_softmax(s, jnp.tile(valid, (RH, 1)))
    o = _dot(p.astype(BF), vc)
    imp = jnp.zeros((qb, nb), F32)
    for r in range(RH):
        ocmp_ref[:, r * HD:(r + 1) * HD] = o[r * qb:(r + 1) * qb]
        imp = imp + p[r * qb:(r + 1) * qb]
    if nbs_pad > nb:
        imp = jnp.concatenate([imp, jnp.zeros((qb, nbs_pad - nb), F32)], axis=1)
    j = lax.broadcasted_iota(jnp.int32, (1, nbs_pad), 1)
    cur = jnp.right_shift(tq, CMP_SHIFT)
    forced = (j == 0) | (j == cur) | (j == cur - 1)
    score = jnp.where(j <= cur, jnp.where(forced, SEL_FORCE, imp), -SEL_FORCE)
    key_scr[...] = _sortable(score)
    sel = _topk_mask(key_scr, min(N_SEL, nbs), j, max(1, (nbs_pad - 1).bit_length()))
    sel_f = jnp.where(sel, 1.0, 0.0)
    for s_i in range(sel_ref.shape[0]):
        sel_ref[s_i] = jnp.concatenate(
            [sel_f[:, s_i * bps:(s_i + 1) * bps], jnp.zeros((qb, LANE - bps), F32)], axis=1)


def _nsa_select_t_kernel(slopes_ref, q_ref, kc_ref, vc_ref, ocmp_ref, sel_ref, *, qb, nb, n_sel, t0):
    g = pl.program_id(1)
    i = pl.program_id(2)
    n_col = lax.broadcasted_iota(jnp.int32, (nb, qb), 0)
    tq = t0 + i * qb + lax.broadcasted_iota(jnp.int32, (nb, qb), 1)
    blk_end = n_col * CMP + (CMP - 1)
    valid = blk_end <= tq
    distf = (tq - blk_end).astype(F32)
    kc = kc_ref[...].astype(BF)
    vct = vc_ref[...].T.astype(BF)
    qs = jnp.concatenate([q_ref[:, r * HD:(r + 1) * HD] for r in range(RH)], axis=0).astype(BF)
    raw = _dot_nt(kc, qs)
    imp = jnp.zeros((nb, qb), F32)
    probs = []
    for r in range(RH):
        s = jnp.where(valid, raw[:, r * qb:(r + 1) * qb] * SCALE - slopes_ref[g * RH + r] * distf, NEG)
        e = jnp.where(valid, jnp.exp(s - jnp.max(s, axis=0, keepdims=True)), 0.0)
        p = e / jnp.maximum(jnp.sum(e, axis=0, keepdims=True), 1e-30)
        probs.append(p.astype(BF))
        imp = imp + p
    o_t = _dot(vct, jnp.concatenate(probs, axis=1))
    for r in range(RH):
        ocmp_ref[:, r * HD:(r + 1) * HD] = o_t[:, r * qb:(r + 1) * qb].T
    cur = jnp.right_shift(tq, CMP_SHIFT)
    forced = (n_col == 0) | (n_col == cur) | (n_col == cur - 1)
    score = jnp.where(n_col <= cur, jnp.where(forced, SEL_FORCE, imp), -SEL_FORCE)
    rank = jnp.zeros((nb, qb), F32)
    for a in range(nb):
        row = score[a:a + 1, :]
        rank = rank + jnp.where((row > score) | ((row == score) & (n_col > a)), 1.0, 0.0)
    bias_t = jnp.where(rank < float(n_sel), 0.0, NEG)
    for n in range(nb):
        sel_ref[n] = jnp.broadcast_to(bias_t[n:n + 1, :], (8, qb))


def _nsa_select(proj, kcvc, slopes, bsz, t, nbs, t0, past_pages):
    qb = min(t, LANE)
    nq = t // qb
    nb = kcvc.shape[3]
    nbs_pad = nbs if nbs == nb else -(-nbs // LANE) * LANE
    if past_pages:
        bps = _fa_pps(past_pages) * (PAGE // CMP)
        n_grp = past_pages * (PAGE // CMP) // bps + 1
        assert n_grp * bps <= nbs_pad and nq == 1
        kern = functools.partial(_nsa_select_kernel, qb=qb, nb=nb, nbs=nbs, nbs_pad=nbs_pad, t0=t0, bps=bps)
        scratch = [pltpu.VMEM((qb, nbs_pad), jnp.int32)]
    else:
        assert nbs == nb
        kern = functools.partial(_nsa_select_t_kernel, qb=qb, nb=nb, n_sel=min(N_SEL, nbs), t0=t0)
        scratch = []
    wq = RH * HD
    if past_pages:
        sel_spec = pl.BlockSpec((None, None, n_grp, qb, LANE), lambda b, g, i: (b, g, 0, 0, 0))
        sel_shape = jax.ShapeDtypeStruct((bsz, GROUPS, n_grp, t, LANE), F32)
    else:
        sel_spec = pl.BlockSpec((None, None, nbs, 8, qb), lambda b, g, i: (b, g, 0, 0, i))
        sel_shape = jax.ShapeDtypeStruct((bsz, GROUPS, nbs, 8, t), F32)
    return pl.pallas_call(
        kern, grid=(bsz, GROUPS, nq),
        in_specs=[pl.BlockSpec(memory_space=pltpu.SMEM),
                  pl.BlockSpec((qb, wq), lambda b, g, i: (b * nq + i, C_QN // wq + g)),
                  pl.BlockSpec((None, None, None, nb, HD), lambda b, g, i: (b, 0, g, 0, 0)),
                  pl.BlockSpec((None, None, None, nb, HD), lambda b, g, i: (b, 1, g, 0, 0))],
        out_specs=[pl.BlockSpec((qb, wq), lambda b, g, i: (b * nq + i, g)), sel_spec],
        out_shape=[jax.ShapeDtypeStruct((bsz * t, NSA_H * HD), F32), sel_shape],
        scratch_shapes=scratch,
        compiler_params=_cp(("parallel", "parallel", "parallel")), name="nsa_select")(slopes, proj, kcvc, kcvc)


def _fa_prompt_kernel(*refs, mode, qb, kb):
    if mode == "win":
        slopes_ref, q_ref, k_ref, v_ref, o_ref = refs[:5]
        mask_ref = None
    else:
        slopes_ref, q_ref, k_ref, v_ref, mask_ref, o_ref = refs[:6]
    qs_scr, bias_scr, s_scr, p_scr, mb_scr, m_scr, a_scr, acc_scr = refs[-8:]
    g = pl.program_id(1)
    i = pl.program_id(2)
    k_local = lax.broadcasted_iota(jnp.int32, (kb, qb), 0)
    q_local = lax.broadcasted_iota(jnp.int32, (kb, qb), 1)
    for r in range(RH):
        qs_scr[r * qb:(r + 1) * qb, :] = (q_ref[:, r * HD:(r + 1) * HD] * (SCALE * LOG2E)).astype(BF)

    @pl.when(i == 0)
    def _():
        for r in range(RH):
            bias_scr[:, r * qb:(r + 1) * qb] = (slopes_ref[g * RH + r] * LOG2E) * k_local.astype(F32)

    m_scr[...] = jnp.full(m_scr.shape, NEG, F32)
    acc_scr[...] = jnp.zeros(acc_scr.shape, F32)
    j_hi = ((i + 1) * qb - 1) // kb
    j_lo = jnp.maximum(i * qb - (WINDOW - 1), 0) // kb if mode == "win" else 0

    def body(j, carry):
        off = pl.multiple_of(j * kb, kb)
        s_scr[...] = _dot_nt(k_ref[pl.ds(off, kb), :].astype(BF), qs_scr[...])
        dist = (i * qb + q_local) - (j * kb + k_local)
        ok = dist >= 0
        if mode == "win":
            ok = ok & (dist < WINDOW)
        mb = jnp.where(ok, 0.0, NEG)
        if mode == "sel":
            tiles = mask_ref[pl.ds(j * (kb // CMP), kb // CMP)]
            mb = mb + jnp.concatenate([jnp.tile(tiles[n], (CMP // 8, 1)) for n in range(kb // CMP)], axis=0)
        elif mode == "dsa":
            mb = mb + mask_ref[pl.ds(off, kb), :].astype(F32)
        mb_scr[...] = mb
        cbase = (j * kb - i * qb).astype(F32)
        for r in range(RH):
            sl = slice(r * qb, (r + 1) * qb)
            c = (slopes_ref[g * RH + r] * LOG2E) * cbase
            x = s_scr[:, sl] + bias_scr[:, sl] + mb_scr[...]
            m_prev = m_scr[:, sl]
            m_new = jnp.maximum(m_prev, jnp.max(x, axis=0, keepdims=True) + c)
            p_scr[:, sl] = jnp.exp2(x - (m_new - c)).astype(BF)
            a_scr[:, sl] = jnp.exp2(m_prev - m_new)
            m_scr[:, sl] = m_new
        vt = jnp.concatenate([v_ref[pl.ds(off, kb), :].T, jnp.ones((16, kb), F32)], axis=0).astype(BF)
        acc_scr[...] = acc_scr[...] * a_scr[...] + _dot(vt, p_scr[...])
        return carry

    lax.fori_loop(j_lo, j_hi + 1, body, 0)
    for r in range(RH):
        sl = slice(r * qb, (r + 1) * qb)
        o_t = jnp.where(m_scr[:, sl] > 0.5 * NEG,
                        acc_scr[0:HD, sl] / jnp.maximum(acc_scr[HD:HD + 1, sl], 1e-30), 0.0)
        o_ref[:, r * HD:(r + 1) * HD] = o_t.T


def _fa_prompt(mode, proj, slopes, bsz, t, q_col, k_col, v_col, mask=None):
    qb = LANE
    kb = min(2 * LANE, t)
    nq = t // qb
    wq = RH * HD
    in_specs = [pl.BlockSpec(memory_space=pltpu.SMEM),
                pl.BlockSpec((qb, wq), lambda b, g, i: (b * nq + i, q_col // wq + g)),
                pl.BlockSpec((t, HD), lambda b, g, i: (b, k_col // HD + g)),
                pl.BlockSpec((t, HD), lambda b, g, i: (b, v_col // HD + g))]
    args = [slopes, proj, proj, proj]
    if mode == "sel":
        in_specs.append(pl.BlockSpec((None, None, t // CMP, 8, qb), lambda b, g, i: (b, g, 0, 0, i)))
        args.append(mask)
    elif mode == "dsa":
        in_specs.append(pl.BlockSpec((None, t, qb), lambda b, g, i: (b, 0, i)))
        args.append(mask)
    kern = functools.partial(_fa_prompt_kernel, mode=mode, qb=qb, kb=kb)
    lanes = RH * qb
    return pl.pallas_call(
        kern, grid=(bsz, GROUPS, nq), in_specs=in_specs,
        out_specs=pl.BlockSpec((qb, wq), lambda b, g, i: (b * nq + i, g)),
        out_shape=jax.ShapeDtypeStruct((bsz * t, GROUPS * wq), F32),
        scratch_shapes=[pltpu.VMEM((lanes, HD), BF), pltpu.VMEM((kb, lanes), F32), pltpu.VMEM((kb, lanes), F32),
                        pltpu.VMEM((kb, lanes), BF), pltpu.VMEM((kb, qb), F32), pltpu.VMEM((1, lanes), F32),
                        pltpu.VMEM((1, lanes), F32), pltpu.VMEM((HD + 16, lanes), F32)],
        compiler_params=_cp(("parallel", "parallel", "arbitrary")), name="fa_prompt_" + mode)(*args)


PPS = 4
FA_PPS = 8


def _fa_sample_kernel(*refs, mode, t, pps, nsteps, t0, kpos0, rpt, kcomp):
    pt_ref, slope_ref, q_ref = refs[:3]
    page_refs = refs[3:3 + pps]
    new_ref = refs[3 + pps]
    rest = refs[4 + pps:]
    if mode == "win":
        mask_ref = maskn_ref = None
        o_ref, qs_scr, m_scr, l_scr, acc_scr = rest
    else:
        mask_ref, maskn_ref, o_ref, qs_scr, m_scr, l_scr, acc_scr = rest
    j = pl.program_id(1)
    bps = pps * (PAGE // CMP)

    @pl.when(j == 0)
    def _():
        for g in range(GROUPS):
            qs_scr[g] = jnp.concatenate(
                [q_ref[:, (g * RH + r) * HD:(g * RH + r + 1) * HD] for r in range(RH)], axis=0).astype(BF)
        m_scr[...] = jnp.full(m_scr.shape, NEG, F32)
        l_scr[...] = jnp.zeros(l_scr.shape, F32)
        acc_scr[...] = jnp.zeros(acc_scr.shape, F32)

    tq = t0 + jnp.concatenate([lax.broadcasted_iota(jnp.int32, (t, 1), 0)] * RH, axis=0)

    def key_bias(mref, g, nk):
        if mode == "sel":
            n = lax.broadcasted_iota(jnp.int32, (bps, nk), 0)
            kk = lax.broadcasted_iota(jnp.int32, (bps, nk), 1)
            expand = jnp.where(n == jnp.right_shift(kk, CMP_SHIFT), 1.0, 0.0).astype(BF)
            bias = (_dot(mref[g][:, 0:bps].astype(BF), expand) - 1.0) * (-NEG)
        else:
            bias = mref[...].astype(F32)
        return jnp.tile(bias, (RH, 1))

    def step(kparts, vparts, kpos, extra_ok, mref):
        nk = kpos.shape[1]
        dist = tq - kpos
        ok = dist >= 0
        if mode == "win":
            ok = ok & (dist < WINDOW)
        if extra_ok is not None:
            ok = ok & extra_ok
        distf = dist.astype(F32)
        raw = [_dot_nt(qs_scr[g], jnp.concatenate(kparts[g], axis=0).astype(BF)) for g in range(GROUPS)]
        probs = []
        for g in range(GROUPS):
            s = jnp.where(ok, raw[g] * SCALE - slope_ref[g] * distf, NEG)
            if mref is not None:
                s = s + key_bias(mref, g, nk)
            m_prev = m_scr[g]
            m_new = jnp.maximum(m_prev, jnp.max(s, axis=-1, keepdims=True))
            a = jnp.exp(m_prev - m_new)
            p = jnp.exp(s - m_new)
            l_scr[g] = a * l_scr[g] + jnp.sum(p, axis=-1, keepdims=True)
            m_scr[g] = m_new
            probs.append((a, p.astype(BF)))
        for g in range(GROUPS):
            a, p = probs[g]
            acc_scr[g] = a * acc_scr[g] + _dot(p, jnp.concatenate(vparts[g], axis=0).astype(BF))

    lane = lax.broadcasted_iota(jnp.int32, (1, pps * PAGE), 1)
    step([[r[pl.ds(kcomp * GROUPS + g, PAGE, stride=rpt), :] for r in page_refs] for g in range(GROUPS)],
         [[r[pl.ds((kcomp + 1) * GROUPS + g, PAGE, stride=rpt), :] for r in page_refs] for g in range(GROUPS)],
         kpos0 + j * (pps * PAGE) + lane, None, mask_ref)

    @pl.when(j == nsteps - 1)
    def _():
        lane1 = lax.broadcasted_iota(jnp.int32, (1, PAGE), 1)
        kv_new = jnp.concatenate([new_ref[...], jnp.zeros((PAGE - t, KV_W), F32)], axis=0)
        step([[kv_new[:, g * HD:(g + 1) * HD]] for g in range(GROUPS)],
             [[kv_new[:, (GROUPS + g) * HD:(GROUPS + g + 1) * HD]] for g in range(GROUPS)],
             t0 + lane1, lane1 < t, maskn_ref)
        for g in range(GROUPS):
            o = jnp.where(m_scr[g] > 0.5 * NEG, acc_scr[g] / jnp.maximum(l_scr[g], 1e-30), 0.0)
            for r in range(RH):
                o_ref[:, (g * RH + r) * HD:(g * RH + r + 1) * HD] = o[r * t:(r + 1) * t]


def _fa_pps(n_pages):
    return FA_PPS if n_pages % FA_PPS == 0 else PPS


def _fa_sample(mode, proj, past, table, slopes, t, t0, kpos0, q_col, rpt, kcomp, new_col, mask=None):
    bsz, nkb = table.shape
    pps = _fa_pps(nkb)
    nsteps = nkb // pps
    rows = RH * t
    wq = GROUPS * RH * HD
    slope_col = jnp.repeat(slopes.reshape(GROUPS, RH), t, axis=1).reshape(GROUPS, rows, 1)
    in_specs = [pl.BlockSpec((GROUPS, rows, 1), lambda b, j, pt: (0, 0, 0)),
                pl.BlockSpec((t, wq), lambda b, j, pt: (b, q_col // wq))]
    for p in range(pps):
        in_specs.append(pl.BlockSpec((PAGE * rpt, HD), lambda b, j, pt, p=p: (pt[b, j * pps + p], 0)))
    in_specs.append(pl.BlockSpec((t, KV_W), lambda b, j, pt: (b, new_col // KV_W)))
    args = [slope_col, proj] + [past] * pps + [proj]
    if mode == "sel":
        in_specs.append(pl.BlockSpec((None, GROUPS, None, t, LANE), lambda b, j, pt: (b, 0, j, 0, 0)))
        in_specs.append(pl.BlockSpec((None, GROUPS, None, t, LANE), lambda b, j, pt: (b, 0, nsteps, 0, 0)))
        args += [mask, mask]
    elif mode == "dsa":
        in_specs.append(pl.BlockSpec((None, t, pps * PAGE), lambda b, j, pt: (b, 0, j)))
        in_specs.append(pl.BlockSpec((None, t, PAGE), lambda b, j, pt: (b, 0, nkb)))
        args += [mask, mask]
    kern = functools.partial(_fa_sample_kernel, mode=mode, t=t, pps=pps, nsteps=nsteps, t0=t0, kpos0=kpos0,
                             rpt=rpt, kcomp=kcomp)
    gs = pltpu.PrefetchScalarGridSpec(
        num_scalar_prefetch=1, grid=(bsz, nsteps), in_specs=in_specs,
        out_specs=pl.BlockSpec((t, wq), lambda b, j, pt: (b, 0)),
        scratch_shapes=[pltpu.VMEM((GROUPS, rows, HD), BF), pltpu.VMEM((GROUPS, rows, 1), F32),
                        pltpu.VMEM((GROUPS, rows, 1), F32), pltpu.VMEM((GROUPS, rows, HD), F32)])
    return pl.pallas_call(
        kern, grid_spec=gs, out_shape=jax.ShapeDtypeStruct((bsz * t, wq), F32),
        compiler_params=_cp(("parallel", "arbitrary")), name="fa_sample_" + mode)(table, *args)


IDX_CHUNK = 256


def _dsa_index_prompt_kernel(q_ref, sq_ref, sk_ref, mask_ref, qb_scr, wb_scr, kpad_scr, score_scr, key_scr,
                             *, qb, t, top):
    i = pl.program_id(1)
    tq = i * qb + lax.broadcasted_iota(jnp.int32, (qb, 1), 0)

    @pl.when(i == 0)
    def _():
        kk = sk_ref[:, 0:IDX_D]
        zero = jnp.zeros((t, IDX_D), F32)
        kpad_scr[0] = jnp.concatenate([kk, zero], axis=1).astype(BF)
        kpad_scr[1] = jnp.concatenate([zero, kk], axis=1).astype(BF)

    qb_scr[...] = (q_ref[...] * (IDX_D ** -0.5)).astype(BF)
    w = sq_ref[:, SM_IDXW:SM_IDXW + IDX_H] * (IDX_H ** -0.5)
    for h in range(IDX_H):
        wb_scr[h] = jnp.broadcast_to(w[:, h:h + 1], (qb, LANE))
    score_scr[...] = jnp.zeros(score_scr.shape, F32)
    ch = min(IDX_CHUNK, t)

    def chunk(c, carry):
        off = pl.multiple_of(c * ch, ch)
        acc = jnp.zeros((qb, ch), F32)
        for pair in range(IDX_H // 2):
            qp = qb_scr[:, pair * LANE:(pair + 1) * LANE]
            for e in range(2):
                lg = _dot_nt(qp, kpad_scr[e, pl.ds(off, ch), :])
                acc = acc + jnp.tile(wb_scr[2 * pair + e], (1, ch // LANE)) * jnp.maximum(lg, 0.0)
        score_scr[:, pl.ds(off, ch)] = acc
        return carry

    lax.fori_loop(0, ((i + 1) * qb + ch - 1) // ch, chunk, 0)
    s_pos = lax.broadcasted_iota(jnp.int32, (1, t), 1)
    causal = s_pos <= tq
    key_scr[...] = _sortable(jnp.where(causal, score_scr[...], NEG))
    sel = _topk_mask(key_scr, top, s_pos, max(1, (t - 1).bit_length()))
    bias = jnp.where(sel & causal, 0.0, NEG)
    for c in range(t // LANE):
        mask_ref[c * LANE:(c + 1) * LANE, :] = bias[:, c * LANE:(c + 1) * LANE].T.astype(BF)


def _dsa_index_prompt(proj, sm, bsz, t):
    qb = LANE
    nq = t // qb
    top = min(IDX_TOPK, t // 4)
    kern = functools.partial(_dsa_index_prompt_kernel, qb=qb, t=t, top=top)
    wq = IDX_H * IDX_D
    return pl.pallas_call(
        kern, grid=(bsz, nq),
        in_specs=[pl.BlockSpec((qb, wq), lambda b, i: (b * nq + i, C_QI // wq)),
                  pl.BlockSpec((qb, LANE), lambda b, i: (b * nq + i, 0)),
                  pl.BlockSpec((t, LANE), lambda b, i: (b, 0))],
        out_specs=pl.BlockSpec((None, t, qb), lambda b, i: (b, 0, i)),
        out_shape=jax.ShapeDtypeStruct((bsz, t, t), BF),
        scratch_shapes=[pltpu.VMEM((qb, wq), BF), pltpu.VMEM((IDX_H, qb, LANE), F32),
                        pltpu.VMEM((2, t, 2 * IDX_D), BF), pltpu.VMEM((qb, t), F32), pltpu.VMEM((qb, t), jnp.int32)],
        compiler_params=_cp(("parallel", "arbitrary")), name="dsa_index_prompt")(proj, sm, sm)


def _dsa_index_sample_kernel(pt_ref, qf_ref, wcol_ref, *refs, t, nsteps, t0, lpad, top):
    page_refs = refs[:PPS]
    sn_ref, mask_ref, score_scr, key_scr = refs[PPS:]
    j = pl.program_id(1)

    @pl.when(j == 0)
    def _():
        score_scr[...] = jnp.full(score_scr.shape, NEG, F32)

    qf = qf_ref[...].astype(BF)
    wcol = wcol_ref[...] * (IDX_H ** -0.5)

    def scores(logits):
        r = jnp.maximum(logits * (IDX_D ** -0.5), 0.0) * wcol
        sc = r[0:t]
        for h in range(1, IDX_H):
            sc = sc + r[h * t:(h + 1) * t]
        return sc

    kpages_t = jnp.concatenate([r[...] for r in page_refs], axis=1).astype(BF)
    score_scr[:, pl.ds(pl.multiple_of(j * (PPS * PAGE), PPS * PAGE), PPS * PAGE)] = scores(_dot(qf, kpages_t))

    @pl.when(j == nsteps - 1)
    def _():
        tq = t0 + lax.broadcasted_iota(jnp.int32, (t, 1), 0)
        lane = lax.broadcasted_iota(jnp.int32, (1, PAGE), 1)
        kn = jnp.concatenate([sn_ref[:, 0:IDX_D], jnp.zeros((PAGE - t, IDX_D), F32)], axis=0).astype(BF)
        ok = (lane < t) & (t0 + lane <= tq)
        score_scr[:, lpad - PAGE:lpad] = jnp.where(ok, scores(_dot_nt(qf, kn)), NEG)
        s_pos = lax.broadcasted_iota(jnp.int32, (1, lpad), 1)
        key_scr[...] = _sortable(score_scr[...])
        sel = _topk_mask(key_scr, top, s_pos, max(1, (lpad - 1).bit_length()))
        mask_ref[...] = jnp.where(sel & (s_pos <= tq) & (s_pos < t0 + t), 0.0, NEG).astype(BF)


def _dsa_index_sample(qf, wcol, idx_t, table, sm, t, t0):
    bsz, nkb = table.shape
    nsteps = nkb // PPS
    lpad = (nkb + 1) * PAGE
    top = min(IDX_TOPK, (t0 + t) // 4)
    rows = IDX_H * t
    kern = functools.partial(_dsa_index_sample_kernel, t=t, nsteps=nsteps, t0=t0, lpad=lpad, top=top)
    gs = pltpu.PrefetchScalarGridSpec(
        num_scalar_prefetch=1, grid=(bsz, nsteps),
        in_specs=[pl.BlockSpec((rows, IDX_D), lambda b, j, pt: (b, 0)),
                  pl.BlockSpec((rows, 1), lambda b, j, pt: (b, 0))]
        + [pl.BlockSpec((IDX_D, PAGE), lambda b, j, pt, p=p: (pt[b, j * PPS + p], 0)) for p in range(PPS)]
        + [pl.BlockSpec((t, LANE), lambda b, j, pt: (b, 0))],
        out_specs=pl.BlockSpec((None, t, lpad), lambda b, j, pt: (b, 0, 0)),
        scratch_shapes=[pltpu.VMEM((t, lpad), F32), pltpu.VMEM((t, lpad), jnp.int32)])
    return pl.pallas_call(
        kern, grid_spec=gs, out_shape=jax.ShapeDtypeStruct((bsz, t, lpad), BF),
        compiler_params=_cp(("parallel", "arbitrary")), name="dsa_index_sample")(
            table, qf, wcol, *([idx_t] * PPS), sm)


def _silu(z):
    return z * jax.nn.sigmoid(z)


def _combine_nsa_kernel(oc_ref, os_ref, ow_ref, gate_ref, z_ref, o_ref):
    gate = jax.nn.sigmoid(gate_ref[:, 0:3 * NSA_H])
    for h in range(NSA_H):
        sl = slice(h * HD, (h + 1) * HD)
        o = (gate[:, 3 * h:3 * h + 1] * oc_ref[:, sl] + gate[:, 3 * h + 1:3 * h + 2] * os_ref[:, sl]
             + gate[:, 3 * h + 2:3 * h + 3] * ow_ref[:, sl])
        o_ref[:, sl] = (o * _silu(z_ref[:, sl])).astype(BF)


def _combine_nsa(o_cmp, o_sel, o_win, proj, sm, tm):
    m, n = o_cmp.shape
    row = pl.BlockSpec((tm, n), lambda i: (i, 0))
    return pl.pallas_call(
        _combine_nsa_kernel, grid=(m // tm,),
        in_specs=[row, row, row, pl.BlockSpec((tm, LANE), lambda i: (i, 1)),
                  pl.BlockSpec((tm, n), lambda i: (i, C_ZN // n))],
        out_specs=row, out_shape=jax.ShapeDtypeStruct((m, n), BF),
        compiler_params=_cp(("parallel",)), name="combine_nsa")(o_cmp, o_sel, o_win, sm, proj)


def _combine_dsa_kernel(o_ref_in, z_ref, o_ref):
    o_ref[...] = (o_ref_in[...] * _silu(z_ref[...])).astype(BF)


def _combine_dsa(o, proj, tm):
    m, n = o.shape
    row = pl.BlockSpec((tm, n), lambda i: (i, 0))
    return pl.pallas_call(
        _combine_dsa_kernel, grid=(m // tm,),
        in_specs=[row, pl.BlockSpec((tm, n), lambda i: (i, C_ZD // n))],
        out_specs=row, out_shape=jax.ShapeDtypeStruct((m, n), BF),
        compiler_params=_cp(("parallel",)), name="combine_dsa")(o, proj)


PREP_TN = 512
PREP_TK = 1024


def _prep_tables():
    src, off = {}, 0
    for name, width in IN_SPLITS:
        src[name] = off
        off += width
    width = dict(IN_SPLITS)
    shifts, base, cls = [], [], []
    for name, _ in PROJ_LAYOUT:
        for c in range(width[name] // PREP_TN):
            s = src[name] + c * PREP_TN
            if s % LANE not in shifts:
                shifts.append(s % LANE)
            base.append(s // LANE)
            cls.append(shifts.index(s % LANE))
    return tuple(shifts), base, cls


def _prep_kernel(base_ref, cls_ref, *refs, shifts):
    o_ref = refs[-1]
    j = pl.program_id(1)
    for k, s in enumerate(shifts):
        @pl.when(cls_ref[j] == k)
        def _():
            win = jnp.concatenate([r[...] for r in refs[:-1]], axis=0)
            o_ref[...] = win[s:s + PREP_TN, :].T.astype(BF)


def _prep_w_in(w_in_t, l):
    shifts, base, cls = _prep_tables()
    assert all(sh % 8 == 0 for sh in shifts)
    k = w_in_t.shape[2]
    nwin = PREP_TN // LANE + 1
    kern = functools.partial(_prep_kernel, shifts=shifts)
    gs = pltpu.PrefetchScalarGridSpec(
        num_scalar_prefetch=2, grid=(k // PREP_TK, len(base)),
        in_specs=[pl.BlockSpec((None, LANE, PREP_TK), lambda i, j, bs, cs, m=m: (l, bs[j] + m, i))
                  for m in range(nwin)],
        out_specs=pl.BlockSpec((PREP_TK, PREP_TN), lambda i, j, bs, cs: (i, j)))
    return pl.pallas_call(
        kern, grid_spec=gs, out_shape=jax.ShapeDtypeStruct((k, PROJ_W), BF),
        compiler_params=_cp(("parallel", "parallel")), name="prep_w_in")(
            jnp.asarray(base, jnp.int32), jnp.asarray(cls, jnp.int32), *([w_in_t] * nwin))


def _prep_small_kernel(g_ref, a_ref, b_ref, o_ref, *, gate_off, w_off, k_off):
    tk = o_ref.shape[0]
    ab = jnp.concatenate([a_ref[...], b_ref[...]], axis=0)
    rows = jnp.concatenate([ab[k_off:k_off + IDX_D], ab[w_off:w_off + IDX_H], jnp.zeros((32, tk), F32),
                            g_ref[gate_off:gate_off + 3 * NSA_H], jnp.zeros((SM_W - 176, tk), F32)], axis=0)
    o_ref[...] = rows.T.astype(BF)


def _prep_w_small(w_in_t, l):
    src, off = {}, 0
    for name, width in IN_SPLITS:
        src[name] = off
        off += width
    gate_blk, ab_blk = src['nsa_gate'] // LANE, src['idx_w'] // LANE
    offs = dict(gate_off=src['nsa_gate'] - gate_blk * LANE, w_off=src['idx_w'] - ab_blk * LANE,
                k_off=src['idx_k'] - ab_blk * LANE)
    assert all(v % 8 == 0 for v in offs.values()) and offs['k_off'] + IDX_D <= 2 * LANE
    k = w_in_t.shape[2]
    kern = functools.partial(_prep_small_kernel, **offs)
    return pl.pallas_call(
        kern, grid=(k // PREP_TK,),
        in_specs=[pl.BlockSpec((None, LANE, PREP_TK), lambda i, blk=blk: (l, blk, i))
                  for blk in (gate_blk, ab_blk, ab_blk + 1)],
        out_specs=pl.BlockSpec((PREP_TK, SM_W), lambda i: (i, 0)),
        out_shape=jax.ShapeDtypeStruct((k, SM_W), BF),
        compiler_params=_cp(("parallel",)), name="prep_w_small")(w_in_t, w_in_t, w_in_t)


def _slopes(n):
    return jnp.exp2(-8.0 * jnp.arange(1, n + 1, dtype=F32) / n)


def _tail(proj, xf, wts, a_in, b_in, tm_mm, tm_ln):
    hm = _gated_matmul(a_in, b_in, wts['pa'], wts['pd'], proj, tm_mm, 512)
    h = _matmul(hm, wts['out'], tm_mm, 512)
    return _residual_ln(xf, h, wts['gain'], wts['bias'], tm_ln)


def _layer_prompt(xf, xb, bsz, t, wts):
    tm = min(1024, bsz * t)
    proj = _matmul(xb, wts['in'], tm, 1024)
    sm = _matmul(xb, wts['in_small'], tm, SM_W)
    n_pages = t // PAGE
    table = jnp.arange(bsz * n_pages, dtype=jnp.int32).reshape(bsz, n_pages)
    kcvc = _compress(proj, table, C_KVN // (4 * HD), 0, wts['pe_page'], wts['w1'], wts['w2'], n_pages, 1)
    sl_n, sl_d = _slopes(NSA_H), _slopes(DSA_H)
    o_cmp, selm = _nsa_select(proj, kcvc, sl_n, bsz, t, t // CMP, 0, 0)
    o_sel = _fa_prompt("sel", proj, sl_n, bsz, t, C_QN, C_KVN + KV_W, C_KVN + KV_W + GROUPS * HD, selm)
    o_win = _fa_prompt("win", proj, sl_n, bsz, t, C_QN, C_KVN + 2 * KV_W, C_KVN + 2 * KV_W + GROUPS * HD)
    dmask = _dsa_index_prompt(proj, sm, bsz, t)
    o_dsa = _fa_prompt("dsa", proj, sl_d, bsz, t, C_QD, C_KVD, C_KVD + GROUPS * HD, dmask)
    tme = min(256, bsz * t)
    a_in = _combine_nsa(o_cmp, o_sel, o_win, proj, sm, tme)
    b_in = _combine_dsa(o_dsa, proj, tme)
    y, yb = _tail(proj, xf, wts, a_in, b_in, tm, tme)
    return y, yb, proj, sm


def _layer_sample(xf, xb, bsz, t, wts, nsa_rows, win_rows, dsa_rows, idx_t, page_table, win_table):
    rows = bsz * t
    proj = _matmul(xb, wts['in'], rows, 512)
    sm = _matmul(xb, wts['in_small'], rows, SM_W)
    n_pages = page_table.shape[1]
    t0 = n_pages * PAGE
    wbuf = win_table.shape[1] * PAGE
    kcvc = _compress(nsa_rows, page_table, 0, 4 * GROUPS, wts['pe_page'], wts['w1'], wts['w2'], min(64, n_pages), PPS)
    sl_n, sl_d = _slopes(NSA_H), _slopes(DSA_H)
    o_cmp, selm = _nsa_select(proj, kcvc, sl_n, bsz, t, -(-(t0 + t) // CMP), t0, n_pages)
    o_sel = _fa_sample("sel", proj, nsa_rows, page_table, sl_n, t, t0, 0, C_QN, 4 * GROUPS, 2, C_KVN + KV_W, selm)
    o_win = _fa_sample("win", proj, win_rows, win_table, sl_n, t, t0, t0 - wbuf, C_QN, 2 * GROUPS, 0,
                       C_KVN + 2 * KV_W)
    qi = proj[:, C_QI:C_QI + IDX_H * IDX_D].reshape(bsz, t, IDX_H, IDX_D).transpose(0, 2, 1, 3)
    qf = qi.reshape(bsz * IDX_H * t, IDX_D)
    wi = sm[:, SM_IDXW:SM_IDXW + IDX_H].reshape(bsz, t, IDX_H).transpose(0, 2, 1)
    wcol = wi.reshape(bsz * IDX_H * t, 1)
    dmask = _dsa_index_sample(qf, wcol, idx_t, page_table, sm, t, t0)
    o_dsa = _fa_sample("dsa", proj, dsa_rows, page_table, sl_d, t, t0, 0, C_QD, 2 * GROUPS, 0, C_KVD, dmask)
    a_in = _combine_nsa(o_cmp, o_sel, o_win, proj, sm, rows)
    b_in = _combine_dsa(o_dsa, proj, rows)
    y, yb = _tail(proj, xf, wts, a_in, b_in, rows, rows)
    return y, yb, proj, sm


def _layer_weights(l, w_in, cmp_pe, cmp_w1, cmp_w2, w_proj_nsa, w_proj_dsa, w_out, ln_gain, ln_bias):
    cmp_pe, cmp_w1, cmp_w2, w_proj_nsa, w_proj_dsa, w_out, ln_gain, ln_bias = (
        a[l] for a in (cmp_pe, cmp_w1, cmp_w2, w_proj_nsa, w_proj_dsa, w_out, ln_gain, ln_bias))
    pe = jnp.concatenate([cmp_pe, cmp_pe], axis=1)
    pe_page = jnp.concatenate([pe[0], pe[0], pe[1], pe[1]], axis=1)
    w_in_t = jnp.swapaxes(w_in, 1, 2)
    return {'in': _prep_w_in(w_in_t, l), 'in_small': _prep_w_small(w_in_t, l), 'pe_page': pe_page,
            'w1': cmp_w1.astype(BF), 'w2': cmp_w2.astype(BF),
            'pa': w_proj_nsa.astype(BF), 'pd': w_proj_dsa.astype(BF), 'out': w_out.astype(BF),
            'gain': ln_gain, 'bias': ln_bias}


def _new_state(proj, sm, bsz, t):
    nsa_kv = proj[:, C_KVN:C_KVN + 2 * KV_W].reshape(bsz, t, 4, GROUPS, HD)
    win = proj[:, C_KVN + 2 * KV_W:C_KVN + 3 * KV_W].reshape(bsz, t, 2, GROUPS, HD)
    dsa_kv = proj[:, C_KVD:C_KVD + KV_W].reshape(bsz, t, 2, GROUPS, HD)
    idx_k = sm[:, 0:IDX_D].reshape(bsz, t, IDX_D)
    return nsa_kv, win, dsa_kv, idx_k


def kernel(x_prompt, x_sample, cache_nsa_kv, state_nsa_win, cache_dsa_kv, cache_dsa_idx, page_table,
           w_in, cmp_pe, cmp_w1, cmp_w2, w_proj_nsa, w_proj_dsa, w_out, ln_gain, ln_bias):
    bp, tp, _ = x_prompt.shape
    bs, ts, _ = x_sample.shape
    n_pool = cache_nsa_kv.shape[1]
    wpages = state_nsa_win.shape[2] // PAGE
    nsa_rows = cache_nsa_kv.reshape(-1, HD)
    dsa_rows = cache_dsa_kv.reshape(-1, HD)
    idx_t = jnp.swapaxes(cache_dsa_idx, 2, 3).reshape(-1, PAGE)
    win_rows = state_nsa_win.reshape(-1, HD)
    win_table = jnp.arange(bs * wpages, dtype=jnp.int32).reshape(bs, wpages)
    yp, ys = x_prompt.reshape(bp * tp, D_MODEL), x_sample.reshape(bs * ts, D_MODEL)
    ypb, ysb = yp.astype(BF), ys.astype(BF)
    outs = [[] for _ in range(8)]
    for l in range(DEPTH):
        wts = _layer_weights(l, w_in, cmp_pe, cmp_w1, cmp_w2, w_proj_nsa, w_proj_dsa, w_out, ln_gain, ln_bias)
        yp, ypb, proj_p, sm_p = _layer_prompt(yp, ypb, bp, tp, wts)
        ys, ysb, proj_s, sm_s = _layer_sample(ys, ysb, bs, ts, wts, nsa_rows, win_rows, dsa_rows, idx_t,
                                              page_table + l * n_pool, win_table + l * bs * wpages)
        nkv_p, win_p, dkv_p, idx_p = _new_state(proj_p, sm_p, bp, tp)
        nkv_s, win_s, dkv_s, idx_s = _new_state(proj_s, sm_s, bs, ts)
        win_all = jnp.concatenate([state_nsa_win[l], win_s], axis=1)
        keep_p, keep_s = min(WINDOW, tp), min(WINDOW, win_all.shape[1])
        for lst, val in zip(outs, (nkv_p, nkv_s, win_p[:, tp - keep_p:], win_all[:, win_all.shape[1] - keep_s:],
                                   dkv_p, dkv_s, idx_p, idx_s)):
            lst.append(val)
    return (yp.reshape(bp, tp, D_MODEL), ys.reshape(bs, ts, D_MODEL)) + tuple(jnp.stack(o) for o in outs)
```

```python
import functools

import jax
import jax.numpy as jnp
from jax import lax
from jax.experimental import pallas as pl
from jax.experimental.pallas import tpu as pltpu

D_MODEL = 4096
DEPTH = 2
PAGE = 128
HD = 128
NSA_H = 16
DSA_H = 16
GROUPS = 2
RH = NSA_H // GROUPS
CMP = 64
CMP_SHIFT = 6
N_SEL = 16
WINDOW = 512
IDX_H = 32
IDX_D = 64
IDX_TOPK = 256
LN_EPS = 1e-5
ALPHA = (2 * DEPTH) ** 0.25
NEG = -1e30
SEL_FORCE = 1e4
SCALE = HD ** -0.5
LOG2E = 1.4426950408889634
INT_MIN = -(2 ** 31)

IN_SPLITS = (('nsa_q', 2048), ('nsa_kv', 1536), ('nsa_gate', 48), ('nsa_z', 2048), ('dsa_q', 2048),
             ('dsa_kv', 512), ('idx_q', 2048), ('idx_w', 32), ('idx_k', 64), ('dsa_z', 2048), ('merge', 8192))

C_QN, C_ZN, C_QD, C_ZD, C_QI, C_MG, C_KVN, C_KVD = 0, 2048, 4096, 6144, 8192, 10240, 18432, 19968
PROJ_W = 20480
PROJ_LAYOUT = (('nsa_q', C_QN), ('nsa_z', C_ZN), ('dsa_q', C_QD), ('dsa_z', C_ZD), ('idx_q', C_QI),
               ('merge', C_MG), ('nsa_kv', C_KVN), ('dsa_kv', C_KVD))
SM_W = 256
SM_IDXW = 64
LANE = 128
KV_W = 2 * GROUPS * HD

VMEM_LIMIT = 48 * 1024 * 1024
BF = jnp.bfloat16
F32 = jnp.float32


def _cp(sem):
    return pltpu.CompilerParams(dimension_semantics=sem, vmem_limit_bytes=VMEM_LIMIT)


def _dot_nt(a, b):
    return lax.dot_general(a, b, (((1,), (1,)), ((), ())), preferred_element_type=F32)


def _dot(a, b):
    return jnp.dot(a, b, preferred_element_type=F32)


def _mm_kernel(x_ref, w_ref, o_ref):
    o_ref[...] = _dot(x_ref[...], w_ref[...]).astype(o_ref.dtype)


def _matmul(x, w, tm, tn, out_dtype=F32):
    m, k = x.shape
    n = w.shape[1]
    return pl.pallas_call(
        _mm_kernel, grid=(m // tm, n // tn),
        in_specs=[pl.BlockSpec((tm, k), lambda i, j: (i, 0)), pl.BlockSpec((k, tn), lambda i, j: (0, j))],
        out_specs=pl.BlockSpec((tm, tn), lambda i, j: (i, j)),
        out_shape=jax.ShapeDtypeStruct((m, n), out_dtype),
        compiler_params=_cp(("parallel", "parallel")), name="matmul")(x, w)


def _gated_mm_kernel(a_ref, b_ref, wa_ref, wb_ref, m0_ref, m1_ref, o_ref):
    a = _dot(a_ref[...], wa_ref[...])
    b = _dot(b_ref[...], wb_ref[...])
    o_ref[...] = (jax.nn.sigmoid(m0_ref[...]) * a + jax.nn.sigmoid(m1_ref[...]) * b).astype(o_ref.dtype)


def _gated_matmul(a_in, b_in, wa, wb, proj, tm, tn):
    m, k = a_in.shape
    n = wa.shape[1]
    c0, c1 = C_MG // tn, (C_MG + D_MODEL) // tn
    return pl.pallas_call(
        _gated_mm_kernel, grid=(m // tm, n // tn),
        in_specs=[pl.BlockSpec((tm, k), lambda i, j: (i, 0)), pl.BlockSpec((tm, k), lambda i, j: (i, 0)),
                  pl.BlockSpec((k, tn), lambda i, j: (0, j)), pl.BlockSpec((k, tn), lambda i, j: (0, j)),
                  pl.BlockSpec((tm, tn), lambda i, j: (i, c0 + j)), pl.BlockSpec((tm, tn), lambda i, j: (i, c1 + j))],
        out_specs=pl.BlockSpec((tm, tn), lambda i, j: (i, j)),
        out_shape=jax.ShapeDtypeStruct((m, n), BF),
        compiler_params=_cp(("parallel", "parallel")), name="gated_matmul")(a_in, b_in, wa, wb, proj, proj)


def _ln_kernel(x_ref, h_ref, g_ref, b_ref, y_ref, yb_ref):
    v = ALPHA * x_ref[...] + h_ref[...]
    mu = jnp.mean(v, axis=-1, keepdims=True)
    c = v - mu
    var = jnp.mean(c * c, axis=-1, keepdims=True)
    y = c * lax.rsqrt(var + LN_EPS) * g_ref[...] + b_ref[...]
    y_ref[...] = y
    yb_ref[...] = y.astype(BF)


def _residual_ln(x, h, gain, bias, tm):
    m, n = x.shape
    row = pl.BlockSpec((tm, n), lambda i: (i, 0))
    vec = pl.BlockSpec((1, n), lambda i: (0, 0))
    return pl.pallas_call(
        _ln_kernel, grid=(m // tm,), in_specs=[row, row, vec, vec], out_specs=[row, row],
        out_shape=[jax.ShapeDtypeStruct((m, n), F32), jax.ShapeDtypeStruct((m, n), BF)],
        compiler_params=_cp(("parallel",)), name="residual_ln")(x, h, gain.reshape(1, n), bias.reshape(1, n))


def _gelu_tanh(x):
    return 0.5 * x * (1.0 + jnp.tanh(0.7978845608028654 * (x + 0.044715 * (x * x * x))))


CMP_TOKC = 8


def _compress_kernel(pt_ref, *refs, pg, pps, rpt):
    page_refs = refs[:pps]
    pe_ref, w1_ref, w2_ref, o_ref, slab_ref = refs[pps:]
    p = pl.program_id(2)
    for k, x_ref in enumerate(page_refs):
        for cg in range(2 * GROUPS):
            x = x_ref[pl.ds(cg, PAGE, stride=rpt), :] if rpt else x_ref[:, cg * HD:(cg + 1) * HD]
            slab_ref[cg, pl.ds(pl.multiple_of((p * pps + k) * PAGE, PAGE), PAGE), :] = (
                x + pe_ref[:, cg * HD:(cg + 1) * HD])

    @pl.when(p == pg // pps - 1)
    def _():
        nblk = pg * (PAGE // CMP)
        for c in range(2):
            acc = jnp.zeros((GROUPS * nblk, HD), F32)
            for tok0 in range(0, CMP, CMP_TOKC):
                lhs = jnp.concatenate(
                    [jnp.concatenate([slab_ref[c * GROUPS + g, pl.ds(tok, nblk, stride=CMP), :]
                                      for g in range(GROUPS)], axis=0).astype(BF)
                     for tok in range(tok0, tok0 + CMP_TOKC)], axis=1)
                acc = acc + _dot(lhs, w1_ref[c, tok0 * HD:(tok0 + CMP_TOKC) * HD, :])
            out = _dot(_gelu_tanh(acc).astype(BF), w2_ref[c])
            for g in range(GROUPS):
                o_ref[c, g] = out[g * nblk:(g + 1) * nblk]


def _compress(src, table, col_blk, rpt, pe_page, w1, w2, pg, pps):
    b, n_pages = table.shape
    ns = n_pages // pg
    nblk = pg * (PAGE // CMP)
    kern = functools.partial(_compress_kernel, pg=pg, pps=pps, rpt=rpt)
    if rpt:
        page_specs = [pl.BlockSpec((PAGE * rpt, HD), lambda bi, s, p, pt, k=k: (pt[bi, s * pg + p * pps + k], 0))
                      for k in range(pps)]
    else:
        page_specs = [pl.BlockSpec((PAGE, 4 * HD), lambda bi, s, p, pt, k=k: (pt[bi, s * pg + p * pps + k], col_blk))
                      for k in range(pps)]
    gs = pltpu.PrefetchScalarGridSpec(
        num_scalar_prefetch=1, grid=(b, ns, pg // pps),
        in_specs=page_specs + [pl.BlockSpec((PAGE, 4 * HD), lambda bi, s, p, pt: (0, 0)),
                               pl.BlockSpec((2, CMP * HD, HD), lambda bi, s, p, pt: (0, 0, 0)),
                               pl.BlockSpec((2, HD, HD), lambda bi, s, p, pt: (0, 0, 0))],
        out_specs=pl.BlockSpec((None, 2, GROUPS, nblk, HD), lambda bi, s, p, pt: (bi, 0, 0, s, 0)),
        scratch_shapes=[pltpu.VMEM((2 * GROUPS, pg * PAGE, HD), F32)])
    return pl.pallas_call(
        kern, grid_spec=gs, out_shape=jax.ShapeDtypeStruct((b, 2, GROUPS, ns * nblk, HD), F32),
        compiler_params=_cp(("parallel", "arbitrary", "arbitrary")), name="compress")(
            table, *([src] * pps), pe_page, w1, w2)


def _sortable(x):
    b = lax.bitcast_convert_type(x, jnp.int32)
    return jnp.where(b < 0, b ^ jnp.int32(0x7FFFFFFF), b)


def _count(pred):
    return jnp.sum(jnp.where(pred, 1.0, 0.0), axis=-1, keepdims=True)


def _topk_mask(key_ref, k, idx, idx_bits):
    rows = key_ref.shape[0]
    kf = float(k)
    zero = jnp.zeros((rows, 1), jnp.int32)
    t0 = jnp.where(_count(key_ref[...] >= zero) >= kf, zero, jnp.full((rows, 1), INT_MIN, jnp.int32))

    def value_bit(i, t):
        cand = t | jnp.left_shift(jnp.int32(1), jnp.int32(30) - i)
        return jnp.where(_count(key_ref[...] >= cand) >= kf, cand, t)

    t = lax.fori_loop(0, 31, value_bit, t0)
    keys = key_ref[...]
    surplus = jnp.max(_count(keys >= t)) > kf

    def tie_cut(_):
        need = kf - _count(key_ref[...] > t)

        def index_bit(i, c):
            cand = c | jnp.left_shift(jnp.int32(1), jnp.int32(idx_bits - 1) - i)
            below = _count((key_ref[...] == t) & (idx < cand))
            return jnp.where(below < need, cand, c)

        return lax.fori_loop(0, idx_bits, index_bit, zero)

    c0 = lax.cond(surplus, tie_cut, lambda _: jnp.full((rows, 1), 2 ** idx_bits - 1, jnp.int32), None)
    return (keys > t) | ((keys == t) & (idx <= c0))


def _masked_softmax(s, mask):
    s = jnp.where(mask, s, NEG)
    e = jnp.where(mask, jnp.exp(s - jnp.max(s, axis=-1, keepdims=True)), 0.0)
    return e / jnp.maximum(jnp.sum(e, axis=-1, keepdims=True), 1e-30)


def _nsa_select_kernel(slopes_ref, q_ref, kc_ref, vc_ref, ocmp_ref, sel_ref, key_scr,
                       *, qb, nb, nbs, nbs_pad, t0, bps):
    g = pl.program_id(1)
    i = pl.program_id(2)
    tq = t0 + i * qb + lax.broadcasted_iota(jnp.int32, (qb, 1), 0)
    blk_end = lax.broadcasted_iota(jnp.int32, (1, nb), 1) * CMP + (CMP - 1)
    valid = blk_end <= tq
    distf = (tq - blk_end).astype(F32)
    kc = kc_ref[...].astype(BF)
    vc = vc_ref[...].astype(BF)
    qs = jnp.concatenate([q_ref[:, r * HD:(r + 1) * HD] for r in range(RH)], axis=0).astype(BF)
    slope = jnp.concatenate([jnp.full((qb, 1), slopes_ref[g * RH + r], F32) for r in range(RH)], axis=0)
    s = _dot_nt(qs, kc) * SCALE - slope * jnp.tile(distf, (RH, 1))
    p = _masked_softmax(s, jnp.tile(valid, (RH, 1)))
    o = _dot(p.astype(BF), vc)
    imp = jnp.zeros((qb, nb), F32)
    for r in range(RH):
        ocmp_ref[:, r * HD:(r + 1) * HD] = o[r * qb:(r + 1) * qb]
        imp = imp + p[r * qb:(r + 1) * qb]
    if nbs_pad > nb:
        imp = jnp.concatenate([imp, jnp.zeros((qb, nbs_pad - nb), F32)], axis=1)
    j = lax.broadcasted_iota(jnp.int32, (1, nbs_pad), 1)
    cur = jnp.right_shift(tq, CMP_SHIFT)
    forced = (j == 0) | (j == cur) | (j == cur - 1)
    score = jnp.where(j <= cur, jnp.where(forced, SEL_FORCE, imp), -SEL_FORCE)
    key_scr[...] = _sortable(score)
    sel = _topk_mask(key_scr, min(N_SEL, nbs), j, max(1, (nbs_pad - 1).bit_length()))
    sel_f = jnp.where(sel, 1.0, 0.0)
    for s_i in range(sel_ref.shape[0]):
        sel_ref[s_i] = jnp.concatenate(
            [sel_f[:, s_i * bps:(s_i + 1) * bps], jnp.zeros((qb, LANE - bps), F32)], axis=1)


def _nsa_select_t_kernel(slopes_ref, q_ref, kc_ref, vc_ref, ocmp_ref, sel_ref, *, qb, nb, n_sel, t0):
    g = pl.program_id(1)
    i = pl.program_id(2)
    n_col = lax.broadcasted_iota(jnp.int32, (nb, qb), 0)
    tq = t0 + i * qb + lax.broadcasted_iota(jnp.int32, (nb, qb), 1)
    blk_end = n_col * CMP + (CMP - 1)
    valid = blk_end <= tq
    distf = (tq - blk_end).astype(F32)
    kc = kc_ref[...].astype(BF)
    vct = vc_ref[...].T.astype(BF)
    qs = jnp.concatenate([q_ref[:, r * HD:(r + 1) * HD] for r in range(RH)], axis=0).astype(BF)
    raw = _dot_nt(kc, qs)
    imp = jnp.zeros((nb, qb), F32)
    probs = []
    for r in range(RH):
        s = jnp.where(valid, raw[:, r * qb:(r + 1) * qb] * SCALE - slopes_ref[g * RH + r] * distf, NEG)
        e = jnp.where(valid, jnp.exp(s - jnp.max(s, axis=0, keepdims=True)), 0.0)
        p = e / jnp.maximum(jnp.sum(e, axis=0, keepdims=True), 1e-30)
        probs.append(p.astype(BF))
        imp = imp + p
    o_t = _dot(vct, jnp.concatenate(probs, axis=1))
    for r in range(RH):
        ocmp_ref[:, r * HD:(r + 1) * HD] = o_t[:, r * qb:(r + 1) * qb].T
    cur = jnp.right_shift(tq, CMP_SHIFT)
    forced = (n_col == 0) | (n_col == cur) | (n_col == cur - 1)
    score = jnp.where(n_col <= cur, jnp.where(forced, SEL_FORCE, imp), -SEL_FORCE)
    rank = jnp.zeros((nb, qb), F32)
    for a in range(nb):
        row = score[a:a + 1, :]
        rank = rank + jnp.where((row > score) | ((row == score) & (n_col > a)), 1.0, 0.0)
    bias_t = jnp.where(rank < float(n_sel), 0.0, NEG)
    for n in range(nb):
        sel_ref[n] = jnp.broadcast_to(bias_t[n:n + 1, :], (8, qb))


def _nsa_select(proj, kcvc, slopes, bsz, t, nbs, t0, past_pages):
    qb = min(t, LANE)
    nq = t // qb
    nb = kcvc.shape[3]
    nbs_pad = nbs if nbs == nb else -(-nbs // LANE) * LANE
    if past_pages:
        bps = _fa_pps(past_pages) * (PAGE // CMP)
        n_grp = past_pages * (PAGE // CMP) // bps + 1
        assert n_grp * bps <= nbs_pad and nq == 1
        kern = functools.partial(_nsa_select_kernel, qb=qb, nb=nb, nbs=nbs, nbs_pad=nbs_pad, t0=t0, bps=bps)
        scratch = [pltpu.VMEM((qb, nbs_pad), jnp.int32)]
    else:
        assert nbs == nb
        kern = functools.partial(_nsa_select_t_kernel, qb=qb, nb=nb, n_sel=min(N_SEL, nbs), t0=t0)
        scratch = []
    wq = RH * HD
    if past_pages:
        sel_spec = pl.BlockSpec((None, None, n_grp, qb, LANE), lambda b, g, i: (b, g, 0, 0, 0))
        sel_shape = jax.ShapeDtypeStruct((bsz, GROUPS, n_grp, t, LANE), F32)
    else:
        sel_spec = pl.BlockSpec((None, None, nbs, 8, qb), lambda b, g, i: (b, g, 0, 0, i))
        sel_shape = jax.ShapeDtypeStruct((bsz, GROUPS, nbs, 8, t), F32)
    return pl.pallas_call(
        kern, grid=(bsz, GROUPS, nq),
        in_specs=[pl.BlockSpec(memory_space=pltpu.SMEM),
                  pl.BlockSpec((qb, wq), lambda b, g, i: (b * nq + i, C_QN // wq + g)),
                  pl.BlockSpec((None, None, None, nb, HD), lambda b, g, i: (b, 0, g, 0, 0)),
                  pl.BlockSpec((None, None, None, nb, HD), lambda b, g, i: (b, 1, g, 0, 0))],
        out_specs=[pl.BlockSpec((qb, wq), lambda b, g, i: (b * nq + i, g)), sel_spec],
        out_shape=[jax.ShapeDtypeStruct((bsz * t, NSA_H * HD), F32), sel_shape],
        scratch_shapes=scratch,
        compiler_params=_cp(("parallel", "parallel", "parallel")), name="nsa_select")(slopes, proj, kcvc, kcvc)


def _fa_prompt_kernel(*refs, mode, qb, kb):
    if mode == "win":
        slopes_ref, q_ref, k_ref, v_ref, o_ref = refs[:5]
        mask_ref = None
    else:
        slopes_ref, q_ref, k_ref, v_ref, mask_ref, o_ref = refs[:6]
    qs_scr, bias_scr, s_scr, p_scr, mb_scr, m_scr, a_scr, acc_scr = refs[-8:]
    g = pl.program_id(1)
    i = pl.program_id(2)
    k_local = lax.broadcasted_iota(jnp.int32, (kb, qb), 0)
    q_local = lax.broadcasted_iota(jnp.int32, (kb, qb), 1)
    for r in range(RH):
        qs_scr[r * qb:(r + 1) * qb, :] = (q_ref[:, r * HD:(r + 1) * HD] * (SCALE * LOG2E)).astype(BF)

    @pl.when(i == 0)
    def _():
        for r in range(RH):
            bias_scr[:, r * qb:(r + 1) * qb] = (slopes_ref[g * RH + r] * LOG2E) * k_local.astype(F32)

    m_scr[...] = jnp.full(m_scr.shape, NEG, F32)
    acc_scr[...] = jnp.zeros(acc_scr.shape, F32)
    j_hi = ((i + 1) * qb - 1) // kb
    j_lo = jnp.maximum(i * qb - (WINDOW - 1), 0) // kb if mode == "win" else 0

    def body(j, carry):
        off = pl.multiple_of(j * kb, kb)
        s_scr[...] = _dot_nt(k_ref[pl.ds(off, kb), :].astype(BF), qs_scr[...])
        dist = (i * qb + q_local) - (j * kb + k_local)
        ok = dist >= 0
        if mode == "win":
            ok = ok & (dist < WINDOW)
        mb = jnp.where(ok, 0.0, NEG)
        if mode == "sel":
            tiles = mask_ref[pl.ds(j * (kb // CMP), kb // CMP)]
            mb = mb + jnp.concatenate([jnp.tile(tiles[n], (CMP // 8, 1)) for n in range(kb // CMP)], axis=0)
        elif mode == "dsa":
            mb = mb + mask_ref[pl.ds(off, kb), :].astype(F32)
        mb_scr[...] = mb
        cbase = (j * kb - i * qb).astype(F32)
        for r in range(RH):
            sl = slice(r * qb, (r + 1) * qb)
            c = (slopes_ref[g * RH + r] * LOG2E) * cbase
            x = s_scr[:, sl] + bias_scr[:, sl] + mb_scr[...]
            m_prev = m_scr[:, sl]
            m_new = jnp.maximum(m_prev, jnp.max(x, axis=0, keepdims=True) + c)
            p_scr[:, sl] = jnp.exp2(x - (m_new - c)).astype(BF)
            a_scr[:, sl] = jnp.exp2(m_prev - m_new)
            m_scr[:, sl] = m_new
        vt = jnp.concatenate([v_ref[pl.ds(off, kb), :].T, jnp.ones((16, kb), F32)], axis=0).astype(BF)
        acc_scr[...] = acc_scr[...] * a_scr[...] + _dot(vt, p_scr[...])
        return carry

    lax.fori_loop(j_lo, j_hi + 1, body, 0)
    for r in range(RH):
        sl = slice(r * qb, (r + 1) * qb)
        o_t = jnp.where(m_scr[:, sl] > 0.5 * NEG,
                        acc_scr[0:HD, sl] / jnp.maximum(acc_scr[HD:HD + 1, sl], 1e-30), 0.0)
        o_ref[:, r * HD:(r + 1) * HD] = o_t.T


def _fa_prompt(mode, proj, slopes, bsz, t, q_col, k_col, v_col, mask=None):
    qb = LANE
    kb = min(2 * LANE, t)
    nq = t // qb
    wq = RH * HD
    in_specs = [pl.BlockSpec(memory_space=pltpu.SMEM),
                pl.BlockSpec((qb, wq), lambda b, g, i: (b * nq + i, q_col // wq + g)),
                pl.BlockSpec((t, HD), lambda b, g, i: (b, k_col // HD + g)),
                pl.BlockSpec((t, HD), lambda b, g, i: (b, v_col // HD + g))]
    args = [slopes, proj, proj, proj]
    if mode == "sel":
        in_specs.append(pl.BlockSpec((None, None, t // CMP, 8, qb), lambda b, g, i: (b, g, 0, 0, i)))
        args.append(mask)
    elif mode == "dsa":
        in_specs.append(pl.BlockSpec((None, t, qb), lambda b, g, i: (b, 0, i)))
        args.append(mask)
    kern = functools.partial(_fa_prompt_kernel, mode=mode, qb=qb, kb=kb)
    lanes = RH * qb
    return pl.pallas_call(
        kern, grid=(bsz, GROUPS, nq), in_specs=in_specs,
        out_specs=pl.BlockSpec((qb, wq), lambda b, g, i: (b * nq + i, g)),
        out_shape=jax.ShapeDtypeStruct((bsz * t, GROUPS * wq), F32),
        scratch_shapes=[pltpu.VMEM((lanes, HD), BF), pltpu.VMEM((kb, lanes), F32), pltpu.VMEM((kb, lanes), F32),
                        pltpu.VMEM((kb, lanes), BF), pltpu.VMEM((kb, qb), F32), pltpu.VMEM((1, lanes), F32),
                        pltpu.VMEM((1, lanes), F32), pltpu.VMEM((HD + 16, lanes), F32)],
        compiler_params=_cp(("parallel", "parallel", "arbitrary")), name="fa_prompt_" + mode)(*args)


PPS = 4
FA_PPS = 16
IDX_PPS = 16


def _pages_per_step(n_pages, want):
    return want if n_pages % want == 0 else PPS


def _fa_sample_kernel(*refs, mode, t, pps, nsteps, t0, kpos0, rpt, kcomp):
    pt_ref, slope_ref, q_ref = refs[:3]
    page_refs = refs[3:3 + pps]
    new_ref = refs[3 + pps]
    rest = refs[4 + pps:]
    if mode == "win":
        mask_ref = maskn_ref = None
        o_ref, qs_scr, m_scr, l_scr, acc_scr = rest
    else:
        mask_ref, maskn_ref, o_ref, qs_scr, m_scr, l_scr, acc_scr = rest
    j = pl.program_id(1)
    bps = pps * (PAGE // CMP)

    @pl.when(j == 0)
    def _():
        for g in range(GROUPS):
            qs_scr[g] = jnp.concatenate(
                [q_ref[:, (g * RH + r) * HD:(g * RH + r + 1) * HD] for r in range(RH)], axis=0).astype(BF)
        m_scr[...] = jnp.full(m_scr.shape, NEG, F32)
        l_scr[...] = jnp.zeros(l_scr.shape, F32)
        acc_scr[...] = jnp.zeros(acc_scr.shape, F32)

    tq = t0 + jnp.concatenate([lax.broadcasted_iota(jnp.int32, (t, 1), 0)] * RH, axis=0)

    def key_bias(mref, g, nk):
        if mode == "sel":
            n = lax.broadcasted_iota(jnp.int32, (bps, nk), 0)
            kk = lax.broadcasted_iota(jnp.int32, (bps, nk), 1)
            expand = jnp.where(n == jnp.right_shift(kk, CMP_SHIFT), 1.0, 0.0).astype(BF)
            bias = (_dot(mref[g][:, 0:bps].astype(BF), expand) - 1.0) * (-NEG)
        else:
            bias = mref[...].astype(F32)
        return jnp.tile(bias, (RH, 1))

    def step(kparts, vparts, kpos, extra_ok, mref):
        nk = kpos.shape[1]
        dist = tq - kpos
        ok = dist >= 0
        if mode == "win":
            ok = ok & (dist < WINDOW)
        if extra_ok is not None:
            ok = ok & extra_ok
        distf = dist.astype(F32)
        raw = [_dot_nt(qs_scr[g], jnp.concatenate(kparts[g], axis=0).astype(BF)) for g in range(GROUPS)]
        probs = []
        for g in range(GROUPS):
            s = jnp.where(ok, raw[g] * SCALE - slope_ref[g] * distf, NEG)
            if mref is not None:
                s = s + key_bias(mref, g, nk)
            m_prev = m_scr[g]
            m_new = jnp.maximum(m_prev, jnp.max(s, axis=-1, keepdims=True))
            a = jnp.exp(m_prev - m_new)
            p = jnp.exp(s - m_new)
            l_scr[g] = a * l_scr[g] + jnp.sum(p, axis=-1, keepdims=True)
            m_scr[g] = m_new
            probs.append((a, p.astype(BF)))
        for g in range(GROUPS):
            a, p = probs[g]
            acc_scr[g] = a * acc_scr[g] + _dot(p, jnp.concatenate(vparts[g], axis=0).astype(BF))

    lane = lax.broadcasted_iota(jnp.int32, (1, pps * PAGE), 1)
    step([[r[pl.ds(kcomp * GROUPS + g, PAGE, stride=rpt), :] for r in page_refs] for g in range(GROUPS)],
         [[r[pl.ds((kcomp + 1) * GROUPS + g, PAGE, stride=rpt), :] for r in page_refs] for g in range(GROUPS)],
         kpos0 + j * (pps * PAGE) + lane, None, mask_ref)

    @pl.when(j == nsteps - 1)
    def _():
        lane1 = lax.broadcasted_iota(jnp.int32, (1, PAGE), 1)
        kv_new = jnp.concatenate([new_ref[...], jnp.zeros((PAGE - t, KV_W), F32)], axis=0)
        step([[kv_new[:, g * HD:(g + 1) * HD]] for g in range(GROUPS)],
             [[kv_new[:, (GROUPS + g) * HD:(GROUPS + g + 1) * HD]] for g in range(GROUPS)],
             t0 + lane1, lane1 < t, maskn_ref)
        for g in range(GROUPS):
            o = jnp.where(m_scr[g] > 0.5 * NEG, acc_scr[g] / jnp.maximum(l_scr[g], 1e-30), 0.0)
            for r in range(RH):
                o_ref[:, (g * RH + r) * HD:(g * RH + r + 1) * HD] = o[r * t:(r + 1) * t]


def _fa_pps(n_pages):
    return _pages_per_step(n_pages, FA_PPS)


def _fa_sample(mode, proj, past, table, slopes, t, t0, kpos0, q_col, rpt, kcomp, new_col, mask=None):
    bsz, nkb = table.shape
    pps = _fa_pps(nkb)
    nsteps = nkb // pps
    rows = RH * t
    wq = GROUPS * RH * HD
    slope_col = jnp.repeat(slopes.reshape(GROUPS, RH), t, axis=1).reshape(GROUPS, rows, 1)
    in_specs = [pl.BlockSpec((GROUPS, rows, 1), lambda b, j, pt: (0, 0, 0)),
                pl.BlockSpec((t, wq), lambda b, j, pt: (b, q_col // wq))]
    for p in range(pps):
        in_specs.append(pl.BlockSpec((PAGE * rpt, HD), lambda b, j, pt, p=p: (pt[b, j * pps + p], 0)))
    in_specs.append(pl.BlockSpec((t, KV_W), lambda b, j, pt: (b, new_col // KV_W)))
    args = [slope_col, proj] + [past] * pps + [proj]
    if mode == "sel":
        in_specs.append(pl.BlockSpec((None, GROUPS, None, t, LANE), lambda b, j, pt: (b, 0, j, 0, 0)))
        in_specs.append(pl.BlockSpec((None, GROUPS, None, t, LANE), lambda b, j, pt: (b, 0, nsteps, 0, 0)))
        args += [mask, mask]
    elif mode == "dsa":
        in_specs.append(pl.BlockSpec((None, t, pps * PAGE), lambda b, j, pt: (b, 0, j)))
        in_specs.append(pl.BlockSpec((None, t, PAGE), lambda b, j, pt: (b, 0, nkb)))
        args += [mask, mask]
    kern = functools.partial(_fa_sample_kernel, mode=mode, t=t, pps=pps, nsteps=nsteps, t0=t0, kpos0=kpos0,
                             rpt=rpt, kcomp=kcomp)
    gs = pltpu.PrefetchScalarGridSpec(
        num_scalar_prefetch=1, grid=(bsz, nsteps), in_specs=in_specs,
        out_specs=pl.BlockSpec((t, wq), lambda b, j, pt: (b, 0)),
        scratch_shapes=[pltpu.VMEM((GROUPS, rows, HD), BF), pltpu.VMEM((GROUPS, rows, 1), F32),
                        pltpu.VMEM((GROUPS, rows, 1), F32), pltpu.VMEM((GROUPS, rows, HD), F32)])
    return pl.pallas_call(
        kern, grid_spec=gs, out_shape=jax.ShapeDtypeStruct((bsz * t, wq), F32),
        compiler_params=_cp(("parallel", "arbitrary")), name="fa_sample_" + mode)(table, *args)


IDX_CHUNK = 256


def _dsa_index_prompt_kernel(q_ref, sq_ref, sk_ref, mask_ref, qb_scr, wb_scr, kpad_scr, score_scr, key_scr,
                             *, qb, t, top):
    i = pl.program_id(1)
    tq = i * qb + lax.broadcasted_iota(jnp.int32, (qb, 1), 0)

    @pl.when(i == 0)
    def _():
        kk = sk_ref[:, 0:IDX_D]
        zero = jnp.zeros((t, IDX_D), F32)
        kpad_scr[0] = jnp.concatenate([kk, zero], axis=1).astype(BF)
        kpad_scr[1] = jnp.concatenate([zero, kk], axis=1).astype(BF)

    qb_scr[...] = (q_ref[...] * (IDX_D ** -0.5)).astype(BF)
    w = sq_ref[:, SM_IDXW:SM_IDXW + IDX_H] * (IDX_H ** -0.5)
    for h in range(IDX_H):
        wb_scr[h] = jnp.broadcast_to(w[:, h:h + 1], (qb, LANE))
    score_scr[...] = jnp.zeros(score_scr.shape, F32)
    ch = min(IDX_CHUNK, t)

    def chunk(c, carry):
        off = pl.multiple_of(c * ch, ch)
        acc = jnp.zeros((qb, ch), F32)
        for pair in range(IDX_H // 2):
            qp = qb_scr[:, pair * LANE:(pair + 1) * LANE]
            for e in range(2):
                lg = _dot_nt(qp, kpad_scr[e, pl.ds(off, ch), :])
                acc = acc + jnp.tile(wb_scr[2 * pair + e], (1, ch // LANE)) * jnp.maximum(lg, 0.0)
        score_scr[:, pl.ds(off, ch)] = acc
        return carry

    lax.fori_loop(0, ((i + 1) * qb + ch - 1) // ch, chunk, 0)
    s_pos = lax.broadcasted_iota(jnp.int32, (1, t), 1)
    causal = s_pos <= tq
    key_scr[...] = _sortable(jnp.where(causal, score_scr[...], NEG))
    sel = _topk_mask(key_scr, top, s_pos, max(1, (t - 1).bit_length()))
    bias = jnp.where(sel & causal, 0.0, NEG)
    for c in range(t // LANE):
        mask_ref[c * LANE:(c + 1) * LANE, :] = bias[:, c * LANE:(c + 1) * LANE].T.astype(BF)


def _dsa_index_prompt(proj, sm, bsz, t):
    qb = LANE
    nq = t // qb
    top = min(IDX_TOPK, t // 4)
    kern = functools.partial(_dsa_index_prompt_kernel, qb=qb, t=t, top=top)
    wq = IDX_H * IDX_D
    return pl.pallas_call(
        kern, grid=(bsz, nq),
        in_specs=[pl.BlockSpec((qb, wq), lambda b, i: (b * nq + i, C_QI // wq)),
                  pl.BlockSpec((qb, LANE), lambda b, i: (b * nq + i, 0)),
                  pl.BlockSpec((t, LANE), lambda b, i: (b, 0))],
        out_specs=pl.BlockSpec((None, t, qb), lambda b, i: (b, 0, i)),
        out_shape=jax.ShapeDtypeStruct((bsz, t, t), BF),
        scratch_shapes=[pltpu.VMEM((qb, wq), BF), pltpu.VMEM((IDX_H, qb, LANE), F32),
                        pltpu.VMEM((2, t, 2 * IDX_D), BF), pltpu.VMEM((qb, t), F32), pltpu.VMEM((qb, t), jnp.int32)],
        compiler_params=_cp(("parallel", "arbitrary")), name="dsa_index_prompt")(proj, sm, sm)


def _dsa_index_sample_kernel(pt_ref, qf_ref, wcol_ref, *refs, t, pps, nsteps, t0, lpad, top):
    page_refs = refs[:pps]
    sn_ref, mask_ref, score_scr, key_scr = refs[pps:]
    j = pl.program_id(1)

    @pl.when(j == 0)
    def _():
        score_scr[...] = jnp.full(score_scr.shape, NEG, F32)

    qf = qf_ref[...].astype(BF)
    wcol = wcol_ref[...] * (IDX_H ** -0.5)

    def scores(logits):
        r = jnp.maximum(logits * (IDX_D ** -0.5), 0.0) * wcol
        sc = r[0:t]
        for h in range(1, IDX_H):
            sc = sc + r[h * t:(h + 1) * t]
        return sc

    kpages_t = jnp.concatenate([r[...] for r in page_refs], axis=1).astype(BF)
    score_scr[:, pl.ds(pl.multiple_of(j * (pps * PAGE), pps * PAGE), pps * PAGE)] = scores(_dot(qf, kpages_t))

    @pl.when(j == nsteps - 1)
    def _():
        tq = t0 + lax.broadcasted_iota(jnp.int32, (t, 1), 0)
        lane = lax.broadcasted_iota(jnp.int32, (1, PAGE), 1)
        kn = jnp.concatenate([sn_ref[:, 0:IDX_D], jnp.zeros((PAGE - t, IDX_D), F32)], axis=0).astype(BF)
        ok = (lane < t) & (t0 + lane <= tq)
        score_scr[:, lpad - PAGE:lpad] = jnp.where(ok, scores(_dot_nt(qf, kn)), NEG)
        s_pos = lax.broadcasted_iota(jnp.int32, (1, lpad), 1)
        key_scr[...] = _sortable(score_scr[...])
        sel = _topk_mask(key_scr, top, s_pos, max(1, (lpad - 1).bit_length()))
        mask_ref[...] = jnp.where(sel & (s_pos <= tq) & (s_pos < t0 + t), 0.0, NEG).astype(BF)


def _dsa_index_sample(qf, wcol, idx_t, table, sm, t, t0):
    bsz, nkb = table.shape
    pps = _pages_per_step(nkb, IDX_PPS)
    nsteps = nkb // pps
    lpad = (nkb + 1) * PAGE
    top = min(IDX_TOPK, (t0 + t) // 4)
    rows = IDX_H * t
    kern = functools.partial(_dsa_index_sample_kernel, t=t, pps=pps, nsteps=nsteps, t0=t0, lpad=lpad, top=top)
    gs = pltpu.PrefetchScalarGridSpec(
        num_scalar_prefetch=1, grid=(bsz, nsteps),
        in_specs=[pl.BlockSpec((rows, IDX_D), lambda b, j, pt: (b, 0)),
                  pl.BlockSpec((rows, 1), lambda b, j, pt: (b, 0))]
        + [pl.BlockSpec((IDX_D, PAGE), lambda b, j, pt, p=p: (pt[b, j * pps + p], 0)) for p in range(pps)]
        + [pl.BlockSpec((t, LANE), lambda b, j, pt: (b, 0))],
        out_specs=pl.BlockSpec((None, t, lpad), lambda b, j, pt: (b, 0, 0)),
        scratch_shapes=[pltpu.VMEM((t, lpad), F32), pltpu.VMEM((t, lpad), jnp.int32)])
    return pl.pallas_call(
        kern, grid_spec=gs, out_shape=jax.ShapeDtypeStruct((bsz, t, lpad), BF),
        compiler_params=_cp(("parallel", "arbitrary")), name="dsa_index_sample")(
            table, qf, wcol, *([idx_t] * pps), sm)


def _silu(z):
    return z * jax.nn.sigmoid(z)


def _combine_nsa_kernel(oc_ref, os_ref, ow_ref, gate_ref, z_ref, o_ref):
    gate = jax.nn.sigmoid(gate_ref[:, 0:3 * NSA_H])
    for h in range(NSA_H):
        sl = slice(h * HD, (h + 1) * HD)
        o = (gate[:, 3 * h:3 * h + 1] * oc_ref[:, sl] + gate[:, 3 * h + 1:3 * h + 2] * os_ref[:, sl]
             + gate[:, 3 * h + 2:3 * h + 3] * ow_ref[:, sl])
        o_ref[:, sl] = (o * _silu(z_ref[:, sl])).astype(BF)


def _combine_nsa(o_cmp, o_sel, o_win, proj, sm, tm):
    m, n = o_cmp.shape
    row = pl.BlockSpec((tm, n), lambda i: (i, 0))
    return pl.pallas_call(
        _combine_nsa_kernel, grid=(m // tm,),
        in_specs=[row, row, row, pl.BlockSpec((tm, LANE), lambda i: (i, 1)),
                  pl.BlockSpec((tm, n), lambda i: (i, C_ZN // n))],
        out_specs=row, out_shape=jax.ShapeDtypeStruct((m, n), BF),
        compiler_params=_cp(("parallel",)), name="combine_nsa")(o_cmp, o_sel, o_win, sm, proj)


def _combine_dsa_kernel(o_ref_in, z_ref, o_ref):
    o_ref[...] = (o_ref_in[...] * _silu(z_ref[...])).astype(BF)


def _combine_dsa(o, proj, tm):
    m, n = o.shape
    row = pl.BlockSpec((tm, n), lambda i: (i, 0))
    return pl.pallas_call(
        _combine_dsa_kernel, grid=(m // tm,),
        in_specs=[row, pl.BlockSpec((tm, n), lambda i: (i, C_ZD // n))],
        out_specs=row, out_shape=jax.ShapeDtypeStruct((m, n), BF),
        compiler_params=_cp(("parallel",)), name="combine_dsa")(o, proj)


PREP_TN = 512
PREP_TK = 1024


def _prep_tables():
    src, off = {}, 0
    for name, width in IN_SPLITS:
        src[name] = off
        off += width
    width = dict(IN_SPLITS)
    shifts, base, cls = [], [], []
    for name, _ in PROJ_LAYOUT:
        for c in range(width[name] // PREP_TN):
            s = src[name] + c * PREP_TN
            if s % LANE not in shifts:
                shifts.append(s % LANE)
            base.append(s // LANE)
            cls.append(shifts.index(s % LANE))
    return tuple(shifts), base, cls


def _prep_kernel(base_ref, cls_ref, *refs, shifts):
    o_ref = refs[-1]
    j = pl.program_id(1)
    for k, s in enumerate(shifts):
        @pl.when(cls_ref[j] == k)
        def _():
            win = jnp.concatenate([r[...] for r in refs[:-1]], axis=0)
            o_ref[...] = win[s:s + PREP_TN, :].T.astype(BF)


def _prep_w_in(w_in_t, l):
    shifts, base, cls = _prep_tables()
    assert all(sh % 8 == 0 for sh in shifts)
    k = w_in_t.shape[2]
    nwin = PREP_TN // LANE + 1
    kern = functools.partial(_prep_kernel, shifts=shifts)
    gs = pltpu.PrefetchScalarGridSpec(
        num_scalar_prefetch=2, grid=(k // PREP_TK, len(base)),
        in_specs=[pl.BlockSpec((None, LANE, PREP_TK), lambda i, j, bs, cs, m=m: (l, bs[j] + m, i))
                  for m in range(nwin)],
        out_specs=pl.BlockSpec((PREP_TK, PREP_TN), lambda i, j, bs, cs: (i, j)))
    return pl.pallas_call(
        kern, grid_spec=gs, out_shape=jax.ShapeDtypeStruct((k, PROJ_W), BF),
        compiler_params=_cp(("parallel", "parallel")), name="prep_w_in")(
            jnp.asarray(base, jnp.int32), jnp.asarray(cls, jnp.int32), *([w_in_t] * nwin))


def _prep_small_kernel(g_ref, a_ref, b_ref, o_ref, *, gate_off, w_off, k_off):
    tk = o_ref.shape[0]
    ab = jnp.concatenate([a_ref[...], b_ref[...]], axis=0)
    rows = jnp.concatenate([ab[k_off:k_off + IDX_D], ab[w_off:w_off + IDX_H], jnp.zeros((32, tk), F32),
                            g_ref[gate_off:gate_off + 3 * NSA_H], jnp.zeros((SM_W - 176, tk), F32)], axis=0)
    o_ref[...] = rows.T.astype(BF)


def _prep_w_small(w_in_t, l):
    src, off = {}, 0
    for name, width in IN_SPLITS:
        src[name] = off
        off += width
    gate_blk, ab_blk = src['nsa_gate'] // LANE, src['idx_w'] // LANE
    offs = dict(gate_off=src['nsa_gate'] - gate_blk * LANE, w_off=src['idx_w'] - ab_blk * LANE,
                k_off=src['idx_k'] - ab_blk * LANE)
    assert all(v % 8 == 0 for v in offs.values()) and offs['k_off'] + IDX_D <= 2 * LANE
    k = w_in_t.shape[2]
    kern = functools.partial(_prep_small_kernel, **offs)
    return pl.pallas_call(
        kern, grid=(k // PREP_TK,),
        in_specs=[pl.BlockSpec((None, LANE, PREP_TK), lambda i, blk=blk: (l, blk, i))
                  for blk in (gate_blk, ab_blk, ab_blk + 1)],
        out_specs=pl.BlockSpec((PREP_TK, SM_W), lambda i: (i, 0)),
        out_shape=jax.ShapeDtypeStruct((k, SM_W), BF),
        compiler_params=_cp(("parallel",)), name="prep_w_small")(w_in_t, w_in_t, w_in_t)


def _slopes(n):
    return jnp.exp2(-8.0 * jnp.arange(1, n + 1, dtype=F32) / n)


def _tail(proj, xf, wts, a_in, b_in, tm_mm, tm_ln):
    hm = _gated_matmul(a_in, b_in, wts['pa'], wts['pd'], proj, tm_mm, 512)
    h = _matmul(hm, wts['out'], tm_mm, 512)
    return _residual_ln(xf, h, wts['gain'], wts['bias'], tm_ln)


def _layer_prompt(xf, xb, bsz, t, wts):
    tm = min(1024, bsz * t)
    proj = _matmul(xb, wts['in'], tm, 1024)
    sm = _matmul(xb, wts['in_small'], tm, SM_W)
    n_pages = t // PAGE
    table = jnp.arange(bsz * n_pages, dtype=jnp.int32).reshape(bsz, n_pages)
    kcvc = _compress(proj, table, C_KVN // (4 * HD), 0, wts['pe_page'], wts['w1'], wts['w2'], n_pages, 1)
    sl_n, sl_d = _slopes(NSA_H), _slopes(DSA_H)
    o_cmp, selm = _nsa_select(proj, kcvc, sl_n, bsz, t, t // CMP, 0, 0)
    o_sel = _fa_prompt("sel", proj, sl_n, bsz, t, C_QN, C_KVN + KV_W, C_KVN + KV_W + GROUPS * HD, selm)
    o_win = _fa_prompt("win", proj, sl_n, bsz, t, C_QN, C_KVN + 2 * KV_W, C_KVN + 2 * KV_W + GROUPS * HD)
    dmask = _dsa_index_prompt(proj, sm, bsz, t)
    o_dsa = _fa_prompt("dsa", proj, sl_d, bsz, t, C_QD, C_KVD, C_KVD + GROUPS * HD, dmask)
    tme = min(256, bsz * t)
    a_in = _combine_nsa(o_cmp, o_sel, o_win, proj, sm, tme)
    b_in = _combine_dsa(o_dsa, proj, tme)
    y, yb = _tail(proj, xf, wts, a_in, b_in, tm, tme)
    return y, yb, proj, sm


def _layer_sample(xf, xb, bsz, t, wts, nsa_rows, win_rows, dsa_rows, idx_t, page_table, win_table):
    rows = bsz * t
    proj = _matmul(xb, wts['in'], rows, 512)
    sm = _matmul(xb, wts['in_small'], rows, SM_W)
    n_pages = page_table.shape[1]
    t0 = n_pages * PAGE
    wbuf = win_table.shape[1] * PAGE
    kcvc = _compress(nsa_rows, page_table, 0, 4 * GROUPS, wts['pe_page'], wts['w1'], wts['w2'], min(64, n_pages),
                     _pages_per_step(min(64, n_pages), FA_PPS))
    sl_n, sl_d = _slopes(NSA_H), _slopes(DSA_H)
    o_cmp, selm = _nsa_select(proj, kcvc, sl_n, bsz, t, -(-(t0 + t) // CMP), t0, n_pages)
    o_sel = _fa_sample("sel", proj, nsa_rows, page_table, sl_n, t, t0, 0, C_QN, 4 * GROUPS, 2, C_KVN + KV_W, selm)
    o_win = _fa_sample("win", proj, win_rows, win_table, sl_n, t, t0, t0 - wbuf, C_QN, 2 * GROUPS, 0,
                       C_KVN + 2 * KV_W)
    qi = proj[:, C_QI:C_QI + IDX_H * IDX_D].reshape(bsz, t, IDX_H, IDX_D).transpose(0, 2, 1, 3)
    qf = qi.reshape(bsz * IDX_H * t, IDX_D)
    wi = sm[:, SM_IDXW:SM_IDXW + IDX_H].reshape(bsz, t, IDX_H).transpose(0, 2, 1)
    wcol = wi.reshape(bsz * IDX_H * t, 1)
    dmask = _dsa_index_sample(qf, wcol, idx_t, page_table, sm, t, t0)
    o_dsa = _fa_sample("dsa", proj, dsa_rows, page_table, sl_d, t, t0, 0, C_QD, 2 * GROUPS, 0, C_KVD, dmask)
    a_in = _combine_nsa(o_cmp, o_sel, o_win, proj, sm, rows)
    b_in = _combine_dsa(o_dsa, proj, rows)
    y, yb = _tail(proj, xf, wts, a_in, b_in, rows, rows)
    return y, yb, proj, sm


def _layer_weights(l, w_in, cmp_pe, cmp_w1, cmp_w2, w_proj_nsa, w_proj_dsa, w_out, ln_gain, ln_bias):
    cmp_pe, cmp_w1, cmp_w2, w_proj_nsa, w_proj_dsa, w_out, ln_gain, ln_bias = (
        a[l] for a in (cmp_pe, cmp_w1, cmp_w2, w_proj_nsa, w_proj_dsa, w_out, ln_gain, ln_bias))
    pe = jnp.concatenate([cmp_pe, cmp_pe], axis=1)
    pe_page = jnp.concatenate([pe[0], pe[0], pe[1], pe[1]], axis=1)
    w_in_t = jnp.swapaxes(w_in, 1, 2)
    return {'in': _prep_w_in(w_in_t, l), 'in_small': _prep_w_small(w_in_t, l), 'pe_page': pe_page,
            'w1': cmp_w1.astype(BF), 'w2': cmp_w2.astype(BF),
            'pa': w_proj_nsa.astype(BF), 'pd': w_proj_dsa.astype(BF), 'out': w_out.astype(BF),
            'gain': ln_gain, 'bias': ln_bias}


def _new_state(proj, sm, bsz, t):
    nsa_kv = proj[:, C_KVN:C_KVN + 2 * KV_W].reshape(bsz, t, 4, GROUPS, HD)
    win = proj[:, C_KVN + 2 * KV_W:C_KVN + 3 * KV_W].reshape(bsz, t, 2, GROUPS, HD)
    dsa_kv = proj[:, C_KVD:C_KVD + KV_W].reshape(bsz, t, 2, GROUPS, HD)
    idx_k = sm[:, 0:IDX_D].reshape(bsz, t, IDX_D)
    return nsa_kv, win, dsa_kv, idx_k


def kernel(x_prompt, x_sample, cache_nsa_kv, state_nsa_win, cache_dsa_kv, cache_dsa_idx, page_table,
           w_in, cmp_pe, cmp_w1, cmp_w2, w_proj_nsa, w_proj_dsa, w_out, ln_gain, ln_bias):
    bp, tp, _ = x_prompt.shape
    bs, ts, _ = x_sample.shape
    n_pool = cache_nsa_kv.shape[1]
    wpages = state_nsa_win.shape[2] // PAGE
    nsa_rows = cache_nsa_kv.reshape(-1, HD)
    dsa_rows = cache_dsa_kv.reshape(-1, HD)
    idx_t = jnp.swapaxes(cache_dsa_idx, 2, 3).reshape(-1, PAGE)
    win_rows = state_nsa_win.reshape(-1, HD)
    win_table = jnp.arange(bs * wpages, dtype=jnp.int32).reshape(bs, wpages)
    yp, ys = x_prompt.reshape(bp * tp, D_MODEL), x_sample.reshape(bs * ts, D_MODEL)
    ypb, ysb = yp.astype(BF), ys.astype(BF)
    outs = [[] for _ in range(8)]
    for l in range(DEPTH):
        wts = _layer_weights(l, w_in, cmp_pe, cmp_w1, cmp_w2, w_proj_nsa, w_proj_dsa, w_out, ln_gain, ln_bias)
        yp, ypb, proj_p, sm_p = _layer_prompt(yp, ypb, bp, tp, wts)
        ys, ysb, proj_s, sm_s = _layer_sample(ys, ysb, bs, ts, wts, nsa_rows, win_rows, dsa_rows, idx_t,
                                              page_table + l * n_pool, win_table + l * bs * wpages)
        nkv_p, win_p, dkv_p, idx_p = _new_state(proj_p, sm_p, bp, tp)
        nkv_s, win_s, dkv_s, idx_s = _new_state(proj_s, sm_s, bs, ts)
        win_all = jnp.concatenate([state_nsa_win[l], win_s], axis=1)
        keep_p, keep_s = min(WINDOW, tp), min(WINDOW, win_all.shape[1])
        for lst, val in zip(outs, (nkv_p, nkv_s, win_p[:, tp - keep_p:], win_all[:, win_all.shape[1] - keep_s:],
                                   dkv_p, dkv_s, idx_p, idx_s)):
            lst.append(val)
    return (yp.reshape(bp, tp, D_MODEL), ys.reshape(bs, ts, D_MODEL)) + tuple(jnp.stack(o) for o in outs)
```

```python
import functools

import jax
import jax.numpy as jnp
from jax import lax
from jax.experimental import pallas as pl
from jax.experimental.pallas import tpu as pltpu

D_MODEL = 4096
DEPTH = 2
PAGE = 128
HD = 128
NSA_H = 16
DSA_H = 16
GROUPS = 2
RH = NSA_H // GROUPS
CMP = 64
CMP_SHIFT = 6
N_SEL = 16
WINDOW = 512
IDX_H = 32
IDX_D = 64
IDX_TOPK = 256
LN_EPS = 1e-5
ALPHA = (2 * DEPTH) ** 0.25
NEG = -1e30
SEL_FORCE = 1e4
SCALE = HD ** -0.5
LOG2E = 1.4426950408889634
INT_MIN = -(2 ** 31)

IN_SPLITS = (('nsa_q', 2048), ('nsa_kv', 1536), ('nsa_gate', 48), ('nsa_z', 2048), ('dsa_q', 2048),
             ('dsa_kv', 512), ('idx_q', 2048), ('idx_w', 32), ('idx_k', 64), ('dsa_z', 2048), ('merge', 8192))

C_QN, C_ZN, C_QD, C_ZD, C_QI, C_MG, C_KVN, C_KVD = 0, 2048, 4096, 6144, 8192, 10240, 18432, 19968
PROJ_W = 20480
PROJ_LAYOUT = (('nsa_q', C_QN), ('nsa_z', C_ZN), ('dsa_q', C_QD), ('dsa_z', C_ZD), ('idx_q', C_QI),
               ('merge', C_MG), ('nsa_kv', C_KVN), ('dsa_kv', C_KVD))
SM_W = 256
SM_IDXW = 64
LANE = 128
KV_W = 2 * GROUPS * HD

VMEM_LIMIT = 48 * 1024 * 1024
BF = jnp.bfloat16
F32 = jnp.float32


def _cp(sem):
    return pltpu.CompilerParams(dimension_semantics=sem, vmem_limit_bytes=VMEM_LIMIT)


def _dot_nt(a, b):
    return lax.dot_general(a, b, (((1,), (1,)), ((), ())), preferred_element_type=F32)


def _dot(a, b):
    return jnp.dot(a, b, preferred_element_type=F32)


def _mm_kernel(x_ref, w_ref, o_ref):
    o_ref[...] = _dot(x_ref[...], w_ref[...]).astype(o_ref.dtype)


def _matmul(x, w, tm, tn, out_dtype=F32):
    m, k = x.shape
    n = w.shape[1]
    return pl.pallas_call(
        _mm_kernel, grid=(m // tm, n // tn),
        in_specs=[pl.BlockSpec((tm, k), lambda i, j: (i, 0)), pl.BlockSpec((k, tn), lambda i, j: (0, j))],
        out_specs=pl.BlockSpec((tm, tn), lambda i, j: (i, j)),
        out_shape=jax.ShapeDtypeStruct((m, n), out_dtype),
        compiler_params=_cp(("parallel", "parallel")), name="matmul")(x, w)


def _gated_mm_kernel(a_ref, b_ref, wa_ref, wb_ref, m0_ref, m1_ref, o_ref):
    a = _dot(a_ref[...], wa_ref[...])
    b = _dot(b_ref[...], wb_ref[...])
    o_ref[...] = (jax.nn.sigmoid(m0_ref[...]) * a + jax.nn.sigmoid(m1_ref[...]) * b).astype(o_ref.dtype)


def _gated_matmul(a_in, b_in, wa, wb, proj, tm, tn):
    m, k = a_in.shape
    n = wa.shape[1]
    c0, c1 = C_MG // tn, (C_MG + D_MODEL) // tn
    return pl.pallas_call(
        _gated_mm_kernel, grid=(m // tm, n // tn),
        in_specs=[pl.BlockSpec((tm, k), lambda i, j: (i, 0)), pl.BlockSpec((tm, k), lambda i, j: (i, 0)),
                  pl.BlockSpec((k, tn), lambda i, j: (0, j)), pl.BlockSpec((k, tn), lambda i, j: (0, j)),
                  pl.BlockSpec((tm, tn), lambda i, j: (i, c0 + j)), pl.BlockSpec((tm, tn), lambda i, j: (i, c1 + j))],
        out_specs=pl.BlockSpec((tm, tn), lambda i, j: (i, j)),
        out_shape=jax.ShapeDtypeStruct((m, n), BF),
        compiler_params=_cp(("parallel", "parallel")), name="gated_matmul")(a_in, b_in, wa, wb, proj, proj)


def _ln_kernel(x_ref, h_ref, g_ref, b_ref, y_ref, yb_ref):
    v = ALPHA * x_ref[...] + h_ref[...]
    mu = jnp.mean(v, axis=-1, keepdims=True)
    c = v - mu
    var = jnp.mean(c * c, axis=-1, keepdims=True)
    y = c * lax.rsqrt(var + LN_EPS) * g_ref[...] + b_ref[...]
    y_ref[...] = y
    yb_ref[...] = y.astype(BF)


def _residual_ln(x, h, gain, bias, tm):
    m, n = x.shape
    row = pl.BlockSpec((tm, n), lambda i: (i, 0))
    vec = pl.BlockSpec((1, n), lambda i: (0, 0))
    return pl.pallas_call(
        _ln_kernel, grid=(m // tm,), in_specs=[row, row, vec, vec], out_specs=[row, row],
        out_shape=[jax.ShapeDtypeStruct((m, n), F32), jax.ShapeDtypeStruct((m, n), BF)],
        compiler_params=_cp(("parallel",)), name="residual_ln")(x, h, gain.reshape(1, n), bias.reshape(1, n))


def _gelu_tanh(x):
    return 0.5 * x * (1.0 + jnp.tanh(0.7978845608028654 * (x + 0.044715 * (x * x * x))))


CMP_TOKC = 8


def _compress_kernel(pt_ref, *refs, pg, pps, rpt):
    page_refs = refs[:pps]
    pe_ref, w1_ref, w2_ref, o_ref, slab_ref = refs[pps:]
    p = pl.program_id(2)
    for k, x_ref in enumerate(page_refs):
        for cg in range(2 * GROUPS):
            x = x_ref[pl.ds(cg, PAGE, stride=rpt), :] if rpt else x_ref[:, cg * HD:(cg + 1) * HD]
            slab_ref[cg, pl.ds(pl.multiple_of((p * pps + k) * PAGE, PAGE), PAGE), :] = (
                x + pe_ref[:, cg * HD:(cg + 1) * HD])

    @pl.when(p == pg // pps - 1)
    def _():
        nblk = pg * (PAGE // CMP)
        for c in range(2):
            acc = jnp.zeros((GROUPS * nblk, HD), F32)
            for tok0 in range(0, CMP, CMP_TOKC):
                lhs = jnp.concatenate(
                    [jnp.concatenate([slab_ref[c * GROUPS + g, pl.ds(tok, nblk, stride=CMP), :]
                                      for g in range(GROUPS)], axis=0).astype(BF)
                     for tok in range(tok0, tok0 + CMP_TOKC)], axis=1)
                acc = acc + _dot(lhs, w1_ref[c, tok0 * HD:(tok0 + CMP_TOKC) * HD, :])
            out = _dot(_gelu_tanh(acc).astype(BF), w2_ref[c])
            for g in range(GROUPS):
                o_ref[c, g] = out[g * nblk:(g + 1) * nblk]


def _compress(src, table, col_blk, rpt, pe_page, w1, w2, pg, pps):
    b, n_pages = table.shape
    ns = n_pages // pg
    nblk = pg * (PAGE // CMP)
    kern = functools.partial(_compress_kernel, pg=pg, pps=pps, rpt=rpt)
    if rpt:
        page_specs = [pl.BlockSpec((PAGE * rpt, HD), lambda bi, s, p, pt, k=k: (pt[bi, s * pg + p * pps + k], 0))
                      for k in range(pps)]
    else:
        page_specs = [pl.BlockSpec((PAGE, 4 * HD), lambda bi, s, p, pt, k=k: (pt[bi, s * pg + p * pps + k], col_blk))
                      for k in range(pps)]
    gs = pltpu.PrefetchScalarGridSpec(
        num_scalar_prefetch=1, grid=(b, ns, pg // pps),
        in_specs=page_specs + [pl.BlockSpec((PAGE, 4 * HD), lambda bi, s, p, pt: (0, 0)),
                               pl.BlockSpec((2, CMP * HD, HD), lambda bi, s, p, pt: (0, 0, 0)),
                               pl.BlockSpec((2, HD, HD), lambda bi, s, p, pt: (0, 0, 0))],
        out_specs=pl.BlockSpec((None, 2, GROUPS, nblk, HD), lambda bi, s, p, pt: (bi, 0, 0, s, 0)),
        scratch_shapes=[pltpu.VMEM((2 * GROUPS, pg * PAGE, HD), F32)])
    return pl.pallas_call(
        kern, grid_spec=gs, out_shape=jax.ShapeDtypeStruct((b, 2, GROUPS, ns * nblk, HD), F32),
        compiler_params=_cp(("parallel", "arbitrary", "arbitrary")), name="compress")(
            table, *([src] * pps), pe_page, w1, w2)


def _sortable(x):
    b = lax.bitcast_convert_type(x, jnp.int32)
    return jnp.where(b < 0, b ^ jnp.int32(0x7FFFFFFF), b)


def _count(pred):
    return jnp.sum(jnp.where(pred, 1.0, 0.0), axis=-1, keepdims=True)


def _topk_mask(key_ref, k, idx, idx_bits):
    rows = key_ref.shape[0]
    kf = float(k)
    zero = jnp.zeros((rows, 1), jnp.int32)
    t0 = jnp.where(_count(key_ref[...] >= zero) >= kf, zero, jnp.full((rows, 1), INT_MIN, jnp.int32))

    def value_bit(i, t):
        cand = t | jnp.left_shift(jnp.int32(1), jnp.int32(30) - i)
        return jnp.where(_count(key_ref[...] >= cand) >= kf, cand, t)

    t = lax.fori_loop(0, 31, value_bit, t0)
    keys = key_ref[...]
    surplus = jnp.max(_count(keys >= t)) > kf

    def tie_cut(_):
        need = kf - _count(key_ref[...] > t)

        def index_bit(i, c):
            cand = c | jnp.left_shift(jnp.int32(1), jnp.int32(idx_bits - 1) - i)
            below = _count((key_ref[...] == t) & (idx < cand))
            return jnp.where(below < need, cand, c)

        return lax.fori_loop(0, idx_bits, index_bit, zero)

    c0 = lax.cond(surplus, tie_cut, lambda _: jnp.full((rows, 1), 2 ** idx_bits - 1, jnp.int32), None)
    return (keys > t) | ((keys == t) & (idx <= c0))


def _masked_softmax(s, mask):
    s = jnp.where(mask, s, NEG)
    e = jnp.where(mask, jnp.exp(s - jnp.max(s, axis=-1, keepdims=True)), 0.0)
    return e / jnp.maximum(jnp.sum(e, axis=-1, keepdims=True), 1e-30)


def _nsa_select_kernel(slopes_ref, q_ref, kc_ref, vc_ref, ocmp_ref, sel_ref, key_scr,
                       *, qb, nb, nbs, nbs_pad, t0, bps):
    g = pl.program_id(1)
    i = pl.program_id(2)
    tq = t0 + i * qb + lax.broadcasted_iota(jnp.int32, (qb, 1), 0)
    blk_end = lax.broadcasted_iota(jnp.int32, (1, nb), 1) * CMP + (CMP - 1)
    valid = blk_end <= tq
    distf = (tq - blk_end).astype(F32)
    kc = kc_ref[...].astype(BF)
    vc = vc_ref[...].astype(BF)
    qs = jnp.concatenate([q_ref[:, r * HD:(r + 1) * HD] for r in range(RH)], axis=0).astype(BF)
    slope = jnp.concatenate([jnp.full((qb, 1), slopes_ref[g * RH + r], F32) for r in range(RH)], axis=0)
    s = _dot_nt(qs, kc) * SCALE - slope * jnp.tile(distf, (RH, 1))
    p = _masked_softmax(s, jnp.tile(valid, (RH, 1)))
    o = _dot(p.astype(BF), vc)
    imp = jnp.zeros((qb, nb), F32)
    for r in range(RH):
        ocmp_ref[:, r * HD:(r + 1) * HD] = o[r * qb:(r + 1) * qb]
        imp = imp + p[r * qb:(r + 1) * qb]
    if nbs_pad > nb:
        imp = jnp.concatenate([imp, jnp.zeros((qb, nbs_pad - nb), F32)], axis=1)
    j = lax.broadcasted_iota(jnp.int32, (1, nbs_pad), 1)
    cur = jnp.right_shift(tq, CMP_SHIFT)
    forced = (j == 0) | (j == cur) | (j == cur - 1)
    score = jnp.where(j <= cur, jnp.where(forced, SEL_FORCE, imp), -SEL_FORCE)
    key_scr[...] = _sortable(score)
    sel = _topk_mask(key_scr, min(N_SEL, nbs), j, max(1, (nbs_pad - 1).bit_length()))
    sel_f = jnp.where(sel, 1.0, 0.0)
    for s_i in range(sel_ref.shape[0]):
        sel_ref[s_i] = jnp.concatenate(
            [sel_f[:, s_i * bps:(s_i + 1) * bps], jnp.zeros((qb, LANE - bps), F32)], axis=1)


def _nsa_select_t_kernel(slopes_ref, q_ref, kc_ref, vc_ref, ocmp_ref, sel_ref, *, qb, nb, n_sel, t0):
    g = pl.program_id(1)
    i = pl.program_id(2)
    n_col = lax.broadcasted_iota(jnp.int32, (nb, qb), 0)
    tq = t0 + i * qb + lax.broadcasted_iota(jnp.int32, (nb, qb), 1)
    blk_end = n_col * CMP + (CMP - 1)
    valid = blk_end <= tq
    distf = (tq - blk_end).astype(F32)
    kc = kc_ref[...].astype(BF)
    vct = vc_ref[...].T.astype(BF)
    qs = jnp.concatenate([q_ref[:, r * HD:(r + 1) * HD] for r in range(RH)], axis=0).astype(BF)
    raw = _dot_nt(kc, qs)
    imp = jnp.zeros((nb, qb), F32)
    probs = []
    for r in range(RH):
        s = jnp.where(valid, raw[:, r * qb:(r + 1) * qb] * SCALE - slopes_ref[g * RH + r] * distf, NEG)
        e = jnp.where(valid, jnp.exp(s - jnp.max(s, axis=0, keepdims=True)), 0.0)
        p = e / jnp.maximum(jnp.sum(e, axis=0, keepdims=True), 1e-30)
        probs.append(p.astype(BF))
        imp = imp + p
    o_t = _dot(vct, jnp.concatenate(probs, axis=1))
    for r in range(RH):
        ocmp_ref[:, r * HD:(r + 1) * HD] = o_t[:, r * qb:(r + 1) * qb].T
    cur = jnp.right_shift(tq, CMP_SHIFT)
    forced = (n_col == 0) | (n_col == cur) | (n_col == cur - 1)
    score = jnp.where(n_col <= cur, jnp.where(forced, SEL_FORCE, imp), -SEL_FORCE)
    rank = jnp.zeros((nb, qb), F32)
    for a in range(nb):
        row = score[a:a + 1, :]
        rank = rank + jnp.where((row > score) | ((row == score) & (n_col > a)), 1.0, 0.0)
    bias_t = jnp.where(rank < float(n_sel), 0.0, NEG)
    for n in range(nb):
        sel_ref[n] = jnp.broadcast_to(bias_t[n:n + 1, :], (8, qb))


def _nsa_select(proj, kcvc, slopes, bsz, t, nbs, t0, past_pages):
    qb = min(t, LANE)
    nq = t // qb
    nb = kcvc.shape[3]
    nbs_pad = nbs if nbs == nb else -(-nbs // LANE) * LANE
    if past_pages:
        bps = _fa_pps(past_pages) * (PAGE // CMP)
        n_grp = past_pages * (PAGE // CMP) // bps + 1
        assert n_grp * bps <= nbs_pad and nq == 1
        kern = functools.partial(_nsa_select_kernel, qb=qb, nb=nb, nbs=nbs, nbs_pad=nbs_pad, t0=t0, bps=bps)
        scratch = [pltpu.VMEM((qb, nbs_pad), jnp.int32)]
    else:
        assert nbs == nb
        kern = functools.partial(_nsa_select_t_kernel, qb=qb, nb=nb, n_sel=min(N_SEL, nbs), t0=t0)
        scratch = []
    wq = RH * HD
    if past_pages:
        sel_spec = pl.BlockSpec((None, None, n_grp, qb, LANE), lambda b, g, i: (b, g, 0, 0, 0))
        sel_shape = jax.ShapeDtypeStruct((bsz, GROUPS, n_grp, t, LANE), F32)
    else:
        sel_spec = pl.BlockSpec((None, None, nbs, 8, qb), lambda b, g, i: (b, g, 0, 0, i))
        sel_shape = jax.ShapeDtypeStruct((bsz, GROUPS, nbs, 8, t), F32)
    return pl.pallas_call(
        kern, grid=(bsz, GROUPS, nq),
        in_specs=[pl.BlockSpec(memory_space=pltpu.SMEM),
                  pl.BlockSpec((qb, wq), lambda b, g, i: (b * nq + i, C_QN // wq + g)),
                  pl.BlockSpec((None, None, None, nb, HD), lambda b, g, i: (b, 0, g, 0, 0)),
                  pl.BlockSpec((None, None, None, nb, HD), lambda b, g, i: (b, 1, g, 0, 0))],
        out_specs=[pl.BlockSpec((qb, wq), lambda b, g, i: (b * nq + i, g)), sel_spec],
        out_shape=[jax.ShapeDtypeStruct((bsz * t, NSA_H * HD), F32), sel_shape],
        scratch_shapes=scratch,
        compiler_params=_cp(("parallel", "parallel", "parallel")), name="nsa_select")(slopes, proj, kcvc, kcvc)


def _fa_prompt_kernel(*refs, mode, qb, kb):
    if mode == "win":
        slopes_ref, q_ref, k_ref, v_ref, o_ref = refs[:5]
        mask_ref = None
    else:
        slopes_ref, q_ref, k_ref, v_ref, mask_ref, o_ref = refs[:6]
    qs_scr, bias_scr, s_scr, p_scr, mb_scr, m_scr, a_scr, acc_scr = refs[-8:]
    g = pl.program_id(1)
    i = pl.program_id(2)
    k_local = lax.broadcasted_iota(jnp.int32, (kb, qb), 0)
    q_local = lax.broadcasted_iota(jnp.int32, (kb, qb), 1)
    for r in range(RH):
        qs_scr[r * qb:(r + 1) * qb, :] = (q_ref[:, r * HD:(r + 1) * HD] * (SCALE * LOG2E)).astype(BF)

    @pl.when(i == 0)
    def _():
        for r in range(RH):
            bias_scr[:, r * qb:(r + 1) * qb] = (slopes_ref[g * RH + r] * LOG2E) * k_local.astype(F32)

    m_scr[...] = jnp.full(m_scr.shape, NEG, F32)
    acc_scr[...] = jnp.zeros(acc_scr.shape, F32)
    j_hi = ((i + 1) * qb - 1) // kb
    j_lo = jnp.maximum(i * qb - (WINDOW - 1), 0) // kb if mode == "win" else 0

    def body(j, carry):
        off = pl.multiple_of(j * kb, kb)
        s_scr[...] = _dot_nt(k_ref[pl.ds(off, kb), :].astype(BF), qs_scr[...])
        dist = (i * qb + q_local) - (j * kb + k_local)
        ok = dist >= 0
        if mode == "win":
            ok = ok & (dist < WINDOW)
        mb = jnp.where(ok, 0.0, NEG)
        if mode == "sel":
            tiles = mask_ref[pl.ds(j * (kb // CMP), kb // CMP)]
            mb = mb + jnp.concatenate([jnp.tile(tiles[n], (CMP // 8, 1)) for n in range(kb // CMP)], axis=0)
        elif mode == "dsa":
            mb = mb + mask_ref[pl.ds(off, kb), :].astype(F32)
        mb_scr[...] = mb
        cbase = (j * kb - i * qb).astype(F32)
        for r in range(RH):
            sl = slice(r * qb, (r + 1) * qb)
            c = (slopes_ref[g * RH + r] * LOG2E) * cbase
            x = s_scr[:, sl] + bias_scr[:, sl] + mb_scr[...]
            m_prev = m_scr[:, sl]
            m_new = jnp.maximum(m_prev, jnp.max(x, axis=0, keepdims=True) + c)
            p_scr[:, sl] = jnp.exp2(x - (m_new - c)).astype(BF)
            a_scr[:, sl] = jnp.exp2(m_prev - m_new)
            m_scr[:, sl] = m_new
        vt = jnp.concatenate([v_ref[pl.ds(off, kb), :].T, jnp.ones((16, kb), F32)], axis=0).astype(BF)
        acc_scr[...] = acc_scr[...] * a_scr[...] + _dot(vt, p_scr[...])
        return carry

    lax.fori_loop(j_lo, j_hi + 1, body, 0)
    for r in range(RH):
        sl = slice(r * qb, (r + 1) * qb)
        o_t = jnp.where(m_scr[:, sl] > 0.5 * NEG,
                        acc_scr[0:HD, sl] / jnp.maximum(acc_scr[HD:HD + 1, sl], 1e-30), 0.0)
        o_ref[:, r * HD:(r + 1) * HD] = o_t.T


def _fa_prompt(mode, proj, slopes, bsz, t, q_col, k_col, v_col, mask=None):
    qb = LANE
    kb = min(2 * LANE, t)
    nq = t // qb
    wq = RH * HD
    in_specs = [pl.BlockSpec(memory_space=pltpu.SMEM),
                pl.BlockSpec((qb, wq), lambda b, g, i: (b * nq + i, q_col // wq + g)),
                pl.BlockSpec((t, HD), lambda b, g, i: (b, k_col // HD + g)),
                pl.BlockSpec((t, HD), lambda b, g, i: (b, v_col // HD + g))]
    args = [slopes, proj, proj, proj]
    if mode == "sel":
        in_specs.append(pl.BlockSpec((None, None, t // CMP, 8, qb), lambda b, g, i: (b, g, 0, 0, i)))
        args.append(mask)
    elif mode == "dsa":
        in_specs.append(pl.BlockSpec((None, t, qb), lambda b, g, i: (b, 0, i)))
        args.append(mask)
    kern = functools.partial(_fa_prompt_kernel, mode=mode, qb=qb, kb=kb)
    lanes = RH * qb
    return pl.pallas_call(
        kern, grid=(bsz, GROUPS, nq), in_specs=in_specs,
        out_specs=pl.BlockSpec((qb, wq), lambda b, g, i: (b * nq + i, g)),
        out_shape=jax.ShapeDtypeStruct((bsz * t, GROUPS * wq), F32),
        scratch_shapes=[pltpu.VMEM((lanes, HD), BF), pltpu.VMEM((kb, lanes), F32), pltpu.VMEM((kb, lanes), F32),
                        pltpu.VMEM((kb, lanes), BF), pltpu.VMEM((kb, qb), F32), pltpu.VMEM((1, lanes), F32),
                        pltpu.VMEM((1, lanes), F32), pltpu.VMEM((HD + 16, lanes), F32)],
        compiler_params=_cp(("parallel", "parallel", "arbitrary")), name="fa_prompt_" + mode)(*args)


PPS = 4
FA_PPS = 16
IDX_PPS = 16


def _pages_per_step(n_pages, want):
    return want if n_pages % want == 0 else PPS


def _fa_sample_kernel(*refs, mode, t, pps, nsteps, t0, kpos0, rpt, kcomp):
    pt_ref, slope_ref, q_ref = refs[:3]
    page_refs = refs[3:3 + pps]
    new_ref = refs[3 + pps]
    rest = refs[4 + pps:]
    if mode == "win":
        mask_ref = maskn_ref = None
        o_ref, qs_scr, m_scr, l_scr, acc_scr = rest
    else:
        mask_ref, maskn_ref, o_ref, qs_scr, m_scr, l_scr, acc_scr = rest
    j = pl.program_id(1)
    bps = pps * (PAGE // CMP)

    @pl.when(j == 0)
    def _():
        for g in range(GROUPS):
            qs_scr[g] = jnp.concatenate(
                [q_ref[:, (g * RH + r) * HD:(g * RH + r + 1) * HD] for r in range(RH)], axis=0).astype(BF)
        m_scr[...] = jnp.full(m_scr.shape, NEG, F32)
        l_scr[...] = jnp.zeros(l_scr.shape, F32)
        acc_scr[...] = jnp.zeros(acc_scr.shape, F32)

    tq = t0 + jnp.concatenate([lax.broadcasted_iota(jnp.int32, (t, 1), 0)] * RH, axis=0)

    def key_bias(mref, g, nk):
        if mode == "sel":
            n = lax.broadcasted_iota(jnp.int32, (bps, nk), 0)
            kk = lax.broadcasted_iota(jnp.int32, (bps, nk), 1)
            expand = jnp.where(n == jnp.right_shift(kk, CMP_SHIFT), 1.0, 0.0).astype(BF)
            bias = (_dot(mref[g][:, 0:bps].astype(BF), expand) - 1.0) * (-NEG)
        else:
            bias = mref[...].astype(F32)
        return jnp.tile(bias, (RH, 1))

    def step(kparts, vparts, kpos, extra_ok, mref):
        nk = kpos.shape[1]
        dist = tq - kpos
        ok = dist >= 0
        if mode == "win":
            ok = ok & (dist < WINDOW)
        if extra_ok is not None:
            ok = ok & extra_ok
        distf = dist.astype(F32)
        raw = [_dot_nt(qs_scr[g], jnp.concatenate(kparts[g], axis=0).astype(BF)) for g in range(GROUPS)]
        probs = []
        for g in range(GROUPS):
            s = jnp.where(ok, raw[g] * SCALE - slope_ref[g] * distf, NEG)
            if mref is not None:
                s = s + key_bias(mref, g, nk)
            m_prev = m_scr[g]
            m_new = jnp.maximum(m_prev, jnp.max(s, axis=-1, keepdims=True))
            a = jnp.exp(m_prev - m_new)
            p = jnp.exp(s - m_new)
            l_scr[g] = a * l_scr[g] + jnp.sum(p, axis=-1, keepdims=True)
            m_scr[g] = m_new
            probs.append((a, p.astype(BF)))
        for g in range(GROUPS):
            a, p = probs[g]
            acc_scr[g] = a * acc_scr[g] + _dot(p, jnp.concatenate(vparts[g], axis=0).astype(BF))

    lane = lax.broadcasted_iota(jnp.int32, (1, pps * PAGE), 1)
    step([[r[pl.ds(kcomp * GROUPS + g, PAGE, stride=rpt), :] for r in page_refs] for g in range(GROUPS)],
         [[r[pl.ds((kcomp + 1) * GROUPS + g, PAGE, stride=rpt), :] for r in page_refs] for g in range(GROUPS)],
         kpos0 + j * (pps * PAGE) + lane, None, mask_ref)

    @pl.when(j == nsteps - 1)
    def _():
        lane1 = lax.broadcasted_iota(jnp.int32, (1, PAGE), 1)
        kv_new = jnp.concatenate([new_ref[...], jnp.zeros((PAGE - t, KV_W), F32)], axis=0)
        step([[kv_new[:, g * HD:(g + 1) * HD]] for g in range(GROUPS)],
             [[kv_new[:, (GROUPS + g) * HD:(GROUPS + g + 1) * HD]] for g in range(GROUPS)],
             t0 + lane1, lane1 < t, maskn_ref)
        for g in range(GROUPS):
            o = jnp.where(m_scr[g] > 0.5 * NEG, acc_scr[g] / jnp.maximum(l_scr[g], 1e-30), 0.0)
            for r in range(RH):
                o_ref[:, (g * RH + r) * HD:(g * RH + r + 1) * HD] = o[r * t:(r + 1) * t]


def _fa_pps(n_pages):
    return _pages_per_step(n_pages, FA_PPS)


def _fa_sample(mode, proj, past, table, slopes, t, t0, kpos0, q_col, rpt, kcomp, new_col, mask=None):
    bsz, nkb = table.shape
    pps = _fa_pps(nkb)
    nsteps = nkb // pps
    rows = RH * t
    wq = GROUPS * RH * HD
    slope_col = jnp.repeat(slopes.reshape(GROUPS, RH), t, axis=1).reshape(GROUPS, rows, 1)
    in_specs = [pl.BlockSpec((GROUPS, rows, 1), lambda b, j, pt: (0, 0, 0)),
                pl.BlockSpec((t, wq), lambda b, j, pt: (b, q_col // wq))]
    for p in range(pps):
        in_specs.append(pl.BlockSpec((PAGE * rpt, HD), lambda b, j, pt, p=p: (pt[b, j * pps + p], 0)))
    in_specs.append(pl.BlockSpec((t, KV_W), lambda b, j, pt: (b, new_col // KV_W)))
    args = [slope_col, proj] + [past] * pps + [proj]
    if mode == "sel":
        in_specs.append(pl.BlockSpec((None, GROUPS, None, t, LANE), lambda b, j, pt: (b, 0, j, 0, 0)))
        in_specs.append(pl.BlockSpec((None, GROUPS, None, t, LANE), lambda b, j, pt: (b, 0, nsteps, 0, 0)))
        args += [mask, mask]
    elif mode == "dsa":
        in_specs.append(pl.BlockSpec((None, t, pps * PAGE), lambda b, j, pt: (b, 0, j)))
        in_specs.append(pl.BlockSpec((None, t, PAGE), lambda b, j, pt: (b, 0, nkb)))
        args += [mask, mask]
    kern = functools.partial(_fa_sample_kernel, mode=mode, t=t, pps=pps, nsteps=nsteps, t0=t0, kpos0=kpos0,
                             rpt=rpt, kcomp=kcomp)
    gs = pltpu.PrefetchScalarGridSpec(
        num_scalar_prefetch=1, grid=(bsz, nsteps), in_specs=in_specs,
        out_specs=pl.BlockSpec((t, wq), lambda b, j, pt: (b, 0)),
        scratch_shapes=[pltpu.VMEM((GROUPS, rows, HD), BF), pltpu.VMEM((GROUPS, rows, 1), F32),
                        pltpu.VMEM((GROUPS, rows, 1), F32), pltpu.VMEM((GROUPS, rows, HD), F32)])
    return pl.pallas_call(
        kern, grid_spec=gs, out_shape=jax.ShapeDtypeStruct((bsz * t, wq), F32),
        compiler_params=_cp(("parallel", "arbitrary")), name="fa_sample_" + mode)(table, *args)


IDX_CHUNK = 256


def _dsa_index_prompt_kernel(q_ref, sq_ref, sk_ref, mask_ref, qb_scr, wb_scr, kpad_scr, score_scr, key_scr,
                             *, qb, t, top):
    i = pl.program_id(1)
    tq = i * qb + lax.broadcasted_iota(jnp.int32, (qb, 1), 0)

    @pl.when(i == 0)
    def _():
        kk = sk_ref[:, 0:IDX_D]
        zero = jnp.zeros((t, IDX_D), F32)
        kpad_scr[0] = jnp.concatenate([kk, zero], axis=1).astype(BF)
        kpad_scr[1] = jnp.concatenate([zero, kk], axis=1).astype(BF)

    qb_scr[...] = (q_ref[...] * (IDX_D ** -0.5)).astype(BF)
    w = sq_ref[:, SM_IDXW:SM_IDXW + IDX_H] * (IDX_H ** -0.5)
    for h in range(IDX_H):
        wb_scr[h] = jnp.broadcast_to(w[:, h:h + 1], (qb, LANE))
    score_scr[...] = jnp.zeros(score_scr.shape, F32)
    ch = min(IDX_CHUNK, t)

    def chunk(c, carry):
        off = pl.multiple_of(c * ch, ch)
        acc = jnp.zeros((qb, ch), F32)
        for pair in range(IDX_H // 2):
            qp = qb_scr[:, pair * LANE:(pair + 1) * LANE]
            for e in range(2):
                lg = _dot_nt(qp, kpad_scr[e, pl.ds(off, ch), :])
                acc = acc + jnp.tile(wb_scr[2 * pair + e], (1, ch // LANE)) * jnp.maximum(lg, 0.0)
        score_scr[:, pl.ds(off, ch)] = acc
        return carry

    lax.fori_loop(0, ((i + 1) * qb + ch - 1) // ch, chunk, 0)
    s_pos = lax.broadcasted_iota(jnp.int32, (1, t), 1)
    causal = s_pos <= tq
    key_scr[...] = _sortable(jnp.where(causal, score_scr[...], NEG))
    sel = _topk_mask(key_scr, top, s_pos, max(1, (t - 1).bit_length()))
    bias = jnp.where(sel & causal, 0.0, NEG)
    for c in range(t // LANE):
        mask_ref[c * LANE:(c + 1) * LANE, :] = bias[:, c * LANE:(c + 1) * LANE].T.astype(BF)


def _dsa_index_prompt(proj, sm, bsz, t):
    qb = LANE
    nq = t // qb
    top = min(IDX_TOPK, t // 4)
    kern = functools.partial(_dsa_index_prompt_kernel, qb=qb, t=t, top=top)
    wq = IDX_H * IDX_D
    return pl.pallas_call(
        kern, grid=(bsz, nq),
        in_specs=[pl.BlockSpec((qb, wq), lambda b, i: (b * nq + i, C_QI // wq)),
                  pl.BlockSpec((qb, LANE), lambda b, i: (b * nq + i, 0)),
                  pl.BlockSpec((t, LANE), lambda b, i: (b, 0))],
        out_specs=pl.BlockSpec((None, t, qb), lambda b, i: (b, 0, i)),
        out_shape=jax.ShapeDtypeStruct((bsz, t, t), BF),
        scratch_shapes=[pltpu.VMEM((qb, wq), BF), pltpu.VMEM((IDX_H, qb, LANE), F32),
                        pltpu.VMEM((2, t, 2 * IDX_D), BF), pltpu.VMEM((qb, t), F32), pltpu.VMEM((qb, t), jnp.int32)],
        compiler_params=_cp(("parallel", "arbitrary")), name="dsa_index_prompt")(proj, sm, sm)


def _dsa_index_sample_kernel(pt_ref, qf_ref, wcol_ref, *refs, t, pps, nsteps, t0, lpad, top):
    page_refs = refs[:pps]
    sn_ref, mask_ref, score_scr, key_scr = refs[pps:]
    j = pl.program_id(1)

    @pl.when(j == 0)
    def _():
        score_scr[...] = jnp.full(score_scr.shape, NEG, F32)

    qf = qf_ref[...].astype(BF)
    wcol = wcol_ref[...] * (IDX_H ** -0.5)

    def scores(logits):
        r = jnp.maximum(logits * (IDX_D ** -0.5), 0.0) * wcol
        sc = r[0:t]
        for h in range(1, IDX_H):
            sc = sc + r[h * t:(h + 1) * t]
        return sc

    kpages_t = jnp.concatenate([r[...] for r in page_refs], axis=1).astype(BF)
    score_scr[:, pl.ds(pl.multiple_of(j * (pps * PAGE), pps * PAGE), pps * PAGE)] = scores(_dot(qf, kpages_t))

    @pl.when(j == nsteps - 1)
    def _():
        tq = t0 + lax.broadcasted_iota(jnp.int32, (t, 1), 0)
        lane = lax.broadcasted_iota(jnp.int32, (1, PAGE), 1)
        kn = jnp.concatenate([sn_ref[:, 0:IDX_D], jnp.zeros((PAGE - t, IDX_D), F32)], axis=0).astype(BF)
        ok = (lane < t) & (t0 + lane <= tq)
        score_scr[:, lpad - PAGE:lpad] = jnp.where(ok, scores(_dot_nt(qf, kn)), NEG)
        s_pos = lax.broadcasted_iota(jnp.int32, (1, lpad), 1)
        key_scr[...] = _sortable(score_scr[...])
        sel = _topk_mask(key_scr, top, s_pos, max(1, (lpad - 1).bit_length()))
        mask_ref[...] = jnp.where(sel & (s_pos <= tq) & (s_pos < t0 + t), 0.0, NEG).astype(BF)


def _dsa_index_sample(qf, wcol, idx_t, table, sm, t, t0):
    bsz, nkb = table.shape
    pps = _pages_per_step(nkb, IDX_PPS)
    nsteps = nkb // pps
    lpad = (nkb + 1) * PAGE
    top = min(IDX_TOPK, (t0 + t) // 4)
    rows = IDX_H * t
    kern = functools.partial(_dsa_index_sample_kernel, t=t, pps=pps, nsteps=nsteps, t0=t0, lpad=lpad, top=top)
    gs = pltpu.PrefetchScalarGridSpec(
        num_scalar_prefetch=1, grid=(bsz, nsteps),
        in_specs=[pl.BlockSpec((rows, IDX_D), lambda b, j, pt: (b, 0)),
                  pl.BlockSpec((rows, 1), lambda b, j, pt: (b, 0))]
        + [pl.BlockSpec((IDX_D, PAGE), lambda b, j, pt, p=p: (pt[b, j * pps + p], 0)) for p in range(pps)]
        + [pl.BlockSpec((t, LANE), lambda b, j, pt: (b, 0))],
        out_specs=pl.BlockSpec((None, t, lpad), lambda b, j, pt: (b, 0, 0)),
        scratch_shapes=[pltpu.VMEM((t, lpad), F32), pltpu.VMEM((t, lpad), jnp.int32)])
    return pl.pallas_call(
        kern, grid_spec=gs, out_shape=jax.ShapeDtypeStruct((bsz, t, lpad), BF),
        compiler_params=_cp(("parallel", "arbitrary")), name="dsa_index_sample")(
            table, qf, wcol, *([idx_t] * pps), sm)


def _silu(z):
    return z * jax.nn.sigmoid(z)


def _combine_nsa_kernel(oc_ref, os_ref, ow_ref, gate_ref, z_ref, o_ref):
    gate = jax.nn.sigmoid(gate_ref[:, 0:3 * NSA_H])
    for h in range(NSA_H):
        sl = slice(h * HD, (h + 1) * HD)
        o = (gate[:, 3 * h:3 * h + 1] * oc_ref[:, sl] + gate[:, 3 * h + 1:3 * h + 2] * os_ref[:, sl]
             + gate[:, 3 * h + 2:3 * h + 3] * ow_ref[:, sl])
        o_ref[:, sl] = (o * _silu(z_ref[:, sl])).astype(BF)


def _combine_nsa(o_cmp, o_sel, o_win, proj, sm, tm):
    m, n = o_cmp.shape
    row = pl.BlockSpec((tm, n), lambda i: (i, 0))
    return pl.pallas_call(
        _combine_nsa_kernel, grid=(m // tm,),
        in_specs=[row, row, row, pl.BlockSpec((tm, LANE), lambda i: (i, 1)),
                  pl.BlockSpec((tm, n), lambda i: (i, C_ZN // n))],
        out_specs=row, out_shape=jax.ShapeDtypeStruct((m, n), BF),
        compiler_params=_cp(("parallel",)), name="combine_nsa")(o_cmp, o_sel, o_win, sm, proj)


def _combine_dsa_kernel(o_ref_in, z_ref, o_ref):
    o_ref[...] = (o_ref_in[...] * _silu(z_ref[...])).astype(BF)


def _combine_dsa(o, proj, tm):
    m, n = o.shape
    row = pl.BlockSpec((tm, n), lambda i: (i, 0))
    return pl.pallas_call(
        _combine_dsa_kernel, grid=(m // tm,),
        in_specs=[row, pl.BlockSpec((tm, n), lambda i: (i, C_ZD // n))],
        out_specs=row, out_shape=jax.ShapeDtypeStruct((m, n), BF),
        compiler_params=_cp(("parallel",)), name="combine_dsa")(o, proj)


PREP_TN = 512
PREP_TK = 1024


def _prep_tables():
    src, off = {}, 0
    for name, width in IN_SPLITS:
        src[name] = off
        off += width
    width = dict(IN_SPLITS)
    shifts, base, cls = [], [], []
    for name, _ in PROJ_LAYOUT:
        for c in range(width[name] // PREP_TN):
            s = src[name] + c * PREP_TN
            if s % LANE not in shifts:
                shifts.append(s % LANE)
            base.append(s // LANE)
            cls.append(shifts.index(s % LANE))
    return tuple(shifts), base, cls


def _prep_kernel(base_ref, cls_ref, *refs, shifts):
    o_ref = refs[-1]
    j = pl.program_id(1)
    for k, s in enumerate(shifts):
        @pl.when(cls_ref[j] == k)
        def _():
            win = jnp.concatenate([r[...] for r in refs[:-1]], axis=0)
            o_ref[...] = win[s:s + PREP_TN, :].T.astype(BF)


def _prep_w_in(w_in_t, l):
    shifts, base, cls = _prep_tables()
    assert all(sh % 8 == 0 for sh in shifts)
    k = w_in_t.shape[2]
    nwin = PREP_TN // LANE + 1
    kern = functools.partial(_prep_kernel, shifts=shifts)
    gs = pltpu.PrefetchScalarGridSpec(
        num_scalar_prefetch=2, grid=(k // PREP_TK, len(base)),
        in_specs=[pl.BlockSpec((None, LANE, PREP_TK), lambda i, j, bs, cs, m=m: (l, bs[j] + m, i))
                  for m in range(nwin)],
        out_specs=pl.BlockSpec((PREP_TK, PREP_TN), lambda i, j, bs, cs: (i, j)))
    return pl.pallas_call(
        kern, grid_spec=gs, out_shape=jax.ShapeDtypeStruct((k, PROJ_W), BF),
        compiler_params=_cp(("parallel", "parallel")), name="prep_w_in")(
            jnp.asarray(base, jnp.int32), jnp.asarray(cls, jnp.int32), *([w_in_t] * nwin))


def _prep_small_kernel(g_ref, a_ref, b_ref, o_ref, *, gate_off, w_off, k_off):
    tk = o_ref.shape[0]
    ab = jnp.concatenate([a_ref[...], b_ref[...]], axis=0)
    rows = jnp.concatenate([ab[k_off:k_off + IDX_D], ab[w_off:w_off + IDX_H], jnp.zeros((32, tk), F32),
                            g_ref[gate_off:gate_off + 3 * NSA_H], jnp.zeros((SM_W - 176, tk), F32)], axis=0)
    o_ref[...] = rows.T.astype(BF)


def _prep_w_small(w_in_t, l):
    src, off = {}, 0
    for name, width in IN_SPLITS:
        src[name] = off
        off += width
    gate_blk, ab_blk = src['nsa_gate'] // LANE, src['idx_w'] // LANE
    offs = dict(gate_off=src['nsa_gate'] - gate_blk * LANE, w_off=src['idx_w'] - ab_blk * LANE,
                k_off=src['idx_k'] - ab_blk * LANE)
    assert all(v % 8 == 0 for v in offs.values()) and offs['k_off'] + IDX_D <= 2 * LANE
    k = w_in_t.shape[2]
    kern = functools.partial(_prep_small_kernel, **offs)
    return pl.pallas_call(
        kern, grid=(k // PREP_TK,),
        in_specs=[pl.BlockSpec((None, LANE, PREP_TK), lambda i, blk=blk: (l, blk, i))
                  for blk in (gate_blk, ab_blk, ab_blk + 1)],
        out_specs=pl.BlockSpec((PREP_TK, SM_W), lambda i: (i, 0)),
        out_shape=jax.ShapeDtypeStruct((k, SM_W), BF),
        compiler_params=_cp(("parallel",)), name="prep_w_small")(w_in_t, w_in_t, w_in_t)


def _slopes(n):
    return jnp.exp2(-8.0 * jnp.arange(1, n + 1, dtype=F32) / n)


def _tail(proj, xf, wts, a_in, b_in, tm_mm, tm_ln):
    hm = _gated_matmul(a_in, b_in, wts['pa'], wts['pd'], proj, tm_mm, 512)
    h = _matmul(hm, wts['out'], tm_mm, 512)
    return _residual_ln(xf, h, wts['gain'], wts['bias'], tm_ln)


def _layer_prompt(xf, xb, bsz, t, wts):
    tm = min(1024, bsz * t)
    proj = _matmul(xb, wts['in'], tm, 1024)
    sm = _matmul(xb, wts['in_small'], tm, SM_W)
    n_pages = t // PAGE
    table = jnp.arange(bsz * n_pages, dtype=jnp.int32).reshape(bsz, n_pages)
    kcvc = _compress(proj, table, C_KVN // (4 * HD), 0, wts['pe_page'], wts['w1'], wts['w2'], n_pages, 1)
    sl_n, sl_d = _slopes(NSA_H), _slopes(DSA_H)
    o_cmp, selm = _nsa_select(proj, kcvc, sl_n, bsz, t, t // CMP, 0, 0)
    o_sel = _fa_prompt("sel", proj, sl_n, bsz, t, C_QN, C_KVN + KV_W, C_KVN + KV_W + GROUPS * HD, selm)
    o_win = _fa_prompt("win", proj, sl_n, bsz, t, C_QN, C_KVN + 2 * KV_W, C_KVN + 2 * KV_W + GROUPS * HD)
    dmask = _dsa_index_prompt(proj, sm, bsz, t)
    o_dsa = _fa_prompt("dsa", proj, sl_d, bsz, t, C_QD, C_KVD, C_KVD + GROUPS * HD, dmask)
    tme = min(256, bsz * t)
    a_in = _combine_nsa(o_cmp, o_sel, o_win, proj, sm, tme)
    b_in = _combine_dsa(o_dsa, proj, tme)
    y, yb = _tail(proj, xf, wts, a_in, b_in, tm, tme)
    return y, yb, proj, sm


def _layer_sample(xf, xb, bsz, t, wts, nsa_rows, win_rows, dsa_rows, idx_t, page_table, win_table):
    rows = bsz * t
    proj = _matmul(xb, wts['in'], rows, 512)
    sm = _matmul(xb, wts['in_small'], rows, SM_W)
    n_pages = page_table.shape[1]
    t0 = n_pages * PAGE
    wbuf = win_table.shape[1] * PAGE
    kcvc = _compress(nsa_rows, page_table, 0, 4 * GROUPS, wts['pe_page'], wts['w1'], wts['w2'], min(64, n_pages),
                     _pages_per_step(min(64, n_pages), FA_PPS))
    sl_n, sl_d = _slopes(NSA_H), _slopes(DSA_H)
    o_cmp, selm = _nsa_select(proj, kcvc, sl_n, bsz, t, -(-(t0 + t) // CMP), t0, n_pages)
    o_sel = _fa_sample("sel", proj, nsa_rows, page_table, sl_n, t, t0, 0, C_QN, 4 * GROUPS, 2, C_KVN + KV_W, selm)
    o_win = _fa_sample("win", proj, win_rows, win_table, sl_n, t, t0, t0 - wbuf, C_QN, 2 * GROUPS, 0,
                       C_KVN + 2 * KV_W)
    qi = proj[:, C_QI:C_QI + IDX_H * IDX_D].reshape(bsz, t, IDX_H, IDX_D).transpose(0, 2, 1, 3)
    qf = qi.reshape(bsz * IDX_H * t, IDX_D)
    wi = sm[:, SM_IDXW:SM_IDXW + IDX_H].reshape(bsz, t, IDX_H).transpose(0, 2, 1)
    wcol = wi.reshape(bsz * IDX_H * t, 1)
    dmask = _dsa_index_sample(qf, wcol, idx_t, page_table, sm, t, t0)
    o_dsa = _fa_sample("dsa", proj, dsa_rows, page_table, sl_d, t, t0, 0, C_QD, 2 * GROUPS, 0, C_KVD, dmask)
    a_in = _combine_nsa(o_cmp, o_sel, o_win, proj, sm, rows)
    b_in = _combine_dsa(o_dsa, proj, rows)
    y, yb = _tail(proj, xf, wts, a_in, b_in, rows, rows)
    return y, yb, proj, sm


def _layer_weights(l, w_in, cmp_pe, cmp_w1, cmp_w2, w_proj_nsa, w_proj_dsa, w_out, ln_gain, ln_bias):
    cmp_pe, cmp_w1, cmp_w2, w_proj_nsa, w_proj_dsa, w_out, ln_gain, ln_bias = (
        a[l] for a in (cmp_pe, cmp_w1, cmp_w2, w_proj_nsa, w_proj_dsa, w_out, ln_gain, ln_bias))
    pe = jnp.concatenate([cmp_pe, cmp_pe], axis=1)
    pe_page = jnp.concatenate([pe[0], pe[0], pe[1], pe[1]], axis=1)
    w_in_t = jnp.swapaxes(w_in, 1, 2)
    return {'in': _prep_w_in(w_in_t, l), 'in_small': _prep_w_small(w_in_t, l), 'pe_page': pe_page,
            'w1': cmp_w1.astype(BF), 'w2': cmp_w2.astype(BF),
            'pa': w_proj_nsa.astype(BF), 'pd': w_proj_dsa.astype(BF), 'out': w_out.astype(BF),
            'gain': ln_gain, 'bias': ln_bias}


STATE_TM = 256


def _state_kernel(x_ref, *refs, tm):
    nsa_ref, dsa_ref = refs[-2:]
    for cg in range(4 * GROUPS):
        nsa_ref[pl.ds(cg, tm, stride=4 * GROUPS), :] = x_ref[:, cg * HD:(cg + 1) * HD]
    for cg in range(2 * GROUPS):
        dsa_ref[pl.ds(cg, tm, stride=2 * GROUPS), :] = x_ref[:, 3 * KV_W + cg * HD:3 * KV_W + (cg + 1) * HD]


def _write_state(proj, l, prev):
    m = proj.shape[0]
    tm = min(STATE_TM, m)
    nt = m // tm
    kern = functools.partial(_state_kernel, tm=tm)
    width = 4 * KV_W
    in_specs = [pl.BlockSpec((tm, width), lambda i: (i, C_KVN // width))]
    args = [proj]
    aliases = {}
    if prev is not None:
        in_specs += [pl.BlockSpec(memory_space=pl.ANY), pl.BlockSpec(memory_space=pl.ANY)]
        args += list(prev)
        aliases = {1: 0, 2: 1}
    return pl.pallas_call(
        kern, grid=(nt,), in_specs=in_specs,
        out_specs=[pl.BlockSpec((tm * 4 * GROUPS, HD), lambda i: (l * nt + i, 0)),
                   pl.BlockSpec((tm * 2 * GROUPS, HD), lambda i: (l * nt + i, 0))],
        out_shape=[jax.ShapeDtypeStruct((DEPTH * m * 4 * GROUPS, HD), F32),
                   jax.ShapeDtypeStruct((DEPTH * m * 2 * GROUPS, HD), F32)],
        input_output_aliases=aliases,
        compiler_params=_cp(("arbitrary",)), name="write_state")(*args)


def _win_and_idx(proj, sm, bsz, t):
    win = proj[:, C_KVN + 2 * KV_W:C_KVN + 3 * KV_W].reshape(bsz, t, 2, GROUPS, HD)
    idx_k = sm[:, 0:IDX_D].reshape(bsz, t, IDX_D)
    return win, idx_k


def kernel(x_prompt, x_sample, cache_nsa_kv, state_nsa_win, cache_dsa_kv, cache_dsa_idx, page_table,
           w_in, cmp_pe, cmp_w1, cmp_w2, w_proj_nsa, w_proj_dsa, w_out, ln_gain, ln_bias):
    bp, tp, _ = x_prompt.shape
    bs, ts, _ = x_sample.shape
    n_pool = cache_nsa_kv.shape[1]
    wpages = state_nsa_win.shape[2] // PAGE
    nsa_rows = cache_nsa_kv.reshape(-1, HD)
    dsa_rows = cache_dsa_kv.reshape(-1, HD)
    idx_t = jnp.swapaxes(cache_dsa_idx, 2, 3).reshape(-1, PAGE)
    win_rows = state_nsa_win.reshape(-1, HD)
    win_table = jnp.arange(bs * wpages, dtype=jnp.int32).reshape(bs, wpages)
    yp, ys = x_prompt.reshape(bp * tp, D_MODEL), x_sample.reshape(bs * ts, D_MODEL)
    ypb, ysb = yp.astype(BF), ys.astype(BF)
    outs = [[] for _ in range(4)]
    state_p = state_s = None
    for l in range(DEPTH):
        wts = _layer_weights(l, w_in, cmp_pe, cmp_w1, cmp_w2, w_proj_nsa, w_proj_dsa, w_out, ln_gain, ln_bias)
        yp, ypb, proj_p, sm_p = _layer_prompt(yp, ypb, bp, tp, wts)
        ys, ysb, proj_s, sm_s = _layer_sample(ys, ysb, bs, ts, wts, nsa_rows, win_rows, dsa_rows, idx_t,
                                              page_table + l * n_pool, win_table + l * bs * wpages)
        state_p = _write_state(proj_p, l, state_p)
        state_s = _write_state(proj_s, l, state_s)
        win_p, idx_p = _win_and_idx(proj_p, sm_p, bp, tp)
        win_s, idx_s = _win_and_idx(proj_s, sm_s, bs, ts)
        win_all = jnp.concatenate([state_nsa_win[l], win_s], axis=1)
        keep_p, keep_s = min(WINDOW, tp), min(WINDOW, win_all.shape[1])
        for lst, val in zip(outs, (win_p[:, tp - keep_p:], win_all[:, win_all.shape[1] - keep_s:], idx_p, idx_s)):
            lst.append(val)
    win_po, win_so, idx_po, idx_so = (jnp.stack(o) for o in outs)
    return (yp.reshape(bp, tp, D_MODEL), ys.reshape(bs, ts, D_MODEL),
            state_p[0].reshape(DEPTH, bp, tp, 4, GROUPS, HD), state_s[0].reshape(DEPTH, bs, ts, 4, GROUPS, HD),
            win_po, win_so,
            state_p[1].reshape(DEPTH, bp, tp, 2, GROUPS, HD), state_s[1].reshape(DEPTH, bs, ts, 2, GROUPS, HD),
            idx_po, idx_so)
```

```python
import functools

import jax
import jax.numpy as jnp
from jax import lax
from jax.experimental import pallas as pl
from jax.experimental.pallas import tpu as pltpu

D_MODEL = 4096
DEPTH = 2
PAGE = 128
HD = 128
NSA_H = 16
DSA_H = 16
GROUPS = 2
RH = NSA_H // GROUPS
CMP = 64
CMP_SHIFT = 6
N_SEL = 16
WINDOW = 512
IDX_H = 32
IDX_D = 64
IDX_TOPK = 256
LN_EPS = 1e-5
ALPHA = (2 * DEPTH) ** 0.25
NEG = -1e30
SEL_FORCE = 1e4
SCALE = HD ** -0.5
LOG2E = 1.4426950408889634
INT_MIN = -(2 ** 31)

IN_SPLITS = (('nsa_q', 2048), ('nsa_kv', 1536), ('nsa_gate', 48), ('nsa_z', 2048), ('dsa_q', 2048),
             ('dsa_kv', 512), ('idx_q', 2048), ('idx_w', 32), ('idx_k', 64), ('dsa_z', 2048), ('merge', 8192))

C_QN, C_ZN, C_QD, C_ZD, C_QI, C_MG, C_KVN, C_KVD = 0, 2048, 4096, 6144, 8192, 10240, 18432, 19968
PROJ_W = 20480
PROJ_LAYOUT = (('nsa_q', C_QN), ('nsa_z', C_ZN), ('dsa_q', C_QD), ('dsa_z', C_ZD), ('idx_q', C_QI),
               ('merge', C_MG), ('nsa_kv', C_KVN), ('dsa_kv', C_KVD))
SM_W = 256
SM_IDXW = 64
LANE = 128
KV_W = 2 * GROUPS * HD

VMEM_LIMIT = 48 * 1024 * 1024
BF = jnp.bfloat16
F32 = jnp.float32


def _cp(sem):
    return pltpu.CompilerParams(dimension_semantics=sem, vmem_limit_bytes=VMEM_LIMIT)


def _dot_nt(a, b):
    return lax.dot_general(a, b, (((1,), (1,)), ((), ())), preferred_element_type=F32)


def _dot(a, b):
    return jnp.dot(a, b, preferred_element_type=F32)


def _mm_kernel(x_ref, w_ref, o_ref):
    o_ref[...] = _dot(x_ref[...], w_ref[...]).astype(o_ref.dtype)


def _w_spec(w, l, tn):
    k = w.shape[-2]
    if w.ndim == 3:
        return pl.BlockSpec((None, k, tn), lambda i, j: (l, 0, j))
    return pl.BlockSpec((k, tn), lambda i, j: (0, j))


def _matmul(x, w, tm, tn, out_dtype=F32, l=0):
    m, k = x.shape
    n = w.shape[-1]
    return pl.pallas_call(
        _mm_kernel, grid=(m // tm, n // tn),
        in_specs=[pl.BlockSpec((tm, k), lambda i, j: (i, 0)), _w_spec(w, l, tn)],
        out_specs=pl.BlockSpec((tm, tn), lambda i, j: (i, j)),
        out_shape=jax.ShapeDtypeStruct((m, n), out_dtype),
        compiler_params=_cp(("parallel", "parallel")), name="matmul")(x, w)


def _gated_mm_kernel(a_ref, b_ref, wa_ref, wb_ref, m0_ref, m1_ref, o_ref):
    a = _dot(a_ref[...], wa_ref[...])
    b = _dot(b_ref[...], wb_ref[...])
    o_ref[...] = (jax.nn.sigmoid(m0_ref[...]) * a + jax.nn.sigmoid(m1_ref[...]) * b).astype(o_ref.dtype)


def _gated_matmul(a_in, b_in, wa, wb, proj, tm, tn, l=0):
    m, k = a_in.shape
    n = wa.shape[-1]
    c0, c1 = C_MG // tn, (C_MG + D_MODEL) // tn
    return pl.pallas_call(
        _gated_mm_kernel, grid=(m // tm, n // tn),
        in_specs=[pl.BlockSpec((tm, k), lambda i, j: (i, 0)), pl.BlockSpec((tm, k), lambda i, j: (i, 0)),
                  _w_spec(wa, l, tn), _w_spec(wb, l, tn),
                  pl.BlockSpec((tm, tn), lambda i, j: (i, c0 + j)), pl.BlockSpec((tm, tn), lambda i, j: (i, c1 + j))],
        out_specs=pl.BlockSpec((tm, tn), lambda i, j: (i, j)),
        out_shape=jax.ShapeDtypeStruct((m, n), BF),
        compiler_params=_cp(("parallel", "parallel")), name="gated_matmul")(a_in, b_in, wa, wb, proj, proj)


def _ln_kernel(x_ref, h_ref, g_ref, b_ref, y_ref, yb_ref):
    v = ALPHA * x_ref[...] + h_ref[...]
    mu = jnp.mean(v, axis=-1, keepdims=True)
    c = v - mu
    var = jnp.mean(c * c, axis=-1, keepdims=True)
    y = c * lax.rsqrt(var + LN_EPS) * g_ref[...] + b_ref[...]
    y_ref[...] = y
    yb_ref[...] = y.astype(BF)


def _residual_ln(x, h, gain, bias, tm):
    m, n = x.shape
    row = pl.BlockSpec((tm, n), lambda i: (i, 0))
    vec = pl.BlockSpec((1, n), lambda i: (0, 0))
    return pl.pallas_call(
        _ln_kernel, grid=(m // tm,), in_specs=[row, row, vec, vec], out_specs=[row, row],
        out_shape=[jax.ShapeDtypeStruct((m, n), F32), jax.ShapeDtypeStruct((m, n), BF)],
        compiler_params=_cp(("parallel",)), name="residual_ln")(x, h, gain.reshape(1, n), bias.reshape(1, n))


def _gelu_tanh(x):
    return 0.5 * x * (1.0 + jnp.tanh(0.7978845608028654 * (x + 0.044715 * (x * x * x))))


CMP_TOKC = 8


def _compress_kernel(pt_ref, *refs, pg, pps, rpt):
    page_refs = refs[:pps]
    pe_ref, w1_ref, w2_ref, o_ref, slab_ref = refs[pps:]
    p = pl.program_id(2)
    for k, x_ref in enumerate(page_refs):
        for cg in range(2 * GROUPS):
            x = x_ref[pl.ds(cg, PAGE, stride=rpt), :] if rpt else x_ref[:, cg * HD:(cg + 1) * HD]
            slab_ref[cg, pl.ds(pl.multiple_of((p * pps + k) * PAGE, PAGE), PAGE), :] = (
                x + pe_ref[:, cg * HD:(cg + 1) * HD])

    @pl.when(p == pg // pps - 1)
    def _():
        nblk = pg * (PAGE // CMP)
        for c in range(2):
            acc = jnp.zeros((GROUPS * nblk, HD), F32)
            for tok0 in range(0, CMP, CMP_TOKC):
                lhs = jnp.concatenate(
                    [jnp.concatenate([slab_ref[c * GROUPS + g, pl.ds(tok, nblk, stride=CMP), :]
                                      for g in range(GROUPS)], axis=0).astype(BF)
                     for tok in range(tok0, tok0 + CMP_TOKC)], axis=1)
                acc = acc + _dot(lhs, w1_ref[c, tok0 * HD:(tok0 + CMP_TOKC) * HD, :])
            out = _dot(_gelu_tanh(acc).astype(BF), w2_ref[c])
            for g in range(GROUPS):
                o_ref[c, g] = out[g * nblk:(g + 1) * nblk]


def _compress(src, table, col_blk, rpt, pe_page, w1, w2, pg, pps):
    b, n_pages = table.shape
    ns = n_pages // pg
    nblk = pg * (PAGE // CMP)
    kern = functools.partial(_compress_kernel, pg=pg, pps=pps, rpt=rpt)
    if rpt:
        page_specs = [pl.BlockSpec((PAGE * rpt, HD), lambda bi, s, p, pt, k=k: (pt[bi, s * pg + p * pps + k], 0))
                      for k in range(pps)]
    else:
        page_specs = [pl.BlockSpec((PAGE, 4 * HD), lambda bi, s, p, pt, k=k: (pt[bi, s * pg + p * pps + k], col_blk))
                      for k in range(pps)]
    gs = pltpu.PrefetchScalarGridSpec(
        num_scalar_prefetch=1, grid=(b, ns, pg // pps),
        in_specs=page_specs + [pl.BlockSpec((PAGE, 4 * HD), lambda bi, s, p, pt: (0, 0)),
                               pl.BlockSpec((2, CMP * HD, HD), lambda bi, s, p, pt: (0, 0, 0)),
                               pl.BlockSpec((2, HD, HD), lambda bi, s, p, pt: (0, 0, 0))],
        out_specs=pl.BlockSpec((None, 2, GROUPS, nblk, HD), lambda bi, s, p, pt: (bi, 0, 0, s, 0)),
        scratch_shapes=[pltpu.VMEM((2 * GROUPS, pg * PAGE, HD), F32)])
    return pl.pallas_call(
        kern, grid_spec=gs, out_shape=jax.ShapeDtypeStruct((b, 2, GROUPS, ns * nblk, HD), F32),
        compiler_params=_cp(("parallel", "arbitrary", "arbitrary")), name="compress")(
            table, *([src] * pps), pe_page, w1, w2)


def _sortable(x):
    b = lax.bitcast_convert_type(x, jnp.int32)
    return jnp.where(b < 0, b ^ jnp.int32(0x7FFFFFFF), b)


def _count(pred):
    return jnp.sum(jnp.where(pred, 1.0, 0.0), axis=-1, keepdims=True)


def _topk_mask(key_ref, k, idx, idx_bits):
    rows = key_ref.shape[0]
    kf = float(k)
    zero = jnp.zeros((rows, 1), jnp.int32)
    t0 = jnp.where(_count(key_ref[...] >= zero) >= kf, zero, jnp.full((rows, 1), INT_MIN, jnp.int32))

    def value_bit(i, t):
        cand = t | jnp.left_shift(jnp.int32(1), jnp.int32(30) - i)
        return jnp.where(_count(key_ref[...] >= cand) >= kf, cand, t)

    t = lax.fori_loop(0, 31, value_bit, t0)
    keys = key_ref[...]
    surplus = jnp.max(_count(keys >= t)) > kf

    def tie_cut(_):
        need = kf - _count(key_ref[...] > t)

        def index_bit(i, c):
            cand = c | jnp.left_shift(jnp.int32(1), jnp.int32(idx_bits - 1) - i)
            below = _count((key_ref[...] == t) & (idx < cand))
            return jnp.where(below < need, cand, c)

        return lax.fori_loop(0, idx_bits, index_bit, zero)

    c0 = lax.cond(surplus, tie_cut, lambda _: jnp.full((rows, 1), 2 ** idx_bits - 1, jnp.int32), None)
    return (keys > t) | ((keys == t) & (idx <= c0))


def _masked_softmax(s, mask):
    s = jnp.where(mask, s, NEG)
    e = jnp.where(mask, jnp.exp(s - jnp.max(s, axis=-1, keepdims=True)), 0.0)
    return e / jnp.maximum(jnp.sum(e, axis=-1, keepdims=True), 1e-30)


def _nsa_select_kernel(slopes_ref, q_ref, kc_ref, vc_ref, ocmp_ref, sel_ref, key_scr,
                       *, qb, nb, nbs, nbs_pad, t0, bps):
    g = pl.program_id(1)
    i = pl.program_id(2)
    tq = t0 + i * qb + lax.broadcasted_iota(jnp.int32, (qb, 1), 0)
    blk_end = lax.broadcasted_iota(jnp.int32, (1, nb), 1) * CMP + (CMP - 1)
    valid = blk_end <= tq
    distf = (tq - blk_end).astype(F32)
    kc = kc_ref[...].astype(BF)
    vc = vc_ref[...].astype(BF)
    qs = jnp.concatenate([q_ref[:, r * HD:(r + 1) * HD] for r in range(RH)], axis=0).astype(BF)
    slope = jnp.concatenate([jnp.full((qb, 1), slopes_ref[g * RH + r], F32) for r in range(RH)], axis=0)
    s = _dot_nt(qs, kc) * SCALE - slope * jnp.tile(distf, (RH, 1))
    p = _masked_softmax(s, jnp.tile(valid, (RH, 1)))
    o = _dot(p.astype(BF), vc)
    imp = jnp.zeros((qb, nb), F32)
    for r in range(RH):
        ocmp_ref[:, r * HD:(r + 1) * HD] = o[r * qb:(r + 1) * qb]
        imp = imp + p[r * qb:(r + 1) * qb]
    if nbs_pad > nb:
        imp = jnp.concatenate([imp, jnp.zeros((qb, nbs_pad - nb), F32)], axis=1)
    j = lax.broadcasted_iota(jnp.int32, (1, nbs_pad), 1)
    cur = jnp.right_shift(tq, CMP_SHIFT)
    forced = (j == 0) | (j == cur) | (j == cur - 1)
    score = jnp.where(j <= cur, jnp.where(forced, SEL_FORCE, imp), -SEL_FORCE)
    key_scr[...] = _sortable(score)
    sel = _topk_mask(key_scr, min(N_SEL, nbs), j, max(1, (nbs_pad - 1).bit_length()))
    sel_f = jnp.where(sel, 1.0, 0.0)
    for s_i in range(sel_ref.shape[0]):
        sel_ref[s_i] = jnp.concatenate(
            [sel_f[:, s_i * bps:(s_i + 1) * bps], jnp.zeros((qb, LANE - bps), F32)], axis=1)


def _nsa_select_t_kernel(slopes_ref, q_ref, kc_ref, vc_ref, ocmp_ref, sel_ref, *, qb, nb, n_sel, t0):
    g = pl.program_id(1)
    i = pl.program_id(2)
    n_col = lax.broadcasted_iota(jnp.int32, (nb, qb), 0)
    tq = t0 + i * qb + lax.broadcasted_iota(jnp.int32, (nb, qb), 1)
    blk_end = n_col * CMP + (CMP - 1)
    valid = blk_end <= tq
    distf = (tq - blk_end).astype(F32)
    kc = kc_ref[...].astype(BF)
    vct = vc_ref[...].T.astype(BF)
    qs = jnp.concatenate([q_ref[:, r * HD:(r + 1) * HD] for r in range(RH)], axis=0).astype(BF)
    raw = _dot_nt(kc, qs)
    imp = jnp.zeros((nb, qb), F32)
    probs = []
    for r in range(RH):
        s = jnp.where(valid, raw[:, r * qb:(r + 1) * qb] * SCALE - slopes_ref[g * RH + r] * distf, NEG)
        e = jnp.where(valid, jnp.exp(s - jnp.max(s, axis=0, keepdims=True)), 0.0)
        p = e / jnp.maximum(jnp.sum(e, axis=0, keepdims=True), 1e-30)
        probs.append(p.astype(BF))
        imp = imp + p
    o_t = _dot(vct, jnp.concatenate(probs, axis=1))
    for r in range(RH):
        ocmp_ref[:, r * HD:(r + 1) * HD] = o_t[:, r * qb:(r + 1) * qb].T
    cur = jnp.right_shift(tq, CMP_SHIFT)
    forced = (n_col == 0) | (n_col == cur) | (n_col == cur - 1)
    score = jnp.where(n_col <= cur, jnp.where(forced, SEL_FORCE, imp), -SEL_FORCE)
    rank = jnp.zeros((nb, qb), F32)
    for a in range(nb):
        row = score[a:a + 1, :]
        rank = rank + jnp.where((row > score) | ((row == score) & (n_col > a)), 1.0, 0.0)
    bias_t = jnp.where(rank < float(n_sel), 0.0, NEG)
    for n in range(nb):
        sel_ref[n] = jnp.broadcast_to(bias_t[n:n + 1, :], (8, qb))


def _nsa_select(proj, kcvc, slopes, bsz, t, nbs, t0, past_pages):
    qb = min(t, LANE)
    nq = t // qb
    nb = kcvc.shape[3]
    nbs_pad = nbs if nbs == nb else -(-nbs // LANE) * LANE
    if past_pages:
        bps = _fa_pps(past_pages) * (PAGE // CMP)
        n_grp = past_pages * (PAGE // CMP) // bps + 1
        assert n_grp * bps <= nbs_pad and nq == 1
        kern = functools.partial(_nsa_select_kernel, qb=qb, nb=nb, nbs=nbs, nbs_pad=nbs_pad, t0=t0, bps=bps)
        scratch = [pltpu.VMEM((qb, nbs_pad), jnp.int32)]
    else:
        assert nbs == nb
        kern = functools.partial(_nsa_select_t_kernel, qb=qb, nb=nb, n_sel=min(N_SEL, nbs), t0=t0)
        scratch = []
    wq = RH * HD
    if past_pages:
        sel_spec = pl.BlockSpec((None, None, n_grp, qb, LANE), lambda b, g, i: (b, g, 0, 0, 0))
        sel_shape = jax.ShapeDtypeStruct((bsz, GROUPS, n_grp, t, LANE), F32)
    else:
        sel_spec = pl.BlockSpec((None, None, nbs, 8, qb), lambda b, g, i: (b, g, 0, 0, i))
        sel_shape = jax.ShapeDtypeStruct((bsz, GROUPS, nbs, 8, t), F32)
    return pl.pallas_call(
        kern, grid=(bsz, GROUPS, nq),
        in_specs=[pl.BlockSpec(memory_space=pltpu.SMEM),
                  pl.BlockSpec((qb, wq), lambda b, g, i: (b * nq + i, C_QN // wq + g)),
                  pl.BlockSpec((None, None, None, nb, HD), lambda b, g, i: (b, 0, g, 0, 0)),
                  pl.BlockSpec((None, None, None, nb, HD), lambda b, g, i: (b, 1, g, 0, 0))],
        out_specs=[pl.BlockSpec((qb, wq), lambda b, g, i: (b * nq + i, g)), sel_spec],
        out_shape=[jax.ShapeDtypeStruct((bsz * t, NSA_H * HD), F32), sel_shape],
        scratch_shapes=scratch,
        compiler_params=_cp(("parallel", "parallel", "parallel")), name="nsa_select")(slopes, proj, kcvc, kcvc)


def _fa_prompt_kernel(*refs, mode, qb, kb):
    if mode == "win":
        slopes_ref, q_ref, k_ref, v_ref, o_ref = refs[:5]
        mask_ref = None
    else:
        slopes_ref, q_ref, k_ref, v_ref, mask_ref, o_ref = refs[:6]
    qs_scr, bias_scr, s_scr, p_scr, mb_scr, m_scr, a_scr, acc_scr = refs[-8:]
    g = pl.program_id(1)
    i = pl.program_id(2)
    k_local = lax.broadcasted_iota(jnp.int32, (kb, qb), 0)
    q_local = lax.broadcasted_iota(jnp.int32, (kb, qb), 1)
    for r in range(RH):
        qs_scr[r * qb:(r + 1) * qb, :] = (q_ref[:, r * HD:(r + 1) * HD] * (SCALE * LOG2E)).astype(BF)

    @pl.when(i == 0)
    def _():
        for r in range(RH):
            bias_scr[:, r * qb:(r + 1) * qb] = (slopes_ref[g * RH + r] * LOG2E) * k_local.astype(F32)

    m_scr[...] = jnp.full(m_scr.shape, NEG, F32)
    acc_scr[...] = jnp.zeros(acc_scr.shape, F32)
    j_hi = ((i + 1) * qb - 1) // kb
    j_lo = jnp.maximum(i * qb - (WINDOW - 1), 0) // kb if mode == "win" else 0

    def body(j, carry):
        off = pl.multiple_of(j * kb, kb)
        s_scr[...] = _dot_nt(k_ref[pl.ds(off, kb), :].astype(BF), qs_scr[...])
        dist = (i * qb + q_local) - (j * kb + k_local)
        ok = dist >= 0
        if mode == "win":
            ok = ok & (dist < WINDOW)
        mb = jnp.where(ok, 0.0, NEG)
        if mode == "sel":
            tiles = mask_ref[pl.ds(j * (kb // CMP), kb // CMP)]
            mb = mb + jnp.concatenate([jnp.tile(tiles[n], (CMP // 8, 1)) for n in range(kb // CMP)], axis=0)
        elif mode == "dsa":
            mb = mb + mask_ref[pl.ds(off, kb), :].astype(F32)
        mb_scr[...] = mb
        cbase = (j * kb - i * qb).astype(F32)
        for ch in range(RH * qb // LANE):
            r, h = divmod(ch, qb // LANE)
            sl = slice(ch * LANE, (ch + 1) * LANE)
            c = (slopes_ref[g * RH + r] * LOG2E) * cbase
            x = s_scr[:, sl] + bias_scr[:, sl] + mb_scr[:, h * LANE:(h + 1) * LANE]
            m_prev = m_scr[:, sl]
            m_new = jnp.maximum(m_prev, jnp.max(x, axis=0, keepdims=True) + c)
            p_scr[:, sl] = jnp.exp2(x - (m_new - c)).astype(BF)
            a_scr[:, sl] = jnp.exp2(m_prev - m_new)
            m_scr[:, sl] = m_new
        vt = jnp.concatenate([v_ref[pl.ds(off, kb), :].T, jnp.ones((16, kb), F32)], axis=0).astype(BF)
        acc_scr[...] = acc_scr[...] * a_scr[...] + _dot(vt, p_scr[...])
        return carry

    lax.fori_loop(j_lo, j_hi + 1, body, 0)
    for r in range(RH):
        sl = slice(r * qb, (r + 1) * qb)
        o_t = jnp.where(m_scr[:, sl] > 0.5 * NEG,
                        acc_scr[0:HD, sl] / jnp.maximum(acc_scr[HD:HD + 1, sl], 1e-30), 0.0)
        o_ref[:, r * HD:(r + 1) * HD] = o_t.T


def _fa_prompt(mode, proj, slopes, bsz, t, q_col, k_col, v_col, mask=None):
    qb = min(2 * LANE, t)
    kb = min(2 * LANE, t)
    nq = t // qb
    wq = RH * HD
    in_specs = [pl.BlockSpec(memory_space=pltpu.SMEM),
                pl.BlockSpec((qb, wq), lambda b, g, i: (b * nq + i, q_col // wq + g)),
                pl.BlockSpec((t, HD), lambda b, g, i: (b, k_col // HD + g)),
                pl.BlockSpec((t, HD), lambda b, g, i: (b, v_col // HD + g))]
    args = [slopes, proj, proj, proj]
    if mode == "sel":
        in_specs.append(pl.BlockSpec((None, None, t // CMP, 8, qb), lambda b, g, i: (b, g, 0, 0, i)))
        args.append(mask)
    elif mode == "dsa":
        in_specs.append(pl.BlockSpec((None, t, qb), lambda b, g, i: (b, 0, i)))
        args.append(mask)
    kern = functools.partial(_fa_prompt_kernel, mode=mode, qb=qb, kb=kb)
    lanes = RH * qb
    return pl.pallas_call(
        kern, grid=(bsz, GROUPS, nq), in_specs=in_specs,
        out_specs=pl.BlockSpec((qb, wq), lambda b, g, i: (b * nq + i, g)),
        out_shape=jax.ShapeDtypeStruct((bsz * t, GROUPS * wq), F32),
        scratch_shapes=[pltpu.VMEM((lanes, HD), BF), pltpu.VMEM((kb, lanes), F32), pltpu.VMEM((kb, lanes), F32),
                        pltpu.VMEM((kb, lanes), BF), pltpu.VMEM((kb, qb), F32), pltpu.VMEM((1, lanes), F32),
                        pltpu.VMEM((1, lanes), F32), pltpu.VMEM((HD + 16, lanes), F32)],
        compiler_params=_cp(("parallel", "parallel", "arbitrary")), name="fa_prompt_" + mode)(*args)


PPS = 4
FA_PPS = 16
IDX_PPS = 16


def _pages_per_step(n_pages, want):
    return want if n_pages % want == 0 else PPS


def _fa_sample_kernel(*refs, mode, t, pps, nsteps, t0, kpos0, rpt, kcomp):
    pt_ref, slope_ref, q_ref = refs[:3]
    page_refs = refs[3:3 + pps]
    new_ref = refs[3 + pps]
    rest = refs[4 + pps:]
    if mode == "win":
        mask_ref = maskn_ref = None
        o_ref, qs_scr, m_scr, l_scr, acc_scr = rest
    else:
        mask_ref, maskn_ref, o_ref, qs_scr, m_scr, l_scr, acc_scr = rest
    j = pl.program_id(1)
    bps = pps * (PAGE // CMP)

    @pl.when(j == 0)
    def _():
        for g in range(GROUPS):
            qs_scr[g] = jnp.concatenate(
                [q_ref[:, (g * RH + r) * HD:(g * RH + r + 1) * HD] for r in range(RH)], axis=0).astype(BF)
        m_scr[...] = jnp.full(m_scr.shape, NEG, F32)
        l_scr[...] = jnp.zeros(l_scr.shape, F32)
        acc_scr[...] = jnp.zeros(acc_scr.shape, F32)

    tq = t0 + jnp.concatenate([lax.broadcasted_iota(jnp.int32, (t, 1), 0)] * RH, axis=0)

    def key_bias(mref, g, nk):
        if mode == "sel":
            n = lax.broadcasted_iota(jnp.int32, (bps, nk), 0)
            kk = lax.broadcasted_iota(jnp.int32, (bps, nk), 1)
            expand = jnp.where(n == jnp.right_shift(kk, CMP_SHIFT), 1.0, 0.0).astype(BF)
            bias = (_dot(mref[g][:, 0:bps].astype(BF), expand) - 1.0) * (-NEG)
        else:
            bias = mref[...].astype(F32)
        return jnp.tile(bias, (RH, 1))

    def step(kparts, vparts, kpos, extra_ok, mref):
        nk = kpos.shape[1]
        dist = tq - kpos
        ok = dist >= 0
        if mode == "win":
            ok = ok & (dist < WINDOW)
        if extra_ok is not None:
            ok = ok & extra_ok
        distf = dist.astype(F32)
        raw = [_dot_nt(qs_scr[g], jnp.concatenate(kparts[g], axis=0).astype(BF)) for g in range(GROUPS)]
        probs = []
        for g in range(GROUPS):
            s = jnp.where(ok, raw[g] * SCALE - slope_ref[g] * distf, NEG)
            if mref is not None:
                s = s + key_bias(mref, g, nk)
            m_prev = m_scr[g]
            m_new = jnp.maximum(m_prev, jnp.max(s, axis=-1, keepdims=True))
            a = jnp.exp(m_prev - m_new)
            p = jnp.exp(s - m_new)
            l_scr[g] = a * l_scr[g] + jnp.sum(p, axis=-1, keepdims=True)
            m_scr[g] = m_new
            probs.append((a, p.astype(BF)))
        for g in range(GROUPS):
            a, p = probs[g]
            acc_scr[g] = a * acc_scr[g] + _dot(p, jnp.concatenate(vparts[g], axis=0).astype(BF))

    lane = lax.broadcasted_iota(jnp.int32, (1, pps * PAGE), 1)
    step([[r[pl.ds(kcomp * GROUPS + g, PAGE, stride=rpt), :] for r in page_refs] for g in range(GROUPS)],
         [[r[pl.ds((kcomp + 1) * GROUPS + g, PAGE, stride=rpt), :] for r in page_refs] for g in range(GROUPS)],
         kpos0 + j * (pps * PAGE) + lane, None, mask_ref)

    @pl.when(j == nsteps - 1)
    def _():
        lane1 = lax.broadcasted_iota(jnp.int32, (1, PAGE), 1)
        kv_new = jnp.concatenate([new_ref[...], jnp.zeros((PAGE - t, KV_W), F32)], axis=0)
        step([[kv_new[:, g * HD:(g + 1) * HD]] for g in range(GROUPS)],
             [[kv_new[:, (GROUPS + g) * HD:(GROUPS + g + 1) * HD]] for g in range(GROUPS)],
             t0 + lane1, lane1 < t, maskn_ref)
        for g in range(GROUPS):
            o = jnp.where(m_scr[g] > 0.5 * NEG, acc_scr[g] / jnp.maximum(l_scr[g], 1e-30), 0.0)
            for r in range(RH):
                o_ref[:, (g * RH + r) * HD:(g * RH + r + 1) * HD] = o[r * t:(r + 1) * t]


def _fa_pps(n_pages):
    return _pages_per_step(n_pages, FA_PPS)


def _fa_sample(mode, proj, past, table, slopes, t, t0, kpos0, q_col, rpt, kcomp, new_col, mask=None):
    bsz, nkb = table.shape
    pps = _fa_pps(nkb)
    nsteps = nkb // pps
    rows = RH * t
    wq = GROUPS * RH * HD
    slope_col = jnp.repeat(slopes.reshape(GROUPS, RH), t, axis=1).reshape(GROUPS, rows, 1)
    in_specs = [pl.BlockSpec((GROUPS, rows, 1), lambda b, j, pt: (0, 0, 0)),
                pl.BlockSpec((t, wq), lambda b, j, pt: (b, q_col // wq))]
    for p in range(pps):
        in_specs.append(pl.BlockSpec((PAGE * rpt, HD), lambda b, j, pt, p=p: (pt[b, j * pps + p], 0)))
    in_specs.append(pl.BlockSpec((t, KV_W), lambda b, j, pt: (b, new_col // KV_W)))
    args = [slope_col, proj] + [past] * pps + [proj]
    if mode == "sel":
        in_specs.append(pl.BlockSpec((None, GROUPS, None, t, LANE), lambda b, j, pt: (b, 0, j, 0, 0)))
        in_specs.append(pl.BlockSpec((None, GROUPS, None, t, LANE), lambda b, j, pt: (b, 0, nsteps, 0, 0)))
        args += [mask, mask]
    elif mode == "dsa":
        in_specs.append(pl.BlockSpec((None, t, pps * PAGE), lambda b, j, pt: (b, 0, j)))
        in_specs.append(pl.BlockSpec((None, t, PAGE), lambda b, j, pt: (b, 0, nkb)))
        args += [mask, mask]
    kern = functools.partial(_fa_sample_kernel, mode=mode, t=t, pps=pps, nsteps=nsteps, t0=t0, kpos0=kpos0,
                             rpt=rpt, kcomp=kcomp)
    gs = pltpu.PrefetchScalarGridSpec(
        num_scalar_prefetch=1, grid=(bsz, nsteps), in_specs=in_specs,
        out_specs=pl.BlockSpec((t, wq), lambda b, j, pt: (b, 0)),
        scratch_shapes=[pltpu.VMEM((GROUPS, rows, HD), BF), pltpu.VMEM((GROUPS, rows, 1), F32),
                        pltpu.VMEM((GROUPS, rows, 1), F32), pltpu.VMEM((GROUPS, rows, HD), F32)])
    return pl.pallas_call(
        kern, grid_spec=gs, out_shape=jax.ShapeDtypeStruct((bsz * t, wq), F32),
        compiler_params=_cp(("parallel", "arbitrary")), name="fa_sample_" + mode)(table, *args)


IDX_CHUNK = 256


def _dsa_index_prompt_kernel(q_ref, sq_ref, sk_ref, mask_ref, qb_scr, wb_scr, kpad_scr, score_scr, key_scr,
                             *, qb, t, top):
    i = pl.program_id(1)
    tq = i * qb + lax.broadcasted_iota(jnp.int32, (qb, 1), 0)

    @pl.when(i == 0)
    def _():
        kk = sk_ref[:, 0:IDX_D]
        zero = jnp.zeros((t, IDX_D), F32)
        kpad_scr[0] = jnp.concatenate([kk, zero], axis=1).astype(BF)
        kpad_scr[1] = jnp.concatenate([zero, kk], axis=1).astype(BF)

    qb_scr[...] = (q_ref[...] * (IDX_D ** -0.5)).astype(BF)
    w = sq_ref[:, SM_IDXW:SM_IDXW + IDX_H] * (IDX_H ** -0.5)
    for h in range(IDX_H):
        wb_scr[h] = jnp.broadcast_to(w[:, h:h + 1], (qb, LANE))
    score_scr[...] = jnp.zeros(score_scr.shape, F32)
    ch = min(IDX_CHUNK, t)

    def chunk(c, carry):
        off = pl.multiple_of(c * ch, ch)
        acc = jnp.zeros((qb, ch), F32)
        for pair in range(IDX_H // 2):
            qp = qb_scr[:, pair * LANE:(pair + 1) * LANE]
            for e in range(2):
                lg = _dot_nt(qp, kpad_scr[e, pl.ds(off, ch), :])
                acc = acc + jnp.tile(wb_scr[2 * pair + e], (1, ch // LANE)) * jnp.maximum(lg, 0.0)
        score_scr[:, pl.ds(off, ch)] = acc
        return carry

    lax.fori_loop(0, ((i + 1) * qb + ch - 1) // ch, chunk, 0)
    s_pos = lax.broadcasted_iota(jnp.int32, (1, t), 1)
    causal = s_pos <= tq
    key_scr[...] = _sortable(jnp.where(causal, score_scr[...], NEG))
    sel = _topk_mask(key_scr, top, s_pos, max(1, (t - 1).bit_length()))
    bias = jnp.where(sel & causal, 0.0, NEG)
    for c in range(t // LANE):
        mask_ref[c * LANE:(c + 1) * LANE, :] = bias[:, c * LANE:(c + 1) * LANE].T.astype(BF)


def _dsa_index_prompt(proj, sm, bsz, t):
    qb = LANE
    nq = t // qb
    top = min(IDX_TOPK, t // 4)
    kern = functools.partial(_dsa_index_prompt_kernel, qb=qb, t=t, top=top)
    wq = IDX_H * IDX_D
    return pl.pallas_call(
        kern, grid=(bsz, nq),
        in_specs=[pl.BlockSpec((qb, wq), lambda b, i: (b * nq + i, C_QI // wq)),
                  pl.BlockSpec((qb, LANE), lambda b, i: (b * nq + i, 0)),
                  pl.BlockSpec((t, LANE), lambda b, i: (b, 0))],
        out_specs=pl.BlockSpec((None, t, qb), lambda b, i: (b, 0, i)),
        out_shape=jax.ShapeDtypeStruct((bsz, t, t), BF),
        scratch_shapes=[pltpu.VMEM((qb, wq), BF), pltpu.VMEM((IDX_H, qb, LANE), F32),
                        pltpu.VMEM((2, t, 2 * IDX_D), BF), pltpu.VMEM((qb, t), F32), pltpu.VMEM((qb, t), jnp.int32)],
        compiler_params=_cp(("parallel", "arbitrary")), name="dsa_index_prompt")(proj, sm, sm)


def _dsa_index_sample_kernel(pt_ref, qf_ref, wcol_ref, *refs, t, pps, nsteps, t0, lpad, top):
    page_refs = refs[:pps]
    sn_ref, mask_ref, score_scr, key_scr = refs[pps:]
    j = pl.program_id(1)

    @pl.when(j == 0)
    def _():
        score_scr[...] = jnp.full(score_scr.shape, NEG, F32)

    qf = qf_ref[...].astype(BF)
    wcol = wcol_ref[...] * (IDX_H ** -0.5)

    def scores(logits):
        r = jnp.maximum(logits * (IDX_D ** -0.5), 0.0) * wcol
        sc = r[0:t]
        for h in range(1, IDX_H):
            sc = sc + r[h * t:(h + 1) * t]
        return sc

    kpages_t = jnp.concatenate([r[...] for r in page_refs], axis=1).astype(BF)
    score_scr[:, pl.ds(pl.multiple_of(j * (pps * PAGE), pps * PAGE), pps * PAGE)] = scores(_dot(qf, kpages_t))

    @pl.when(j == nsteps - 1)
    def _():
        tq = t0 + lax.broadcasted_iota(jnp.int32, (t, 1), 0)
        lane = lax.broadcasted_iota(jnp.int32, (1, PAGE), 1)
        kn = jnp.concatenate([sn_ref[:, 0:IDX_D], jnp.zeros((PAGE - t, IDX_D), F32)], axis=0).astype(BF)
        ok = (lane < t) & (t0 + lane <= tq)
        score_scr[:, lpad - PAGE:lpad] = jnp.where(ok, scores(_dot_nt(qf, kn)), NEG)
        s_pos = lax.broadcasted_iota(jnp.int32, (1, lpad), 1)
        key_scr[...] = _sortable(score_scr[...])
        sel = _topk_mask(key_scr, top, s_pos, max(1, (lpad - 1).bit_length()))
        mask_ref[...] = jnp.where(sel & (s_pos <= tq) & (s_pos < t0 + t), 0.0, NEG).astype(BF)


def _dsa_index_sample(qf, wcol, idx_t, table, sm, t, t0):
    bsz, nkb = table.shape
    pps = _pages_per_step(nkb, IDX_PPS)
    nsteps = nkb // pps
    lpad = (nkb + 1) * PAGE
    top = min(IDX_TOPK, (t0 + t) // 4)
    rows = IDX_H * t
    kern = functools.partial(_dsa_index_sample_kernel, t=t, pps=pps, nsteps=nsteps, t0=t0, lpad=lpad, top=top)
    gs = pltpu.PrefetchScalarGridSpec(
        num_scalar_prefetch=1, grid=(bsz, nsteps),
        in_specs=[pl.BlockSpec((rows, IDX_D), lambda b, j, pt: (b, 0)),
                  pl.BlockSpec((rows, 1), lambda b, j, pt: (b, 0))]
        + [pl.BlockSpec((IDX_D, PAGE), lambda b, j, pt, p=p: (pt[b, j * pps + p], 0)) for p in range(pps)]
        + [pl.BlockSpec((t, LANE), lambda b, j, pt: (b, 0))],
        out_specs=pl.BlockSpec((None, t, lpad), lambda b, j, pt: (b, 0, 0)),
        scratch_shapes=[pltpu.VMEM((t, lpad), F32), pltpu.VMEM((t, lpad), jnp.int32)])
    return pl.pallas_call(
        kern, grid_spec=gs, out_shape=jax.ShapeDtypeStruct((bsz, t, lpad), BF),
        compiler_params=_cp(("parallel", "arbitrary")), name="dsa_index_sample")(
            table, qf, wcol, *([idx_t] * pps), sm)


def _silu(z):
    return z * jax.nn.sigmoid(z)


def _combine_nsa_kernel(oc_ref, os_ref, ow_ref, gate_ref, z_ref, o_ref):
    gate = jax.nn.sigmoid(gate_ref[:, 0:3 * NSA_H])
    for h in range(NSA_H):
        sl = slice(h * HD, (h + 1) * HD)
        o = (gate[:, 3 * h:3 * h + 1] * oc_ref[:, sl] + gate[:, 3 * h + 1:3 * h + 2] * os_ref[:, sl]
             + gate[:, 3 * h + 2:3 * h + 3] * ow_ref[:, sl])
        o_ref[:, sl] = (o * _silu(z_ref[:, sl])).astype(BF)


def _combine_nsa(o_cmp, o_sel, o_win, proj, sm, tm):
    m, n = o_cmp.shape
    row = pl.BlockSpec((tm, n), lambda i: (i, 0))
    return pl.pallas_call(
        _combine_nsa_kernel, grid=(m // tm,),
        in_specs=[row, row, row, pl.BlockSpec((tm, LANE), lambda i: (i, 1)),
                  pl.BlockSpec((tm, n), lambda i: (i, C_ZN // n))],
        out_specs=row, out_shape=jax.ShapeDtypeStruct((m, n), BF),
        compiler_params=_cp(("parallel",)), name="combine_nsa")(o_cmp, o_sel, o_win, sm, proj)


def _combine_dsa_kernel(o_ref_in, z_ref, o_ref):
    o_ref[...] = (o_ref_in[...] * _silu(z_ref[...])).astype(BF)


def _combine_dsa(o, proj, tm):
    m, n = o.shape
    row = pl.BlockSpec((tm, n), lambda i: (i, 0))
    return pl.pallas_call(
        _combine_dsa_kernel, grid=(m // tm,),
        in_specs=[row, pl.BlockSpec((tm, n), lambda i: (i, C_ZD // n))],
        out_specs=row, out_shape=jax.ShapeDtypeStruct((m, n), BF),
        compiler_params=_cp(("parallel",)), name="combine_dsa")(o, proj)


PREP_TN = 512
PREP_TK = 1024


def _prep_tables():
    src, off = {}, 0
    for name, width in IN_SPLITS:
        src[name] = off
        off += width
    width = dict(IN_SPLITS)
    shifts, base, cls = [], [], []
    for name, _ in PROJ_LAYOUT:
        for c in range(width[name] // PREP_TN):
            s = src[name] + c * PREP_TN
            if s % LANE not in shifts:
                shifts.append(s % LANE)
            base.append(s // LANE)
            cls.append(shifts.index(s % LANE))
    return tuple(shifts), base, cls


def _prep_kernel(base_ref, cls_ref, *refs, shifts):
    o_ref = refs[-1]
    j = pl.program_id(1)
    for k, s in enumerate(shifts):
        @pl.when(cls_ref[j] == k)
        def _():
            win = jnp.concatenate([r[...] for r in refs[:-1]], axis=0)
            o_ref[...] = win[s:s + PREP_TN, :].T.astype(BF)


def _prep_w_in(w_in_t, l):
    shifts, base, cls = _prep_tables()
    assert all(sh % 8 == 0 for sh in shifts)
    k = w_in_t.shape[2]
    nwin = PREP_TN // LANE + 1
    kern = functools.partial(_prep_kernel, shifts=shifts)
    gs = pltpu.PrefetchScalarGridSpec(
        num_scalar_prefetch=2, grid=(k // PREP_TK, len(base)),
        in_specs=[pl.BlockSpec((None, LANE, PREP_TK), lambda i, j, bs, cs, m=m: (l, bs[j] + m, i))
                  for m in range(nwin)],
        out_specs=pl.BlockSpec((PREP_TK, PREP_TN), lambda i, j, bs, cs: (i, j)))
    return pl.pallas_call(
        kern, grid_spec=gs, out_shape=jax.ShapeDtypeStruct((k, PROJ_W), BF),
        compiler_params=_cp(("parallel", "parallel")), name="prep_w_in")(
            jnp.asarray(base, jnp.int32), jnp.asarray(cls, jnp.int32), *([w_in_t] * nwin))


def _prep_small_kernel(g_ref, a_ref, b_ref, o_ref, *, gate_off, w_off, k_off):
    tk = o_ref.shape[0]
    ab = jnp.concatenate([a_ref[...], b_ref[...]], axis=0)
    rows = jnp.concatenate([ab[k_off:k_off + IDX_D], ab[w_off:w_off + IDX_H], jnp.zeros((32, tk), F32),
                            g_ref[gate_off:gate_off + 3 * NSA_H], jnp.zeros((SM_W - 176, tk), F32)], axis=0)
    o_ref[...] = rows.T.astype(BF)


def _prep_w_small(w_in_t, l):
    src, off = {}, 0
    for name, width in IN_SPLITS:
        src[name] = off
        off += width
    gate_blk, ab_blk = src['nsa_gate'] // LANE, src['idx_w'] // LANE
    offs = dict(gate_off=src['nsa_gate'] - gate_blk * LANE, w_off=src['idx_w'] - ab_blk * LANE,
                k_off=src['idx_k'] - ab_blk * LANE)
    assert all(v % 8 == 0 for v in offs.values()) and offs['k_off'] + IDX_D <= 2 * LANE
    k = w_in_t.shape[2]
    kern = functools.partial(_prep_small_kernel, **offs)
    return pl.pallas_call(
        kern, grid=(k // PREP_TK,),
        in_specs=[pl.BlockSpec((None, LANE, PREP_TK), lambda i, blk=blk: (l, blk, i))
                  for blk in (gate_blk, ab_blk, ab_blk + 1)],
        out_specs=pl.BlockSpec((PREP_TK, SM_W), lambda i: (i, 0)),
        out_shape=jax.ShapeDtypeStruct((k, SM_W), BF),
        compiler_params=_cp(("parallel",)), name="prep_w_small")(w_in_t, w_in_t, w_in_t)


def _slopes(n):
    return jnp.exp2(-8.0 * jnp.arange(1, n + 1, dtype=F32) / n)


def _tail(proj, xf, wts, a_in, b_in, tm_mm, tm_ln):
    hm = _gated_matmul(a_in, b_in, wts['pa'], wts['pd'], proj, tm_mm, 512, wts['l'])
    h = _matmul(hm, wts['out'], tm_mm, 512, l=wts['l'])
    return _residual_ln(xf, h, wts['gain'], wts['bias'], tm_ln)


def _layer_prompt(xf, xb, bsz, t, wts):
    tm = min(1024, bsz * t)
    proj = _matmul(xb, wts['in'], tm, 1024)
    sm = _matmul(xb, wts['in_small'], tm, SM_W)
    n_pages = t // PAGE
    table = jnp.arange(bsz * n_pages, dtype=jnp.int32).reshape(bsz, n_pages)
    kcvc = _compress(proj, table, C_KVN // (4 * HD), 0, wts['pe_page'], wts['w1'], wts['w2'], n_pages, 1)
    sl_n, sl_d = _slopes(NSA_H), _slopes(DSA_H)
    o_cmp, selm = _nsa_select(proj, kcvc, sl_n, bsz, t, t // CMP, 0, 0)
    o_sel = _fa_prompt("sel", proj, sl_n, bsz, t, C_QN, C_KVN + KV_W, C_KVN + KV_W + GROUPS * HD, selm)
    o_win = _fa_prompt("win", proj, sl_n, bsz, t, C_QN, C_KVN + 2 * KV_W, C_KVN + 2 * KV_W + GROUPS * HD)
    dmask = _dsa_index_prompt(proj, sm, bsz, t)
    o_dsa = _fa_prompt("dsa", proj, sl_d, bsz, t, C_QD, C_KVD, C_KVD + GROUPS * HD, dmask)
    tme = min(256, bsz * t)
    a_in = _combine_nsa(o_cmp, o_sel, o_win, proj, sm, tme)
    b_in = _combine_dsa(o_dsa, proj, tme)
    y, yb = _tail(proj, xf, wts, a_in, b_in, tm, tme)
    return y, yb, proj, sm


def _layer_sample(xf, xb, bsz, t, wts, nsa_rows, win_rows, dsa_rows, idx_t, page_table, win_table):
    rows = bsz * t
    proj = _matmul(xb, wts['in'], rows, 512)
    sm = _matmul(xb, wts['in_small'], rows, SM_W)
    n_pages = page_table.shape[1]
    t0 = n_pages * PAGE
    wbuf = win_table.shape[1] * PAGE
    kcvc = _compress(nsa_rows, page_table, 0, 4 * GROUPS, wts['pe_page'], wts['w1'], wts['w2'], min(64, n_pages),
                     _pages_per_step(min(64, n_pages), FA_PPS))
    sl_n, sl_d = _slopes(NSA_H), _slopes(DSA_H)
    o_cmp, selm = _nsa_select(proj, kcvc, sl_n, bsz, t, -(-(t0 + t) // CMP), t0, n_pages)
    o_sel = _fa_sample("sel", proj, nsa_rows, page_table, sl_n, t, t0, 0, C_QN, 4 * GROUPS, 2, C_KVN + KV_W, selm)
    o_win = _fa_sample("win", proj, win_rows, win_table, sl_n, t, t0, t0 - wbuf, C_QN, 2 * GROUPS, 0,
                       C_KVN + 2 * KV_W)
    qi = proj[:, C_QI:C_QI + IDX_H * IDX_D].reshape(bsz, t, IDX_H, IDX_D).transpose(0, 2, 1, 3)
    qf = qi.reshape(bsz * IDX_H * t, IDX_D)
    wi = sm[:, SM_IDXW:SM_IDXW + IDX_H].reshape(bsz, t, IDX_H).transpose(0, 2, 1)
    wcol = wi.reshape(bsz * IDX_H * t, 1)
    dmask = _dsa_index_sample(qf, wcol, idx_t, page_table, sm, t, t0)
    o_dsa = _fa_sample("dsa", proj, dsa_rows, page_table, sl_d, t, t0, 0, C_QD, 2 * GROUPS, 0, C_KVD, dmask)
    a_in = _combine_nsa(o_cmp, o_sel, o_win, proj, sm, rows)
    b_in = _combine_dsa(o_dsa, proj, rows)
    y, yb = _tail(proj, xf, wts, a_in, b_in, rows, rows)
    return y, yb, proj, sm


def _layer_weights(l, w_in, cmp_pe, cmp_w1, cmp_w2, w_proj_nsa, w_proj_dsa, w_out, ln_gain, ln_bias):
    cmp_pe, cmp_w1, cmp_w2, ln_gain, ln_bias = (a[l] for a in (cmp_pe, cmp_w1, cmp_w2, ln_gain, ln_bias))
    pe = jnp.concatenate([cmp_pe, cmp_pe], axis=1)
    pe_page = jnp.concatenate([pe[0], pe[0], pe[1], pe[1]], axis=1)
    w_in_t = jnp.swapaxes(w_in, 1, 2)
    return {'in': _prep_w_in(w_in_t, l), 'in_small': _prep_w_small(w_in_t, l), 'pe_page': pe_page,
            'w1': cmp_w1.astype(BF), 'w2': cmp_w2.astype(BF),
            'pa': w_proj_nsa.astype(BF), 'pd': w_proj_dsa.astype(BF), 'out': w_out.astype(BF),
            'l': l, 'gain': ln_gain, 'bias': ln_bias}


STATE_TM = 256


def _state_kernel(x_ref, *refs, tm):
    nsa_ref, dsa_ref = refs[-2:]
    for cg in range(4 * GROUPS):
        nsa_ref[pl.ds(cg, tm, stride=4 * GROUPS), :] = x_ref[:, cg * HD:(cg + 1) * HD]
    for cg in range(2 * GROUPS):
        dsa_ref[pl.ds(cg, tm, stride=2 * GROUPS), :] = x_ref[:, 3 * KV_W + cg * HD:3 * KV_W + (cg + 1) * HD]


def _write_state(proj, l, prev):
    m = proj.shape[0]
    tm = min(STATE_TM, m)
    nt = m // tm
    kern = functools.partial(_state_kernel, tm=tm)
    width = 4 * KV_W
    in_specs = [pl.BlockSpec((tm, width), lambda i: (i, C_KVN // width))]
    args = [proj]
    aliases = {}
    if prev is not None:
        in_specs += [pl.BlockSpec(memory_space=pl.ANY), pl.BlockSpec(memory_space=pl.ANY)]
        args += list(prev)
        aliases = {1: 0, 2: 1}
    return pl.pallas_call(
        kern, grid=(nt,), in_specs=in_specs,
        out_specs=[pl.BlockSpec((tm * 4 * GROUPS, HD), lambda i: (l * nt + i, 0)),
                   pl.BlockSpec((tm * 2 * GROUPS, HD), lambda i: (l * nt + i, 0))],
        out_shape=[jax.ShapeDtypeStruct((DEPTH * m * 4 * GROUPS, HD), F32),
                   jax.ShapeDtypeStruct((DEPTH * m * 2 * GROUPS, HD), F32)],
        input_output_aliases=aliases,
        compiler_params=_cp(("arbitrary",)), name="write_state")(*args)


def _win_and_idx(proj, sm, bsz, t):
    win = proj[:, C_KVN + 2 * KV_W:C_KVN + 3 * KV_W].reshape(bsz, t, 2, GROUPS, HD)
    idx_k = sm[:, 0:IDX_D].reshape(bsz, t, IDX_D)
    return win, idx_k


def kernel(x_prompt, x_sample, cache_nsa_kv, state_nsa_win, cache_dsa_kv, cache_dsa_idx, page_table,
           w_in, cmp_pe, cmp_w1, cmp_w2, w_proj_nsa, w_proj_dsa, w_out, ln_gain, ln_bias):
    bp, tp, _ = x_prompt.shape
    bs, ts, _ = x_sample.shape
    n_pool = cache_nsa_kv.shape[1]
    wpages = state_nsa_win.shape[2] // PAGE
    nsa_rows = cache_nsa_kv.reshape(-1, HD)
    dsa_rows = cache_dsa_kv.reshape(-1, HD)
    idx_t = jnp.swapaxes(cache_dsa_idx, 2, 3).reshape(-1, PAGE)
    win_rows = state_nsa_win.reshape(-1, HD)
    win_table = jnp.arange(bs * wpages, dtype=jnp.int32).reshape(bs, wpages)
    yp, ys = x_prompt.reshape(bp * tp, D_MODEL), x_sample.reshape(bs * ts, D_MODEL)
    ypb, ysb = yp.astype(BF), ys.astype(BF)
    outs = [[] for _ in range(4)]
    state_p = state_s = None
    for l in range(DEPTH):
        wts = _layer_weights(l, w_in, cmp_pe, cmp_w1, cmp_w2, w_proj_nsa, w_proj_dsa, w_out, ln_gain, ln_bias)
        yp, ypb, proj_p, sm_p = _layer_prompt(yp, ypb, bp, tp, wts)
        ys, ysb, proj_s, sm_s = _layer_sample(ys, ysb, bs, ts, wts, nsa_rows, win_rows, dsa_rows, idx_t,
                                              page_table + l * n_pool, win_table + l * bs * wpages)
        state_p = _write_state(proj_p, l, state_p)
        state_s = _write_state(proj_s, l, state_s)
        win_p, idx_p = _win_and_idx(proj_p, sm_p, bp, tp)
        win_s, idx_s = _win_and_idx(proj_s, sm_s, bs, ts)
        win_all = jnp.concatenate([state_nsa_win[l], win_s], axis=1)
        keep_p, keep_s = min(WINDOW, tp), min(WINDOW, win_all.shape[1])
        for lst, val in zip(outs, (win_p[:, tp - keep_p:], win_all[:, win_all.shape[1] - keep_s:], idx_p, idx_s)):
            lst.append(val)
    win_po, win_so, idx_po, idx_so = (jnp.stack(o) for o in outs)
    return (yp.reshape(bp, tp, D_MODEL), ys.reshape(bs, ts, D_MODEL),
            state_p[0].reshape(DEPTH, bp, tp, 4, GROUPS, HD), state_s[0].reshape(DEPTH, bs, ts, 4, GROUPS, HD),
            win_po, win_so,
            state_p[1].reshape(DEPTH, bp, tp, 2, GROUPS, HD), state_s[1].reshape(DEPTH, bs, ts, 2, GROUPS, HD),
            idx_po, idx_so)
```

```python
import functools

import jax
import jax.numpy as jnp
from jax import lax
from jax.experimental import pallas as pl
from jax.experimental.pallas import tpu as pltpu

D_MODEL = 4096
DEPTH = 2
PAGE = 128
HD = 128
NSA_H = 16
DSA_H = 16
GROUPS = 2
RH = NSA_H // GROUPS
CMP = 64
CMP_SHIFT = 6
N_SEL = 16
WINDOW = 512
IDX_H = 32
IDX_D = 64
IDX_TOPK = 256
LN_EPS = 1e-5
ALPHA = (2 * DEPTH) ** 0.25
NEG = -1e30
SEL_FORCE = 1e4
SCALE = HD ** -0.5
LOG2E = 1.4426950408889634
INT_MIN = -(2 ** 31)

IN_SPLITS = (('nsa_q', 2048), ('nsa_kv', 1536), ('nsa_gate', 48), ('nsa_z', 2048), ('dsa_q', 2048),
             ('dsa_kv', 512), ('idx_q', 2048), ('idx_w', 32), ('idx_k', 64), ('dsa_z', 2048), ('merge', 8192))

C_QN, C_ZN, C_QD, C_ZD, C_QI, C_MG, C_KVN, C_KVD = 0, 2048, 4096, 6144, 8192, 10240, 18432, 19968
PROJ_W = 20480
PROJ_LAYOUT = (('nsa_q', C_QN), ('nsa_z', C_ZN), ('dsa_q', C_QD), ('dsa_z', C_ZD), ('idx_q', C_QI),
               ('merge', C_MG), ('nsa_kv', C_KVN), ('dsa_kv', C_KVD))
SM_W = 256
SM_IDXW = 64
LANE = 128
KV_W = 2 * GROUPS * HD

VMEM_LIMIT = 48 * 1024 * 1024
BF = jnp.bfloat16
F32 = jnp.float32


def _cp(sem):
    return pltpu.CompilerParams(dimension_semantics=sem, vmem_limit_bytes=VMEM_LIMIT)


def _dot_nt(a, b):
    return lax.dot_general(a, b, (((1,), (1,)), ((), ())), preferred_element_type=F32)


def _dot(a, b):
    return jnp.dot(a, b, preferred_element_type=F32)


def _mm_kernel(x_ref, w_ref, o_ref):
    o_ref[...] = _dot(x_ref[...], w_ref[...]).astype(o_ref.dtype)


def _w_spec(w, l, tn):
    k = w.shape[-2]
    if w.ndim == 3:
        return pl.BlockSpec((None, k, tn), lambda i, j: (l, 0, j))
    return pl.BlockSpec((k, tn), lambda i, j: (0, j))


def _matmul(x, w, tm, tn, out_dtype=F32, l=0):
    m, k = x.shape
    n = w.shape[-1]
    return pl.pallas_call(
        _mm_kernel, grid=(m // tm, n // tn),
        in_specs=[pl.BlockSpec((tm, k), lambda i, j: (i, 0)), _w_spec(w, l, tn)],
        out_specs=pl.BlockSpec((tm, tn), lambda i, j: (i, j)),
        out_shape=jax.ShapeDtypeStruct((m, n), out_dtype),
        compiler_params=_cp(("parallel", "parallel")), name="matmul")(x, w)


def _gated_mm_kernel(a_ref, b_ref, wa_ref, wb_ref, m0_ref, m1_ref, o_ref):
    a = _dot(a_ref[...], wa_ref[...])
    b = _dot(b_ref[...], wb_ref[...])
    o_ref[...] = (jax.nn.sigmoid(m0_ref[...]) * a + jax.nn.sigmoid(m1_ref[...]) * b).astype(o_ref.dtype)


def _gated_matmul(a_in, b_in, wa, wb, proj, tm, tn, l=0):
    m, k = a_in.shape
    n = wa.shape[-1]
    c0, c1 = C_MG // tn, (C_MG + D_MODEL) // tn
    return pl.pallas_call(
        _gated_mm_kernel, grid=(m // tm, n // tn),
        in_specs=[pl.BlockSpec((tm, k), lambda i, j: (i, 0)), pl.BlockSpec((tm, k), lambda i, j: (i, 0)),
                  _w_spec(wa, l, tn), _w_spec(wb, l, tn),
                  pl.BlockSpec((tm, tn), lambda i, j: (i, c0 + j)), pl.BlockSpec((tm, tn), lambda i, j: (i, c1 + j))],
        out_specs=pl.BlockSpec((tm, tn), lambda i, j: (i, j)),
        out_shape=jax.ShapeDtypeStruct((m, n), BF),
        compiler_params=_cp(("parallel", "parallel")), name="gated_matmul")(a_in, b_in, wa, wb, proj, proj)


def _ln_kernel(x_ref, h_ref, g_ref, b_ref, y_ref, yb_ref):
    v = ALPHA * x_ref[...] + h_ref[...]
    mu = jnp.mean(v, axis=-1, keepdims=True)
    c = v - mu
    var = jnp.mean(c * c, axis=-1, keepdims=True)
    y = c * lax.rsqrt(var + LN_EPS) * g_ref[...] + b_ref[...]
    y_ref[...] = y
    yb_ref[...] = y.astype(BF)


def _residual_ln(x, h, gain, bias, tm):
    m, n = x.shape
    row = pl.BlockSpec((tm, n), lambda i: (i, 0))
    vec = pl.BlockSpec((1, n), lambda i: (0, 0))
    return pl.pallas_call(
        _ln_kernel, grid=(m // tm,), in_specs=[row, row, vec, vec], out_specs=[row, row],
        out_shape=[jax.ShapeDtypeStruct((m, n), F32), jax.ShapeDtypeStruct((m, n), BF)],
        compiler_params=_cp(("parallel",)), name="residual_ln")(x, h, gain.reshape(1, n), bias.reshape(1, n))


def _gelu_tanh(x):
    return 0.5 * x * (1.0 + jnp.tanh(0.7978845608028654 * (x + 0.044715 * (x * x * x))))


CMP_TOKC = 8


def _compress_kernel(pt_ref, *refs, pg, pps, rpt):
    page_refs = refs[:pps]
    pe_ref, w1_ref, w2_ref, o_ref, slab_ref = refs[pps:]
    p = pl.program_id(2)
    for k, x_ref in enumerate(page_refs):
        for cg in range(2 * GROUPS):
            x = x_ref[pl.ds(cg, PAGE, stride=rpt), :] if rpt else x_ref[:, cg * HD:(cg + 1) * HD]
            slab_ref[cg, pl.ds(pl.multiple_of((p * pps + k) * PAGE, PAGE), PAGE), :] = (
                x + pe_ref[:, cg * HD:(cg + 1) * HD])

    @pl.when(p == pg // pps - 1)
    def _():
        nblk = pg * (PAGE // CMP)
        for c in range(2):
            acc = jnp.zeros((GROUPS * nblk, HD), F32)
            for tok0 in range(0, CMP, CMP_TOKC):
                lhs = jnp.concatenate(
                    [jnp.concatenate([slab_ref[c * GROUPS + g, pl.ds(tok, nblk, stride=CMP), :]
                                      for g in range(GROUPS)], axis=0).astype(BF)
                     for tok in range(tok0, tok0 + CMP_TOKC)], axis=1)
                acc = acc + _dot(lhs, w1_ref[c, tok0 * HD:(tok0 + CMP_TOKC) * HD, :])
            out = _dot(_gelu_tanh(acc).astype(BF), w2_ref[c])
            for g in range(GROUPS):
                o_ref[c, g] = out[g * nblk:(g + 1) * nblk]


def _compress(src, table, col_blk, rpt, pe_page, w1, w2, pg, pps):
    b, n_pages = table.shape
    ns = n_pages // pg
    nblk = pg * (PAGE // CMP)
    kern = functools.partial(_compress_kernel, pg=pg, pps=pps, rpt=rpt)
    if rpt:
        page_specs = [pl.BlockSpec((PAGE * rpt, HD), lambda bi, s, p, pt, k=k: (pt[bi, s * pg + p * pps + k], 0))
                      for k in range(pps)]
    else:
        page_specs = [pl.BlockSpec((PAGE, 4 * HD), lambda bi, s, p, pt, k=k: (pt[bi, s * pg + p * pps + k], col_blk))
                      for k in range(pps)]
    gs = pltpu.PrefetchScalarGridSpec(
        num_scalar_prefetch=1, grid=(b, ns, pg // pps),
        in_specs=page_specs + [pl.BlockSpec((PAGE, 4 * HD), lambda bi, s, p, pt: (0, 0)),
                               pl.BlockSpec((2, CMP * HD, HD), lambda bi, s, p, pt: (0, 0, 0)),
                               pl.BlockSpec((2, HD, HD), lambda bi, s, p, pt: (0, 0, 0))],
        out_specs=pl.BlockSpec((None, 2, GROUPS, nblk, HD), lambda bi, s, p, pt: (bi, 0, 0, s, 0)),
        scratch_shapes=[pltpu.VMEM((2 * GROUPS, pg * PAGE, HD), F32)])
    return pl.pallas_call(
        kern, grid_spec=gs, out_shape=jax.ShapeDtypeStruct((b, 2, GROUPS, ns * nblk, HD), F32),
        compiler_params=_cp(("parallel", "arbitrary", "arbitrary")), name="compress")(
            table, *([src] * pps), pe_page, w1, w2)


def _sortable(x):
    b = lax.bitcast_convert_type(x, jnp.int32)
    return jnp.where(b < 0, b ^ jnp.int32(0x7FFFFFFF), b)


def _count(pred):
    return jnp.sum(jnp.where(pred, 1.0, 0.0), axis=-1, keepdims=True)


def _topk_mask(key_ref, k, idx, idx_bits):
    rows = key_ref.shape[0]
    kf = float(k)
    zero = jnp.zeros((rows, 1), jnp.int32)
    t0 = jnp.where(_count(key_ref[...] >= zero) >= kf, zero, jnp.full((rows, 1), INT_MIN, jnp.int32))

    def value_bit(i, t):
        cand = t | jnp.left_shift(jnp.int32(1), jnp.int32(30) - i)
        return jnp.where(_count(key_ref[...] >= cand) >= kf, cand, t)

    t = lax.fori_loop(0, 31, value_bit, t0)
    keys = key_ref[...]
    surplus = jnp.max(_count(keys >= t)) > kf

    def tie_cut(_):
        need = kf - _count(key_ref[...] > t)

        def index_bit(i, c):
            cand = c | jnp.left_shift(jnp.int32(1), jnp.int32(idx_bits - 1) - i)
            below = _count((key_ref[...] == t) & (idx < cand))
            return jnp.where(below < need, cand, c)

        return lax.fori_loop(0, idx_bits, index_bit, zero)

    c0 = lax.cond(surplus, tie_cut, lambda _: jnp.full((rows, 1), 2 ** idx_bits - 1, jnp.int32), None)
    return (keys > t) | ((keys == t) & (idx <= c0))


def _masked_softmax(s, mask):
    s = jnp.where(mask, s, NEG)
    e = jnp.where(mask, jnp.exp(s - jnp.max(s, axis=-1, keepdims=True)), 0.0)
    return e / jnp.maximum(jnp.sum(e, axis=-1, keepdims=True), 1e-30)


def _nsa_select_kernel(slopes_ref, q_ref, kc_ref, vc_ref, ocmp_ref, sel_ref, key_scr,
                       *, qb, nb, nbs, nbs_pad, t0, bps):
    g = pl.program_id(1)
    i = pl.program_id(2)
    tq = t0 + i * qb + lax.broadcasted_iota(jnp.int32, (qb, 1), 0)
    blk_end = lax.broadcasted_iota(jnp.int32, (1, nb), 1) * CMP + (CMP - 1)
    valid = blk_end <= tq
    distf = (tq - blk_end).astype(F32)
    kc = kc_ref[...].astype(BF)
    vc = vc_ref[...].astype(BF)
    qs = jnp.concatenate([q_ref[:, r * HD:(r + 1) * HD] for r in range(RH)], axis=0).astype(BF)
    slope = jnp.concatenate([jnp.full((qb, 1), slopes_ref[g * RH + r], F32) for r in range(RH)], axis=0)
    s = _dot_nt(qs, kc) * SCALE - slope * jnp.tile(distf, (RH, 1))
    p = _masked_softmax(s, jnp.tile(valid, (RH, 1)))
    o = _dot(p.astype(BF), vc)
    imp = jnp.zeros((qb, nb), F32)
    for r in range(RH):
        ocmp_ref[:, r * HD:(r + 1) * HD] = o[r * qb:(r + 1) * qb]
        imp = imp + p[r * qb:(r + 1) * qb]
    if nbs_pad > nb:
        imp = jnp.concatenate([imp, jnp.zeros((qb, nbs_pad - nb), F32)], axis=1)
    j = lax.broadcasted_iota(jnp.int32, (1, nbs_pad), 1)
    cur = jnp.right_shift(tq, CMP_SHIFT)
    forced = (j == 0) | (j == cur) | (j == cur - 1)
    score = jnp.where(j <= cur, jnp.where(forced, SEL_FORCE, imp), -SEL_FORCE)
    key_scr[...] = _sortable(score)
    sel = _topk_mask(key_scr, min(N_SEL, nbs), j, max(1, (nbs_pad - 1).bit_length()))
    sel_f = jnp.where(sel, 1.0, 0.0)
    for s_i in range(sel_ref.shape[0]):
        sel_ref[s_i] = jnp.concatenate(
            [sel_f[:, s_i * bps:(s_i + 1) * bps], jnp.zeros((qb, LANE - bps), F32)], axis=1)


def _nsa_select_t_kernel(slopes_ref, q_ref, kc_ref, vc_ref, ocmp_ref, sel_ref, *, qb, nb, n_sel, t0):
    g = pl.program_id(1)
    i = pl.program_id(2)
    n_col = lax.broadcasted_iota(jnp.int32, (nb, qb), 0)
    tq = t0 + i * qb + lax.broadcasted_iota(jnp.int32, (nb, qb), 1)
    blk_end = n_col * CMP + (CMP - 1)
    valid = blk_end <= tq
    distf = (tq - blk_end).astype(F32)
    kc = kc_ref[...].astype(BF)
    vct = vc_ref[...].T.astype(BF)
    qs = jnp.concatenate([q_ref[:, r * HD:(r + 1) * HD] for r in range(RH)], axis=0).astype(BF)
    raw = _dot_nt(kc, qs)
    imp = jnp.zeros((nb, qb), F32)
    probs = []
    for r in range(RH):
        s = jnp.where(valid, raw[:, r * qb:(r + 1) * qb] * SCALE - slopes_ref[g * RH + r] * distf, NEG)
        e = jnp.where(valid, jnp.exp(s - jnp.max(s, axis=0, keepdims=True)), 0.0)
        p = e / jnp.maximum(jnp.sum(e, axis=0, keepdims=True), 1e-30)
        probs.append(p.astype(BF))
        imp = imp + p
    o_t = _dot(vct, jnp.concatenate(probs, axis=1))
    for r in range(RH):
        ocmp_ref[:, r * HD:(r + 1) * HD] = o_t[:, r * qb:(r + 1) * qb].T
    cur = jnp.right_shift(tq, CMP_SHIFT)
    forced = (n_col == 0) | (n_col == cur) | (n_col == cur - 1)
    score = jnp.where(n_col <= cur, jnp.where(forced, SEL_FORCE, imp), -SEL_FORCE)
    rank = jnp.zeros((nb, qb), F32)
    for a in range(nb):
        row = score[a:a + 1, :]
        rank = rank + jnp.where((row > score) | ((row == score) & (n_col > a)), 1.0, 0.0)
    bias_t = jnp.where(rank < float(n_sel), 0.0, NEG)
    for n in range(nb):
        sel_ref[n] = jnp.broadcast_to(bias_t[n:n + 1, :], (8, qb))


def _nsa_select(proj, kcvc, slopes, bsz, t, nbs, t0, past_pages):
    qb = min(t, LANE)
    nq = t // qb
    nb = kcvc.shape[3]
    nbs_pad = nbs if nbs == nb else -(-nbs // LANE) * LANE
    if past_pages:
        bps = _fa_pps(past_pages) * (PAGE // CMP)
        n_grp = past_pages * (PAGE // CMP) // bps + 1
        assert n_grp * bps <= nbs_pad and nq == 1
        kern = functools.partial(_nsa_select_kernel, qb=qb, nb=nb, nbs=nbs, nbs_pad=nbs_pad, t0=t0, bps=bps)
        scratch = [pltpu.VMEM((qb, nbs_pad), jnp.int32)]
    else:
        assert nbs == nb
        kern = functools.partial(_nsa_select_t_kernel, qb=qb, nb=nb, n_sel=min(N_SEL, nbs), t0=t0)
        scratch = []
    wq = RH * HD
    if past_pages:
        sel_spec = pl.BlockSpec((None, None, n_grp, qb, LANE), lambda b, g, i: (b, g, 0, 0, 0))
        sel_shape = jax.ShapeDtypeStruct((bsz, GROUPS, n_grp, t, LANE), F32)
    else:
        sel_spec = pl.BlockSpec((None, None, nbs, 8, qb), lambda b, g, i: (b, g, 0, 0, i))
        sel_shape = jax.ShapeDtypeStruct((bsz, GROUPS, nbs, 8, t), F32)
    return pl.pallas_call(
        kern, grid=(bsz, GROUPS, nq),
        in_specs=[pl.BlockSpec(memory_space=pltpu.SMEM),
                  pl.BlockSpec((qb, wq), lambda b, g, i: (b * nq + i, C_QN // wq + g)),
                  pl.BlockSpec((None, None, None, nb, HD), lambda b, g, i: (b, 0, g, 0, 0)),
                  pl.BlockSpec((None, None, None, nb, HD), lambda b, g, i: (b, 1, g, 0, 0))],
        out_specs=[pl.BlockSpec((qb, wq), lambda b, g, i: (b * nq + i, g)), sel_spec],
        out_shape=[jax.ShapeDtypeStruct((bsz * t, NSA_H * HD), F32), sel_shape],
        scratch_shapes=scratch,
        compiler_params=_cp(("parallel", "parallel", "parallel")), name="nsa_select")(slopes, proj, kcvc, kcvc)


def _fa_prompt_kernel(*refs, mode, qb, kb):
    if mode == "win":
        slopes_ref, q_ref, k_ref, v_ref, o_ref = refs[:5]
        mask_ref = None
    else:
        slopes_ref, q_ref, k_ref, v_ref, mask_ref, o_ref = refs[:6]
    qs_scr, bias_scr, s_scr, p_scr, mb_scr, m_scr, a_scr, acc_scr = refs[-8:]
    g = pl.program_id(1)
    i = pl.program_id(2)
    k_local = lax.broadcasted_iota(jnp.int32, (kb, qb), 0)
    q_local = lax.broadcasted_iota(jnp.int32, (kb, qb), 1)
    for r in range(RH):
        qs_scr[r * qb:(r + 1) * qb, :] = (q_ref[:, r * HD:(r + 1) * HD] * (SCALE * LOG2E)).astype(BF)

    @pl.when(i == 0)
    def _():
        for r in range(RH):
            bias_scr[:, r * qb:(r + 1) * qb] = (slopes_ref[g * RH + r] * LOG2E) * k_local.astype(F32)

    m_scr[...] = jnp.full(m_scr.shape, NEG, F32)
    acc_scr[...] = jnp.zeros(acc_scr.shape, F32)
    j_hi = ((i + 1) * qb - 1) // kb
    j_lo = jnp.maximum(i * qb - (WINDOW - 1), 0) // kb if mode == "win" else 0

    def body(j, carry):
        off = pl.multiple_of(j * kb, kb)
        s_scr[...] = _dot_nt(k_ref[pl.ds(off, kb), :].astype(BF), qs_scr[...])
        dist = (i * qb + q_local) - (j * kb + k_local)
        ok = dist >= 0
        if mode == "win":
            ok = ok & (dist < WINDOW)
        mb = jnp.where(ok, 0.0, NEG)
        if mode == "sel":
            tiles = mask_ref[pl.ds(j * (kb // CMP), kb // CMP)]
            mb = mb + jnp.concatenate([jnp.tile(tiles[n], (CMP // 8, 1)) for n in range(kb // CMP)], axis=0)
        elif mode == "dsa":
            mb = mb + mask_ref[pl.ds(off, kb), :].astype(F32)
        mb_scr[...] = mb
        cbase = (j * kb - i * qb).astype(F32)
        for ch in range(RH * qb // LANE):
            r, h = divmod(ch, qb // LANE)
            sl = slice(ch * LANE, (ch + 1) * LANE)
            c = (slopes_ref[g * RH + r] * LOG2E) * cbase
            x = s_scr[:, sl] + bias_scr[:, sl] + mb_scr[:, h * LANE:(h + 1) * LANE]
            m_prev = m_scr[:, sl]
            m_new = jnp.maximum(m_prev, jnp.max(x, axis=0, keepdims=True) + c)
            p_scr[:, sl] = jnp.exp2(x - (m_new - c)).astype(BF)
            a_scr[:, sl] = jnp.exp2(m_prev - m_new)
            m_scr[:, sl] = m_new
        vt = jnp.concatenate([v_ref[pl.ds(off, kb), :].T, jnp.ones((16, kb), F32)], axis=0).astype(BF)
        acc_scr[...] = acc_scr[...] * a_scr[...] + _dot(vt, p_scr[...])
        return carry

    lax.fori_loop(j_lo, j_hi + 1, body, 0)
    for r in range(RH):
        sl = slice(r * qb, (r + 1) * qb)
        o_t = jnp.where(m_scr[:, sl] > 0.5 * NEG,
                        acc_scr[0:HD, sl] / jnp.maximum(acc_scr[HD:HD + 1, sl], 1e-30), 0.0)
        o_ref[:, r * HD:(r + 1) * HD] = o_t.T


def _fa_prompt(mode, proj, slopes, bsz, t, q_col, k_col, v_col, mask=None):
    qb = min(2 * LANE, t)
    kb = min(2 * LANE, t)
    nq = t // qb
    wq = RH * HD
    in_specs = [pl.BlockSpec(memory_space=pltpu.SMEM),
                pl.BlockSpec((qb, wq), lambda b, g, i: (b * nq + i, q_col // wq + g)),
                pl.BlockSpec((t, HD), lambda b, g, i: (b, k_col // HD + g)),
                pl.BlockSpec((t, HD), lambda b, g, i: (b, v_col // HD + g))]
    args = [slopes, proj, proj, proj]
    if mode == "sel":
        in_specs.append(pl.BlockSpec((None, None, t // CMP, 8, qb), lambda b, g, i: (b, g, 0, 0, i)))
        args.append(mask)
    elif mode == "dsa":
        in_specs.append(pl.BlockSpec((None, t, qb), lambda b, g, i: (b, 0, i)))
        args.append(mask)
    kern = functools.partial(_fa_prompt_kernel, mode=mode, qb=qb, kb=kb)
    lanes = RH * qb
    return pl.pallas_call(
        kern, grid=(bsz, GROUPS, nq), in_specs=in_specs,
        out_specs=pl.BlockSpec((qb, wq), lambda b, g, i: (b * nq + i, g)),
        out_shape=jax.ShapeDtypeStruct((bsz * t, GROUPS * wq), F32),
        scratch_shapes=[pltpu.VMEM((lanes, HD), BF), pltpu.VMEM((kb, lanes), F32), pltpu.VMEM((kb, lanes), F32),
                        pltpu.VMEM((kb, lanes), BF), pltpu.VMEM((kb, qb), F32), pltpu.VMEM((1, lanes), F32),
                        pltpu.VMEM((1, lanes), F32), pltpu.VMEM((HD + 16, lanes), F32)],
        compiler_params=_cp(("parallel", "parallel", "arbitrary")), name="fa_prompt_" + mode)(*args)


PPS = 4
FA_PPS = 16
IDX_PPS = 16


def _pages_per_step(n_pages, want):
    return want if n_pages % want == 0 else PPS


def _fa_sample_kernel(*refs, mode, t, pps, nsteps, t0, kpos0, rpt, kcomp):
    pt_ref, slope_ref, q_ref = refs[:3]
    page_refs = refs[3:3 + pps]
    new_ref = refs[3 + pps]
    rest = refs[4 + pps:]
    if mode == "win":
        mask_ref = maskn_ref = None
        o_ref, qs_scr, m_scr, l_scr, acc_scr = rest
    else:
        mask_ref, maskn_ref, o_ref, qs_scr, m_scr, l_scr, acc_scr = rest
    j = pl.program_id(1)
    bps = pps * (PAGE // CMP)

    @pl.when(j == 0)
    def _():
        for g in range(GROUPS):
            qs_scr[g] = jnp.concatenate(
                [q_ref[:, (g * RH + r) * HD:(g * RH + r + 1) * HD] for r in range(RH)], axis=0).astype(BF)
        m_scr[...] = jnp.full(m_scr.shape, NEG, F32)
        l_scr[...] = jnp.zeros(l_scr.shape, F32)
        acc_scr[...] = jnp.zeros(acc_scr.shape, F32)

    tq = t0 + jnp.concatenate([lax.broadcasted_iota(jnp.int32, (t, 1), 0)] * RH, axis=0)

    def key_bias(mref, g, nk):
        if mode == "sel":
            n = lax.broadcasted_iota(jnp.int32, (bps, nk), 0)
            kk = lax.broadcasted_iota(jnp.int32, (bps, nk), 1)
            expand = jnp.where(n == jnp.right_shift(kk, CMP_SHIFT), 1.0, 0.0).astype(BF)
            bias = (_dot(mref[g][:, 0:bps].astype(BF), expand) - 1.0) * (-NEG)
        else:
            bias = mref[...].astype(F32)
        return jnp.tile(bias, (RH, 1))

    def step(kparts, vparts, kpos, extra_ok, mref):
        nk = kpos.shape[1]
        dist = tq - kpos
        ok = dist >= 0
        if mode == "win":
            ok = ok & (dist < WINDOW)
        if extra_ok is not None:
            ok = ok & extra_ok
        distf = dist.astype(F32)
        raw = [_dot_nt(qs_scr[g], jnp.concatenate(kparts[g], axis=0).astype(BF)) for g in range(GROUPS)]
        probs = []
        for g in range(GROUPS):
            s = jnp.where(ok, raw[g] * SCALE - slope_ref[g] * distf, NEG)
            if mref is not None:
                s = s + key_bias(mref, g, nk)
            m_prev = m_scr[g]
            m_new = jnp.maximum(m_prev, jnp.max(s, axis=-1, keepdims=True))
            a = jnp.exp(m_prev - m_new)
            p = jnp.exp(s - m_new)
            l_scr[g] = a * l_scr[g] + jnp.sum(p, axis=-1, keepdims=True)
            m_scr[g] = m_new
            probs.append((a, p.astype(BF)))
        for g in range(GROUPS):
            a, p = probs[g]
            acc_scr[g] = a * acc_scr[g] + _dot(p, jnp.concatenate(vparts[g], axis=0).astype(BF))

    lane = lax.broadcasted_iota(jnp.int32, (1, pps * PAGE), 1)
    step([[r[pl.ds(kcomp * GROUPS + g, PAGE, stride=rpt), :] for r in page_refs] for g in range(GROUPS)],
         [[r[pl.ds((kcomp + 1) * GROUPS + g, PAGE, stride=rpt), :] for r in page_refs] for g in range(GROUPS)],
         kpos0 + j * (pps * PAGE) + lane, None, mask_ref)

    @pl.when(j == nsteps - 1)
    def _():
        lane1 = lax.broadcasted_iota(jnp.int32, (1, PAGE), 1)
        kv_new = jnp.concatenate([new_ref[...], jnp.zeros((PAGE - t, KV_W), F32)], axis=0)
        step([[kv_new[:, g * HD:(g + 1) * HD]] for g in range(GROUPS)],
             [[kv_new[:, (GROUPS + g) * HD:(GROUPS + g + 1) * HD]] for g in range(GROUPS)],
             t0 + lane1, lane1 < t, maskn_ref)
        for g in range(GROUPS):
            o = jnp.where(m_scr[g] > 0.5 * NEG, acc_scr[g] / jnp.maximum(l_scr[g], 1e-30), 0.0)
            for r in range(RH):
                o_ref[:, (g * RH + r) * HD:(g * RH + r + 1) * HD] = o[r * t:(r + 1) * t]


def _fa_pps(n_pages):
    return _pages_per_step(n_pages, FA_PPS)


def _fa_sample(mode, proj, past, table, slopes, t, t0, kpos0, q_col, rpt, kcomp, new_col, mask=None):
    bsz, nkb = table.shape
    pps = _fa_pps(nkb)
    nsteps = nkb // pps
    rows = RH * t
    wq = GROUPS * RH * HD
    slope_col = jnp.repeat(slopes.reshape(GROUPS, RH), t, axis=1).reshape(GROUPS, rows, 1)
    in_specs = [pl.BlockSpec((GROUPS, rows, 1), lambda b, j, pt: (0, 0, 0)),
                pl.BlockSpec((t, wq), lambda b, j, pt: (b, q_col // wq))]
    for p in range(pps):
        in_specs.append(pl.BlockSpec((PAGE * rpt, HD), lambda b, j, pt, p=p: (pt[b, j * pps + p], 0)))
    in_specs.append(pl.BlockSpec((t, KV_W), lambda b, j, pt: (b, new_col // KV_W)))
    args = [slope_col, proj] + [past] * pps + [proj]
    if mode == "sel":
        in_specs.append(pl.BlockSpec((None, GROUPS, None, t, LANE), lambda b, j, pt: (b, 0, j, 0, 0)))
        in_specs.append(pl.BlockSpec((None, GROUPS, None, t, LANE), lambda b, j, pt: (b, 0, nsteps, 0, 0)))
        args += [mask, mask]
    elif mode == "dsa":
        in_specs.append(pl.BlockSpec((None, t, pps * PAGE), lambda b, j, pt: (b, 0, j)))
        in_specs.append(pl.BlockSpec((None, t, PAGE), lambda b, j, pt: (b, 0, nkb)))
        args += [mask, mask]
    kern = functools.partial(_fa_sample_kernel, mode=mode, t=t, pps=pps, nsteps=nsteps, t0=t0, kpos0=kpos0,
                             rpt=rpt, kcomp=kcomp)
    gs = pltpu.PrefetchScalarGridSpec(
        num_scalar_prefetch=1, grid=(bsz, nsteps), in_specs=in_specs,
        out_specs=pl.BlockSpec((t, wq), lambda b, j, pt: (b, 0)),
        scratch_shapes=[pltpu.VMEM((GROUPS, rows, HD), BF), pltpu.VMEM((GROUPS, rows, 1), F32),
                        pltpu.VMEM((GROUPS, rows, 1), F32), pltpu.VMEM((GROUPS, rows, HD), F32)])
    return pl.pallas_call(
        kern, grid_spec=gs, out_shape=jax.ShapeDtypeStruct((bsz * t, wq), F32),
        compiler_params=_cp(("parallel", "arbitrary")), name="fa_sample_" + mode)(table, *args)


IDX_CHUNK = 256


def _dsa_index_prompt_kernel(q_ref, sq_ref, sk_ref, mask_ref, qb_scr, wb_scr, kpad_scr, score_scr, key_scr,
                             *, qb, t, top):
    i = pl.program_id(1)
    tq = i * qb + lax.broadcasted_iota(jnp.int32, (qb, 1), 0)

    @pl.when(i == 0)
    def _():
        kk = sk_ref[:, 0:IDX_D]
        zero = jnp.zeros((t, IDX_D), F32)
        kpad_scr[0] = jnp.concatenate([kk, zero], axis=1).astype(BF)
        kpad_scr[1] = jnp.concatenate([zero, kk], axis=1).astype(BF)

    qb_scr[...] = (q_ref[...] * (IDX_D ** -0.5)).astype(BF)
    w = sq_ref[:, SM_IDXW:SM_IDXW + IDX_H] * (IDX_H ** -0.5)
    for h in range(IDX_H):
        wb_scr[h] = jnp.broadcast_to(w[:, h:h + 1], (qb, LANE))
    score_scr[...] = jnp.zeros(score_scr.shape, F32)
    ch = min(IDX_CHUNK * LANE // qb, t)

    def chunk(c, carry):
        off = pl.multiple_of(c * ch, ch)
        acc = jnp.zeros((qb, ch), F32)
        for pair in range(IDX_H // 2):
            qp = qb_scr[:, pair * LANE:(pair + 1) * LANE]
            for e in range(2):
                lg = _dot_nt(qp, kpad_scr[e, pl.ds(off, ch), :])
                acc = acc + jnp.tile(wb_scr[2 * pair + e], (1, ch // LANE)) * jnp.maximum(lg, 0.0)
        score_scr[:, pl.ds(off, ch)] = acc
        return carry

    lax.fori_loop(0, ((i + 1) * qb + ch - 1) // ch, chunk, 0)
    s_pos = lax.broadcasted_iota(jnp.int32, (1, t), 1)
    causal = s_pos <= tq
    key_scr[...] = _sortable(jnp.where(causal, score_scr[...], NEG))
    sel = _topk_mask(key_scr, top, s_pos, max(1, (t - 1).bit_length()))
    bias = jnp.where(sel & causal, 0.0, NEG)
    for c in range(t // LANE):
        mask_ref[c * LANE:(c + 1) * LANE, :] = bias[:, c * LANE:(c + 1) * LANE].T.astype(BF)


def _dsa_index_prompt(proj, sm, bsz, t):
    qb = min(2 * LANE, t)
    nq = t // qb
    top = min(IDX_TOPK, t // 4)
    kern = functools.partial(_dsa_index_prompt_kernel, qb=qb, t=t, top=top)
    wq = IDX_H * IDX_D
    return pl.pallas_call(
        kern, grid=(bsz, nq),
        in_specs=[pl.BlockSpec((qb, wq), lambda b, i: (b * nq + i, C_QI // wq)),
                  pl.BlockSpec((qb, LANE), lambda b, i: (b * nq + i, 0)),
                  pl.BlockSpec((t, LANE), lambda b, i: (b, 0))],
        out_specs=pl.BlockSpec((None, t, qb), lambda b, i: (b, 0, i)),
        out_shape=jax.ShapeDtypeStruct((bsz, t, t), BF),
        scratch_shapes=[pltpu.VMEM((qb, wq), BF), pltpu.VMEM((IDX_H, qb, LANE), F32),
                        pltpu.VMEM((2, t, 2 * IDX_D), BF), pltpu.VMEM((qb, t), F32), pltpu.VMEM((qb, t), jnp.int32)],
        compiler_params=_cp(("parallel", "arbitrary")), name="dsa_index_prompt")(proj, sm, sm)


def _dsa_index_sample_kernel(pt_ref, qf_ref, wcol_ref, *refs, t, pps, nsteps, t0, lpad, top):
    page_refs = refs[:pps]
    sn_ref, mask_ref, score_scr, key_scr = refs[pps:]
    j = pl.program_id(1)

    @pl.when(j == 0)
    def _():
        score_scr[...] = jnp.full(score_scr.shape, NEG, F32)

    qf = qf_ref[...].astype(BF)
    wcol = wcol_ref[...] * (IDX_H ** -0.5)

    def scores(logits):
        r = jnp.maximum(logits * (IDX_D ** -0.5), 0.0) * wcol
        sc = r[0:t]
        for h in range(1, IDX_H):
            sc = sc + r[h * t:(h + 1) * t]
        return sc

    kpages_t = jnp.concatenate([r[...] for r in page_refs], axis=1).astype(BF)
    score_scr[:, pl.ds(pl.multiple_of(j * (pps * PAGE), pps * PAGE), pps * PAGE)] = scores(_dot(qf, kpages_t))

    @pl.when(j == nsteps - 1)
    def _():
        tq = t0 + lax.broadcasted_iota(jnp.int32, (t, 1), 0)
        lane = lax.broadcasted_iota(jnp.int32, (1, PAGE), 1)
        kn = jnp.concatenate([sn_ref[:, 0:IDX_D], jnp.zeros((PAGE - t, IDX_D), F32)], axis=0).astype(BF)
        ok = (lane < t) & (t0 + lane <= tq)
        score_scr[:, lpad - PAGE:lpad] = jnp.where(ok, scores(_dot_nt(qf, kn)), NEG)
        s_pos = lax.broadcasted_iota(jnp.int32, (1, lpad), 1)
        key_scr[...] = _sortable(score_scr[...])
        sel = _topk_mask(key_scr, top, s_pos, max(1, (lpad - 1).bit_length()))
        mask_ref[...] = jnp.where(sel & (s_pos <= tq) & (s_pos < t0 + t), 0.0, NEG).astype(BF)


def _dsa_index_sample(qf, wcol, idx_t, table, sm, t, t0):
    bsz, nkb = table.shape
    pps = _pages_per_step(nkb, IDX_PPS)
    nsteps = nkb // pps
    lpad = (nkb + 1) * PAGE
    top = min(IDX_TOPK, (t0 + t) // 4)
    rows = IDX_H * t
    kern = functools.partial(_dsa_index_sample_kernel, t=t, pps=pps, nsteps=nsteps, t0=t0, lpad=lpad, top=top)
    gs = pltpu.PrefetchScalarGridSpec(
        num_scalar_prefetch=1, grid=(bsz, nsteps),
        in_specs=[pl.BlockSpec((rows, IDX_D), lambda b, j, pt: (b, 0)),
                  pl.BlockSpec((rows, 1), lambda b, j, pt: (b, 0))]
        + [pl.BlockSpec((IDX_D, PAGE), lambda b, j, pt, p=p: (pt[b, j * pps + p], 0)) for p in range(pps)]
        + [pl.BlockSpec((t, LANE), lambda b, j, pt: (b, 0))],
        out_specs=pl.BlockSpec((None, t, lpad), lambda b, j, pt: (b, 0, 0)),
        scratch_shapes=[pltpu.VMEM((t, lpad), F32), pltpu.VMEM((t, lpad), jnp.int32)])
    return pl.pallas_call(
        kern, grid_spec=gs, out_shape=jax.ShapeDtypeStruct((bsz, t, lpad), BF),
        compiler_params=_cp(("parallel", "arbitrary")), name="dsa_index_sample")(
            table, qf, wcol, *([idx_t] * pps), sm)


def _silu(z):
    return z * jax.nn.sigmoid(z)


def _combine_nsa_kernel(oc_ref, os_ref, ow_ref, gate_ref, z_ref, o_ref):
    gate = jax.nn.sigmoid(gate_ref[:, 0:3 * NSA_H])
    for h in range(NSA_H):
        sl = slice(h * HD, (h + 1) * HD)
        o = (gate[:, 3 * h:3 * h + 1] * oc_ref[:, sl] + gate[:, 3 * h + 1:3 * h + 2] * os_ref[:, sl]
             + gate[:, 3 * h + 2:3 * h + 3] * ow_ref[:, sl])
        o_ref[:, sl] = (o * _silu(z_ref[:, sl])).astype(BF)


def _combine_nsa(o_cmp, o_sel, o_win, proj, sm, tm):
    m, n = o_cmp.shape
    row = pl.BlockSpec((tm, n), lambda i: (i, 0))
    return pl.pallas_call(
        _combine_nsa_kernel, grid=(m // tm,),
        in_specs=[row, row, row, pl.BlockSpec((tm, LANE), lambda i: (i, 1)),
                  pl.BlockSpec((tm, n), lambda i: (i, C_ZN // n))],
        out_specs=row, out_shape=jax.ShapeDtypeStruct((m, n), BF),
        compiler_params=_cp(("parallel",)), name="combine_nsa")(o_cmp, o_sel, o_win, sm, proj)


def _combine_dsa_kernel(o_ref_in, z_ref, o_ref):
    o_ref[...] = (o_ref_in[...] * _silu(z_ref[...])).astype(BF)


def _combine_dsa(o, proj, tm):
    m, n = o.shape
    row = pl.BlockSpec((tm, n), lambda i: (i, 0))
    return pl.pallas_call(
        _combine_dsa_kernel, grid=(m // tm,),
        in_specs=[row, pl.BlockSpec((tm, n), lambda i: (i, C_ZD // n))],
        out_specs=row, out_shape=jax.ShapeDtypeStruct((m, n), BF),
        compiler_params=_cp(("parallel",)), name="combine_dsa")(o, proj)


PREP_TN = 512
PREP_TK = 1024


def _prep_tables():
    src, off = {}, 0
    for name, width in IN_SPLITS:
        src[name] = off
        off += width
    width = dict(IN_SPLITS)
    shifts, base, cls = [], [], []
    for name, _ in PROJ_LAYOUT:
        for c in range(width[name] // PREP_TN):
            s = src[name] + c * PREP_TN
            if s % LANE not in shifts:
                shifts.append(s % LANE)
            base.append(s // LANE)
            cls.append(shifts.index(s % LANE))
    return tuple(shifts), base, cls


def _prep_kernel(base_ref, cls_ref, *refs, shifts):
    o_ref = refs[-1]
    j = pl.program_id(1)
    for k, s in enumerate(shifts):
        @pl.when(cls_ref[j] == k)
        def _():
            win = jnp.concatenate([r[...] for r in refs[:-1]], axis=0)
            o_ref[...] = win[s:s + PREP_TN, :].T.astype(BF)


def _prep_w_in(w_in_t, l):
    shifts, base, cls = _prep_tables()
    assert all(sh % 8 == 0 for sh in shifts)
    k = w_in_t.shape[2]
    nwin = PREP_TN // LANE + 1
    kern = functools.partial(_prep_kernel, shifts=shifts)
    gs = pltpu.PrefetchScalarGridSpec(
        num_scalar_prefetch=2, grid=(k // PREP_TK, len(base)),
        in_specs=[pl.BlockSpec((None, LANE, PREP_TK), lambda i, j, bs, cs, m=m: (l, bs[j] + m, i))
                  for m in range(nwin)],
        out_specs=pl.BlockSpec((PREP_TK, PREP_TN), lambda i, j, bs, cs: (i, j)))
    return pl.pallas_call(
        kern, grid_spec=gs, out_shape=jax.ShapeDtypeStruct((k, PROJ_W), BF),
        compiler_params=_cp(("parallel", "parallel")), name="prep_w_in")(
            jnp.asarray(base, jnp.int32), jnp.asarray(cls, jnp.int32), *([w_in_t] * nwin))


def _prep_small_kernel(g_ref, a_ref, b_ref, o_ref, *, gate_off, w_off, k_off):
    tk = o_ref.shape[0]
    ab = jnp.concatenate([a_ref[...], b_ref[...]], axis=0)
    rows = jnp.concatenate([ab[k_off:k_off + IDX_D], ab[w_off:w_off + IDX_H], jnp.zeros((32, tk), F32),
                            g_ref[gate_off:gate_off + 3 * NSA_H], jnp.zeros((SM_W - 176, tk), F32)], axis=0)
    o_ref[...] = rows.T.astype(BF)


def _prep_w_small(w_in_t, l):
    src, off = {}, 0
    for name, width in IN_SPLITS:
        src[name] = off
        off += width
    gate_blk, ab_blk = src['nsa_gate'] // LANE, src['idx_w'] // LANE
    offs = dict(gate_off=src['nsa_gate'] - gate_blk * LANE, w_off=src['idx_w'] - ab_blk * LANE,
                k_off=src['idx_k'] - ab_blk * LANE)
    assert all(v % 8 == 0 for v in offs.values()) and offs['k_off'] + IDX_D <= 2 * LANE
    k = w_in_t.shape[2]
    kern = functools.partial(_prep_small_kernel, **offs)
    return pl.pallas_call(
        kern, grid=(k // PREP_TK,),
        in_specs=[pl.BlockSpec((None, LANE, PREP_TK), lambda i, blk=blk: (l, blk, i))
                  for blk in (gate_blk, ab_blk, ab_blk + 1)],
        out_specs=pl.BlockSpec((PREP_TK, SM_W), lambda i: (i, 0)),
        out_shape=jax.ShapeDtypeStruct((k, SM_W), BF),
        compiler_params=_cp(("parallel",)), name="prep_w_small")(w_in_t, w_in_t, w_in_t)


def _slopes(n):
    return jnp.exp2(-8.0 * jnp.arange(1, n + 1, dtype=F32) / n)


def _tail(proj, xf, wts, a_in, b_in, tm_mm, tm_ln):
    hm = _gated_matmul(a_in, b_in, wts['pa'], wts['pd'], proj, tm_mm, 512, wts['l'])
    h = _matmul(hm, wts['out'], tm_mm, 512, l=wts['l'])
    return _residual_ln(xf, h, wts['gain'], wts['bias'], tm_ln)


def _layer_prompt(xf, xb, bsz, t, wts):
    tm = min(1024, bsz * t)
    proj = _matmul(xb, wts['in'], tm, 1024)
    sm = _matmul(xb, wts['in_small'], tm, SM_W)
    n_pages = t // PAGE
    table = jnp.arange(bsz * n_pages, dtype=jnp.int32).reshape(bsz, n_pages)
    kcvc = _compress(proj, table, C_KVN // (4 * HD), 0, wts['pe_page'], wts['w1'], wts['w2'], n_pages, 1)
    sl_n, sl_d = _slopes(NSA_H), _slopes(DSA_H)
    o_cmp, selm = _nsa_select(proj, kcvc, sl_n, bsz, t, t // CMP, 0, 0)
    o_sel = _fa_prompt("sel", proj, sl_n, bsz, t, C_QN, C_KVN + KV_W, C_KVN + KV_W + GROUPS * HD, selm)
    o_win = _fa_prompt("win", proj, sl_n, bsz, t, C_QN, C_KVN + 2 * KV_W, C_KVN + 2 * KV_W + GROUPS * HD)
    dmask = _dsa_index_prompt(proj, sm, bsz, t)
    o_dsa = _fa_prompt("dsa", proj, sl_d, bsz, t, C_QD, C_KVD, C_KVD + GROUPS * HD, dmask)
    tme = min(256, bsz * t)
    a_in = _combine_nsa(o_cmp, o_sel, o_win, proj, sm, tme)
    b_in = _combine_dsa(o_dsa, proj, tme)
    y, yb = _tail(proj, xf, wts, a_in, b_in, tm, tme)
    return y, yb, proj, sm


def _layer_sample(xf, xb, bsz, t, wts, nsa_rows, win_rows, dsa_rows, idx_t, page_table, win_table):
    rows = bsz * t
    proj = _matmul(xb, wts['in'], rows, 512)
    sm = _matmul(xb, wts['in_small'], rows, SM_W)
    n_pages = page_table.shape[1]
    t0 = n_pages * PAGE
    wbuf = win_table.shape[1] * PAGE
    kcvc = _compress(nsa_rows, page_table, 0, 4 * GROUPS, wts['pe_page'], wts['w1'], wts['w2'], min(64, n_pages),
                     _pages_per_step(min(64, n_pages), FA_PPS))
    sl_n, sl_d = _slopes(NSA_H), _slopes(DSA_H)
    o_cmp, selm = _nsa_select(proj, kcvc, sl_n, bsz, t, -(-(t0 + t) // CMP), t0, n_pages)
    o_sel = _fa_sample("sel", proj, nsa_rows, page_table, sl_n, t, t0, 0, C_QN, 4 * GROUPS, 2, C_KVN + KV_W, selm)
    o_win = _fa_sample("win", proj, win_rows, win_table, sl_n, t, t0, t0 - wbuf, C_QN, 2 * GROUPS, 0,
                       C_KVN + 2 * KV_W)
    qi = proj[:, C_QI:C_QI + IDX_H * IDX_D].reshape(bsz, t, IDX_H, IDX_D).transpose(0, 2, 1, 3)
    qf = qi.reshape(bsz * IDX_H * t, IDX_D)
    wi = sm[:, SM_IDXW:SM_IDXW + IDX_H].reshape(bsz, t, IDX_H).transpose(0, 2, 1)
    wcol = wi.reshape(bsz * IDX_H * t, 1)
    dmask = _dsa_index_sample(qf, wcol, idx_t, page_table, sm, t, t0)
    o_dsa = _fa_sample("dsa", proj, dsa_rows, page_table, sl_d, t, t0, 0, C_QD, 2 * GROUPS, 0, C_KVD, dmask)
    a_in = _combine_nsa(o_cmp, o_sel, o_win, proj, sm, rows)
    b_in = _combine_dsa(o_dsa, proj, rows)
    y, yb = _tail(proj, xf, wts, a_in, b_in, rows, rows)
    return y, yb, proj, sm


def _layer_weights(l, w_in, cmp_pe, cmp_w1, cmp_w2, w_proj_nsa, w_proj_dsa, w_out, ln_gain, ln_bias):
    cmp_pe, cmp_w1, cmp_w2, ln_gain, ln_bias = (a[l] for a in (cmp_pe, cmp_w1, cmp_w2, ln_gain, ln_bias))
    pe = jnp.concatenate([cmp_pe, cmp_pe], axis=1)
    pe_page = jnp.concatenate([pe[0], pe[0], pe[1], pe[1]], axis=1)
    w_in_t = jnp.swapaxes(w_in, 1, 2)
    return {'in': _prep_w_in(w_in_t, l), 'in_small': _prep_w_small(w_in_t, l), 'pe_page': pe_page,
            'w1': cmp_w1.astype(BF), 'w2': cmp_w2.astype(BF),
            'pa': w_proj_nsa.astype(BF), 'pd': w_proj_dsa.astype(BF), 'out': w_out.astype(BF),
            'l': l, 'gain': ln_gain, 'bias': ln_bias}


STATE_TM = 256


def _state_kernel(x_ref, *refs, tm):
    nsa_ref, dsa_ref = refs[-2:]
    for cg in range(4 * GROUPS):
        nsa_ref[pl.ds(cg, tm, stride=4 * GROUPS), :] = x_ref[:, cg * HD:(cg + 1) * HD]
    for cg in range(2 * GROUPS):
        dsa_ref[pl.ds(cg, tm, stride=2 * GROUPS), :] = x_ref[:, 3 * KV_W + cg * HD:3 * KV_W + (cg + 1) * HD]


def _write_state(proj, l, prev):
    m = proj.shape[0]
    tm = min(STATE_TM, m)
    nt = m // tm
    kern = functools.partial(_state_kernel, tm=tm)
    width = 4 * KV_W
    in_specs = [pl.BlockSpec((tm, width), lambda i: (i, C_KVN // width))]
    args = [proj]
    aliases = {}
    if prev is not None:
        in_specs += [pl.BlockSpec(memory_space=pl.ANY), pl.BlockSpec(memory_space=pl.ANY)]
        args += list(prev)
        aliases = {1: 0, 2: 1}
    return pl.pallas_call(
        kern, grid=(nt,), in_specs=in_specs,
        out_specs=[pl.BlockSpec((tm * 4 * GROUPS, HD), lambda i: (l * nt + i, 0)),
                   pl.BlockSpec((tm * 2 * GROUPS, HD), lambda i: (l * nt + i, 0))],
        out_shape=[jax.ShapeDtypeStruct((DEPTH * m * 4 * GROUPS, HD), F32),
                   jax.ShapeDtypeStruct((DEPTH * m * 2 * GROUPS, HD), F32)],
        input_output_aliases=aliases,
        compiler_params=_cp(("arbitrary",)), name="write_state")(*args)


def _win_and_idx(proj, sm, bsz, t):
    win = proj[:, C_KVN + 2 * KV_W:C_KVN + 3 * KV_W].reshape(bsz, t, 2, GROUPS, HD)
    idx_k = sm[:, 0:IDX_D].reshape(bsz, t, IDX_D)
    return win, idx_k


def kernel(x_prompt, x_sample, cache_nsa_kv, state_nsa_win, cache_dsa_kv, cache_dsa_idx, page_table,
           w_in, cmp_pe, cmp_w1, cmp_w2, w_proj_nsa, w_proj_dsa, w_out, ln_gain, ln_bias):
    bp, tp, _ = x_prompt.shape
    bs, ts, _ = x_sample.shape
    n_pool = cache_nsa_kv.shape[1]
    wpages = state_nsa_win.shape[2] // PAGE
    nsa_rows = cache_nsa_kv.reshape(-1, HD)
    dsa_rows = cache_dsa_kv.reshape(-1, HD)
    idx_t = jnp.swapaxes(cache_dsa_idx, 2, 3).reshape(-1, PAGE)
    win_rows = state_nsa_win.reshape(-1, HD)
    win_table = jnp.arange(bs * wpages, dtype=jnp.int32).reshape(bs, wpages)
    yp, ys = x_prompt.reshape(bp * tp, D_MODEL), x_sample.reshape(bs * ts, D_MODEL)
    ypb, ysb = yp.astype(BF), ys.astype(BF)
    outs = [[] for _ in range(4)]
    state_p = state_s = None
    for l in range(DEPTH):
        wts = _layer_weights(l, w_in, cmp_pe, cmp_w1, cmp_w2, w_proj_nsa, w_proj_dsa, w_out, ln_gain, ln_bias)
        yp, ypb, proj_p, sm_p = _layer_prompt(yp, ypb, bp, tp, wts)
        ys, ysb, proj_s, sm_s = _layer_sample(ys, ysb, bs, ts, wts, nsa_rows, win_rows, dsa_rows, idx_t,
                                              page_table + l * n_pool, win_table + l * bs * wpages)
        state_p = _write_state(proj_p, l, state_p)
        state_s = _write_state(proj_s, l, state_s)
        win_p, idx_p = _win_and_idx(proj_p, sm_p, bp, tp)
        win_s, idx_s = _win_and_idx(proj_s, sm_s, bs, ts)
        win_all = jnp.concatenate([state_nsa_win[l], win_s], axis=1)
        keep_p, keep_s = min(WINDOW, tp), min(WINDOW, win_all.shape[1])
        for lst, val in zip(outs, (win_p[:, tp - keep_p:], win_all[:, win_all.shape[1] - keep_s:], idx_p, idx_s)):
            lst.append(val)
    win_po, win_so, idx_po, idx_so = (jnp.stack(o) for o in outs)
    return (yp.reshape(bp, tp, D_MODEL), ys.reshape(bs, ts, D_MODEL),
            state_p[0].reshape(DEPTH, bp, tp, 4, GROUPS, HD), state_s[0].reshape(DEPTH, bs, ts, 4, GROUPS, HD),
            win_po, win_so,
            state_p[1].reshape(DEPTH, bp, tp, 2, GROUPS, HD), state_s[1].reshape(DEPTH, bs, ts, 2, GROUPS, HD),
            idx_po, idx_so)
```

```python
import functools

import jax
import jax.numpy as jnp
from jax import lax
from jax.experimental import pallas as pl
from jax.experimental.pallas import tpu as pltpu

D_MODEL = 4096
DEPTH = 2
PAGE = 128
HD = 128
NSA_H = 16
DSA_H = 16
GROUPS = 2
RH = NSA_H // GROUPS
CMP = 64
CMP_SHIFT = 6
N_SEL = 16
WINDOW = 512
IDX_H = 32
IDX_D = 64
IDX_TOPK = 256
LN_EPS = 1e-5
ALPHA = (2 * DEPTH) ** 0.25
NEG = -1e30
SEL_FORCE = 1e4
SCALE = HD ** -0.5
LOG2E = 1.4426950408889634
INT_MIN = -(2 ** 31)

IN_SPLITS = (('nsa_q', 2048), ('nsa_kv', 1536), ('nsa_gate', 48), ('nsa_z', 2048), ('dsa_q', 2048),
             ('dsa_kv', 512), ('idx_q', 2048), ('idx_w', 32), ('idx_k', 64), ('dsa_z', 2048), ('merge', 8192))

C_QN, C_ZN, C_QD, C_ZD, C_QI, C_MG, C_KVN, C_KVD = 0, 2048, 4096, 6144, 8192, 10240, 18432, 19968
PROJ_W = 20480
PROJ_LAYOUT = (('nsa_q', C_QN), ('nsa_z', C_ZN), ('dsa_q', C_QD), ('dsa_z', C_ZD), ('idx_q', C_QI),
               ('merge', C_MG), ('nsa_kv', C_KVN), ('dsa_kv', C_KVD))
SM_W = 256
SM_IDXW = 64
LANE = 128
KV_W = 2 * GROUPS * HD

VMEM_LIMIT = 48 * 1024 * 1024
BF16_TILE_ROWS = 16
BF = jnp.bfloat16
F32 = jnp.float32


def _cp(sem):
    return pltpu.CompilerParams(dimension_semantics=sem, vmem_limit_bytes=VMEM_LIMIT)


def _dot_nt(a, b):
    return lax.dot_general(a, b, (((1,), (1,)), ((), ())), preferred_element_type=F32)


def _dot(a, b):
    return jnp.dot(a, b, preferred_element_type=F32)


def _mm_kernel(x_ref, w_ref, o_ref):
    o_ref[...] = _dot(x_ref[...], w_ref[...]).astype(o_ref.dtype)


def _w_spec(w, l, tn):
    k = w.shape[-2]
    if w.ndim == 3:
        return pl.BlockSpec((None, k, tn), lambda i, j: (l, 0, j))
    return pl.BlockSpec((k, tn), lambda i, j: (0, j))


def _matmul(x, w, tm, tn, out_dtype=F32, l=0):
    m, k = x.shape
    n = w.shape[-1]
    return pl.pallas_call(
        _mm_kernel, grid=(m // tm, n // tn),
        in_specs=[pl.BlockSpec((tm, k), lambda i, j: (i, 0)), _w_spec(w, l, tn)],
        out_specs=pl.BlockSpec((tm, tn), lambda i, j: (i, j)),
        out_shape=jax.ShapeDtypeStruct((m, n), out_dtype),
        compiler_params=_cp(("parallel", "parallel")), name="matmul")(x, w)


def _gated_mm_kernel(a_ref, b_ref, wa_ref, wb_ref, m0_ref, m1_ref, o_ref):
    a = _dot(a_ref[...], wa_ref[...])
    b = _dot(b_ref[...], wb_ref[...])
    o_ref[...] = (jax.nn.sigmoid(m0_ref[...]) * a + jax.nn.sigmoid(m1_ref[...]) * b).astype(o_ref.dtype)


def _gated_matmul(a_in, b_in, wa, wb, proj, tm, tn, l=0):
    m, k = a_in.shape
    n = wa.shape[-1]
    c0, c1 = C_MG // tn, (C_MG + D_MODEL) // tn
    return pl.pallas_call(
        _gated_mm_kernel, grid=(m // tm, n // tn),
        in_specs=[pl.BlockSpec((tm, k), lambda i, j: (i, 0)), pl.BlockSpec((tm, k), lambda i, j: (i, 0)),
                  _w_spec(wa, l, tn), _w_spec(wb, l, tn),
                  pl.BlockSpec((tm, tn), lambda i, j: (i, c0 + j)), pl.BlockSpec((tm, tn), lambda i, j: (i, c1 + j))],
        out_specs=pl.BlockSpec((tm, tn), lambda i, j: (i, j)),
        out_shape=jax.ShapeDtypeStruct((m, n), BF),
        compiler_params=_cp(("parallel", "parallel")), name="gated_matmul")(a_in, b_in, wa, wb, proj, proj)


def _ln_kernel(x_ref, h_ref, g_ref, b_ref, y_ref, yb_ref):
    v = ALPHA * x_ref[...] + h_ref[...]
    mu = jnp.mean(v, axis=-1, keepdims=True)
    c = v - mu
    var = jnp.mean(c * c, axis=-1, keepdims=True)
    y = c * lax.rsqrt(var + LN_EPS) * g_ref[...] + b_ref[...]
    y_ref[...] = y
    yb_ref[...] = y.astype(BF)


def _residual_ln(x, h, gain, bias, tm):
    m, n = x.shape
    row = pl.BlockSpec((tm, n), lambda i: (i, 0))
    vec = pl.BlockSpec((1, n), lambda i: (0, 0))
    return pl.pallas_call(
        _ln_kernel, grid=(m // tm,), in_specs=[row, row, vec, vec], out_specs=[row, row],
        out_shape=[jax.ShapeDtypeStruct((m, n), F32), jax.ShapeDtypeStruct((m, n), BF)],
        compiler_params=_cp(("parallel",)), name="residual_ln")(x, h, gain.reshape(1, n), bias.reshape(1, n))


def _gelu_tanh(x):
    return 0.5 * x * (1.0 + jnp.tanh(0.7978845608028654 * (x + 0.044715 * (x * x * x))))


CMP_TOKC = 8


def _compress_kernel(pt_ref, *refs, pg, pps, rpt):
    page_refs = refs[:pps]
    pe_ref, w1_ref, w2_ref, o_ref, slab_ref = refs[pps:]
    p = pl.program_id(2)
    for k, x_ref in enumerate(page_refs):
        for cg in range(2 * GROUPS):
            x = x_ref[pl.ds(cg, PAGE, stride=rpt), :] if rpt else x_ref[:, cg * HD:(cg + 1) * HD]
            slab_ref[cg, pl.ds(pl.multiple_of((p * pps + k) * PAGE, PAGE), PAGE), :] = (
                x + pe_ref[:, cg * HD:(cg + 1) * HD])

    @pl.when(p == pg // pps - 1)
    def _():
        nblk = pg * (PAGE // CMP)
        for c in range(2):
            acc = jnp.zeros((GROUPS * nblk, HD), F32)
            for tok0 in range(0, CMP, CMP_TOKC):
                lhs = jnp.concatenate(
                    [jnp.concatenate([slab_ref[c * GROUPS + g, pl.ds(tok, nblk, stride=CMP), :]
                                      for g in range(GROUPS)], axis=0).astype(BF)
                     for tok in range(tok0, tok0 + CMP_TOKC)], axis=1)
                acc = acc + _dot(lhs, w1_ref[c, tok0 * HD:(tok0 + CMP_TOKC) * HD, :])
            out = _dot(_gelu_tanh(acc).astype(BF), w2_ref[c])
            for g in range(GROUPS):
                o_ref[c, g] = out[g * nblk:(g + 1) * nblk]


def _compress(src, table, col_blk, rpt, pe_page, w1, w2, pg, pps):
    b, n_pages = table.shape
    ns = n_pages // pg
    nblk = pg * (PAGE // CMP)
    kern = functools.partial(_compress_kernel, pg=pg, pps=pps, rpt=rpt)
    if rpt:
        page_specs = [pl.BlockSpec((PAGE * rpt, HD), lambda bi, s, p, pt, k=k: (pt[bi, s * pg + p * pps + k], 0))
                      for k in range(pps)]
    else:
        page_specs = [pl.BlockSpec((PAGE, 4 * HD), lambda bi, s, p, pt, k=k: (pt[bi, s * pg + p * pps + k], col_blk))
                      for k in range(pps)]
    gs = pltpu.PrefetchScalarGridSpec(
        num_scalar_prefetch=1, grid=(b, ns, pg // pps),
        in_specs=page_specs + [pl.BlockSpec((PAGE, 4 * HD), lambda bi, s, p, pt: (0, 0)),
                               pl.BlockSpec((2, CMP * HD, HD), lambda bi, s, p, pt: (0, 0, 0)),
                               pl.BlockSpec((2, HD, HD), lambda bi, s, p, pt: (0, 0, 0))],
        out_specs=pl.BlockSpec((None, 2, GROUPS, nblk, HD), lambda bi, s, p, pt: (bi, 0, 0, s, 0)),
        scratch_shapes=[pltpu.VMEM((2 * GROUPS, pg * PAGE, HD), F32)])
    return pl.pallas_call(
        kern, grid_spec=gs, out_shape=jax.ShapeDtypeStruct((b, 2, GROUPS, ns * nblk, HD), F32),
        compiler_params=_cp(("parallel", "arbitrary", "arbitrary")), name="compress")(
            table, *([src] * pps), pe_page, w1, w2)


def _sortable(x):
    b = lax.bitcast_convert_type(x, jnp.int32)
    return jnp.where(b < 0, b ^ jnp.int32(0x7FFFFFFF), b)


def _count(pred):
    return jnp.sum(jnp.where(pred, 1.0, 0.0), axis=-1, keepdims=True)


def _topk_mask(key_ref, k, idx, idx_bits):
    rows = key_ref.shape[0]
    kf = float(k)
    zero = jnp.zeros((rows, 1), jnp.int32)
    t0 = jnp.where(_count(key_ref[...] >= zero) >= kf, zero, jnp.full((rows, 1), INT_MIN, jnp.int32))

    def value_bit(i, t):
        cand = t | jnp.left_shift(jnp.int32(1), jnp.int32(30) - i)
        return jnp.where(_count(key_ref[...] >= cand) >= kf, cand, t)

    t = lax.fori_loop(0, 31, value_bit, t0)
    keys = key_ref[...]
    surplus = jnp.max(_count(keys >= t)) > kf

    def tie_cut(_):
        need = kf - _count(key_ref[...] > t)

        def index_bit(i, c):
            cand = c | jnp.left_shift(jnp.int32(1), jnp.int32(idx_bits - 1) - i)
            below = _count((key_ref[...] == t) & (idx < cand))
            return jnp.where(below < need, cand, c)

        return lax.fori_loop(0, idx_bits, index_bit, zero)

    c0 = lax.cond(surplus, tie_cut, lambda _: jnp.full((rows, 1), 2 ** idx_bits - 1, jnp.int32), None)
    return (keys > t) | ((keys == t) & (idx <= c0))


def _masked_softmax(s, mask):
    s = jnp.where(mask, s, NEG)
    e = jnp.where(mask, jnp.exp(s - jnp.max(s, axis=-1, keepdims=True)), 0.0)
    return e / jnp.maximum(jnp.sum(e, axis=-1, keepdims=True), 1e-30)


def _nsa_select_kernel(slopes_ref, q_ref, kc_ref, vc_ref, ocmp_ref, sel_ref, key_scr,
                       *, qb, nb, nbs, nbs_pad, t0, bps):
    g = pl.program_id(1)
    i = pl.program_id(2)
    tq = t0 + i * qb + lax.broadcasted_iota(jnp.int32, (qb, 1), 0)
    blk_end = lax.broadcasted_iota(jnp.int32, (1, nb), 1) * CMP + (CMP - 1)
    valid = blk_end <= tq
    distf = (tq - blk_end).astype(F32)
    kc = kc_ref[...].astype(BF)
    vc = vc_ref[...].astype(BF)
    qs = jnp.concatenate([q_ref[:, r * HD:(r + 1) * HD] for r in range(RH)], axis=0).astype(BF)
    slope = jnp.concatenate([jnp.full((qb, 1), slopes_ref[g * RH + r], F32) for r in range(RH)], axis=0)
    s = _dot_nt(qs, kc) * SCALE - slope * jnp.tile(distf, (RH, 1))
    p = _masked_softmax(s, jnp.tile(valid, (RH, 1)))
    o = _dot(p.astype(BF), vc)
    imp = jnp.zeros((qb, nb), F32)
    for r in range(RH):
        ocmp_ref[:, r * HD:(r + 1) * HD] = o[r * qb:(r + 1) * qb]
        imp = imp + p[r * qb:(r + 1) * qb]
    if nbs_pad > nb:
        imp = jnp.concatenate([imp, jnp.zeros((qb, nbs_pad - nb), F32)], axis=1)
    j = lax.broadcasted_iota(jnp.int32, (1, nbs_pad), 1)
    cur = jnp.right_shift(tq, CMP_SHIFT)
    forced = (j == 0) | (j == cur) | (j == cur - 1)
    score = jnp.where(j <= cur, jnp.where(forced, SEL_FORCE, imp), -SEL_FORCE)
    key_scr[...] = _sortable(score)
    sel = _topk_mask(key_scr, min(N_SEL, nbs), j, max(1, (nbs_pad - 1).bit_length()))
    sel_f = jnp.where(sel, 1.0, 0.0)
    for s_i in range(sel_ref.shape[0]):
        sel_ref[s_i] = jnp.concatenate(
            [sel_f[:, s_i * bps:(s_i + 1) * bps], jnp.zeros((qb, LANE - bps), F32)], axis=1)


def _nsa_select_t_kernel(slopes_ref, q_ref, kc_ref, vc_ref, ocmp_ref, sel_ref, *, qb, nb, n_sel, t0):
    g = pl.program_id(1)
    i = pl.program_id(2)
    n_col = lax.broadcasted_iota(jnp.int32, (nb, qb), 0)
    tq = t0 + i * qb + lax.broadcasted_iota(jnp.int32, (nb, qb), 1)
    blk_end = n_col * CMP + (CMP - 1)
    valid = blk_end <= tq
    distf = (tq - blk_end).astype(F32)
    kc = kc_ref[...].astype(BF)
    vct = vc_ref[...].T.astype(BF)
    qs = jnp.concatenate([q_ref[:, r * HD:(r + 1) * HD] for r in range(RH)], axis=0).astype(BF)
    raw = _dot_nt(kc, qs)
    imp = jnp.zeros((nb, qb), F32)
    probs = []
    for r in range(RH):
        s = jnp.where(valid, raw[:, r * qb:(r + 1) * qb] * SCALE - slopes_ref[g * RH + r] * distf, NEG)
        e = jnp.where(valid, jnp.exp(s - jnp.max(s, axis=0, keepdims=True)), 0.0)
        p = e / jnp.maximum(jnp.sum(e, axis=0, keepdims=True), 1e-30)
        probs.append(p.astype(BF))
        imp = imp + p
    o_t = _dot(vct, jnp.concatenate(probs, axis=1))
    for r in range(RH):
        ocmp_ref[:, r * HD:(r + 1) * HD] = o_t[:, r * qb:(r + 1) * qb].T
    cur = jnp.right_shift(tq, CMP_SHIFT)
    forced = (n_col == 0) | (n_col == cur) | (n_col == cur - 1)
    score = jnp.where(n_col <= cur, jnp.where(forced, SEL_FORCE, imp), -SEL_FORCE)
    rank = jnp.zeros((nb, qb), F32)
    for a in range(nb):
        row = score[a:a + 1, :]
        rank = rank + jnp.where((row > score) | ((row == score) & (n_col > a)), 1.0, 0.0)
    bias_t = jnp.where(rank < float(n_sel), 0.0, NEG)
    for n in range(nb):
        sel_ref[n] = jnp.broadcast_to(bias_t[n:n + 1, :], (8, qb))


def _nsa_select(proj, kcvc, slopes, bsz, t, nbs, t0, past_pages):
    qb = min(t, LANE if past_pages else 2 * LANE)
    nq = t // qb
    nb = kcvc.shape[3]
    nbs_pad = nbs if nbs == nb else -(-nbs // LANE) * LANE
    if past_pages:
        bps = _fa_pps(past_pages) * (PAGE // CMP)
        n_grp = past_pages * (PAGE // CMP) // bps + 1
        assert n_grp * bps <= nbs_pad and nq == 1
        kern = functools.partial(_nsa_select_kernel, qb=qb, nb=nb, nbs=nbs, nbs_pad=nbs_pad, t0=t0, bps=bps)
        scratch = [pltpu.VMEM((qb, nbs_pad), jnp.int32)]
    else:
        assert nbs == nb
        kern = functools.partial(_nsa_select_t_kernel, qb=qb, nb=nb, n_sel=min(N_SEL, nbs), t0=t0)
        scratch = []
    wq = RH * HD
    if past_pages:
        sel_spec = pl.BlockSpec((None, None, n_grp, qb, LANE), lambda b, g, i: (b, g, 0, 0, 0))
        sel_shape = jax.ShapeDtypeStruct((bsz, GROUPS, n_grp, t, LANE), F32)
    else:
        sel_spec = pl.BlockSpec((None, None, nbs, 8, qb), lambda b, g, i: (b, g, 0, 0, i))
        sel_shape = jax.ShapeDtypeStruct((bsz, GROUPS, nbs, 8, t), F32)
    return pl.pallas_call(
        kern, grid=(bsz, GROUPS, nq),
        in_specs=[pl.BlockSpec(memory_space=pltpu.SMEM),
                  pl.BlockSpec((qb, wq), lambda b, g, i: (b * nq + i, C_QN // wq + g)),
                  pl.BlockSpec((None, None, None, nb, HD), lambda b, g, i: (b, 0, g, 0, 0)),
                  pl.BlockSpec((None, None, None, nb, HD), lambda b, g, i: (b, 1, g, 0, 0))],
        out_specs=[pl.BlockSpec((qb, wq), lambda b, g, i: (b * nq + i, g)), sel_spec],
        out_shape=[jax.ShapeDtypeStruct((bsz * t, NSA_H * HD), F32), sel_shape],
        scratch_shapes=scratch,
        compiler_params=_cp(("parallel", "parallel", "parallel")), name="nsa_select")(slopes, proj, kcvc, kcvc)


def _fa_prompt_kernel(*refs, mode, qb, kb):
    if mode == "win":
        slopes_ref, q_ref, k_ref, v_ref, o_ref = refs[:5]
        mask_ref = None
    else:
        slopes_ref, q_ref, k_ref, v_ref, mask_ref, o_ref = refs[:6]
    qs_scr, bias_scr, s_scr, p_scr, mb_scr, m_scr, a_scr, acc_scr = refs[-8:]
    g = pl.program_id(1)
    i = pl.program_id(2)
    k_local = lax.broadcasted_iota(jnp.int32, (kb, qb), 0)
    q_local = lax.broadcasted_iota(jnp.int32, (kb, qb), 1)
    for r in range(RH):
        qs_scr[r * qb:(r + 1) * qb, :] = (q_ref[:, r * HD:(r + 1) * HD] * (SCALE * LOG2E)).astype(BF)

    @pl.when(i == 0)
    def _():
        for r in range(RH):
            bias_scr[:, r * qb:(r + 1) * qb] = (slopes_ref[g * RH + r] * LOG2E) * k_local.astype(F32)

    m_scr[...] = jnp.full(m_scr.shape, NEG, F32)
    acc_scr[...] = jnp.zeros(acc_scr.shape, F32)
    j_hi = ((i + 1) * qb - 1) // kb
    j_lo = jnp.maximum(i * qb - (WINDOW - 1), 0) // kb if mode == "win" else 0

    def body(j, carry):
        off = pl.multiple_of(j * kb, kb)
        s_scr[...] = _dot_nt(k_ref[pl.ds(off, kb), :].astype(BF), qs_scr[...])
        dist = (i * qb + q_local) - (j * kb + k_local)
        ok = dist >= 0
        if mode == "win":
            ok = ok & (dist < WINDOW)
        mb = jnp.where(ok, 0.0, NEG)
        if mode == "sel":
            tiles = mask_ref[pl.ds(j * (kb // CMP), kb // CMP)]
            mb = mb + jnp.concatenate([jnp.tile(tiles[n], (CMP // 8, 1)) for n in range(kb // CMP)], axis=0)
        elif mode == "dsa":
            mb = mb + mask_ref[pl.ds(off, kb), :].astype(F32)
        mb_scr[...] = mb
        cbase = (j * kb - i * qb).astype(F32)
        for ch in range(RH * qb // LANE):
            r, h = divmod(ch, qb // LANE)
            sl = slice(ch * LANE, (ch + 1) * LANE)
            c = (slopes_ref[g * RH + r] * LOG2E) * cbase
            x = s_scr[:, sl] + bias_scr[:, sl] + mb_scr[:, h * LANE:(h + 1) * LANE]
            m_prev = m_scr[:, sl]
            m_new = jnp.maximum(m_prev, jnp.max(x, axis=0, keepdims=True) + c)
            p_scr[:, sl] = jnp.exp2(x - (m_new - c)).astype(BF)
            a_scr[:, sl] = jnp.exp2(m_prev - m_new)
            m_scr[:, sl] = m_new
        vt = jnp.concatenate([v_ref[pl.ds(off, kb), :].T, jnp.ones((BF16_TILE_ROWS, kb), F32)], axis=0).astype(BF)
        acc_scr[...] = acc_scr[...] * a_scr[...] + _dot(vt, p_scr[...])
        return carry

    lax.fori_loop(j_lo, j_hi + 1, body, 0)
    for r in range(RH):
        sl = slice(r * qb, (r + 1) * qb)
        o_t = jnp.where(m_scr[:, sl] > 0.5 * NEG,
                        acc_scr[0:HD, sl] / jnp.maximum(acc_scr[HD:HD + 1, sl], 1e-30), 0.0)
        o_ref[:, r * HD:(r + 1) * HD] = o_t.T


def _fa_prompt(mode, proj, slopes, bsz, t, q_col, k_col, v_col, mask=None):
    qb = min(2 * LANE, t)
    kb = min(2 * LANE, t)
    nq = t // qb
    wq = RH * HD
    in_specs = [pl.BlockSpec(memory_space=pltpu.SMEM),
                pl.BlockSpec((qb, wq), lambda b, g, i: (b * nq + i, q_col // wq + g)),
                pl.BlockSpec((t, HD), lambda b, g, i: (b, k_col // HD + g)),
                pl.BlockSpec((t, HD), lambda b, g, i: (b, v_col // HD + g))]
    args = [slopes, proj, proj, proj]
    if mode == "sel":
        in_specs.append(pl.BlockSpec((None, None, t // CMP, 8, qb), lambda b, g, i: (b, g, 0, 0, i)))
        args.append(mask)
    elif mode == "dsa":
        in_specs.append(pl.BlockSpec((None, t, qb), lambda b, g, i: (b, 0, i)))
        args.append(mask)
    kern = functools.partial(_fa_prompt_kernel, mode=mode, qb=qb, kb=kb)
    lanes = RH * qb
    return pl.pallas_call(
        kern, grid=(bsz, GROUPS, nq), in_specs=in_specs,
        out_specs=pl.BlockSpec((qb, wq), lambda b, g, i: (b * nq + i, g)),
        out_shape=jax.ShapeDtypeStruct((bsz * t, GROUPS * wq), F32),
        scratch_shapes=[pltpu.VMEM((lanes, HD), BF), pltpu.VMEM((kb, lanes), F32), pltpu.VMEM((kb, lanes), F32),
                        pltpu.VMEM((kb, lanes), BF), pltpu.VMEM((kb, qb), F32), pltpu.VMEM((1, lanes), F32),
                        pltpu.VMEM((1, lanes), F32), pltpu.VMEM((HD + BF16_TILE_ROWS, lanes), F32)],
        compiler_params=_cp(("parallel", "parallel", "arbitrary")), name="fa_prompt_" + mode)(*args)


PPS = 4
FA_PPS = 16
IDX_PPS = 16


def _pages_per_step(n_pages, want):
    return want if n_pages % want == 0 else PPS


def _fa_sample_kernel(*refs, mode, t, pps, nsteps, t0, kpos0, rpt, kcomp):
    pt_ref, slope_ref, q_ref = refs[:3]
    page_refs = refs[3:3 + pps]
    new_ref = refs[3 + pps]
    rest = refs[4 + pps:]
    if mode == "win":
        mask_ref = maskn_ref = None
        o_ref, qs_scr, m_scr, l_scr, acc_scr = rest
    else:
        mask_ref, maskn_ref, o_ref, qs_scr, m_scr, l_scr, acc_scr = rest
    j = pl.program_id(1)
    bps = pps * (PAGE // CMP)

    @pl.when(j == 0)
    def _():
        for g in range(GROUPS):
            qs_scr[g] = jnp.concatenate(
                [q_ref[:, (g * RH + r) * HD:(g * RH + r + 1) * HD] for r in range(RH)], axis=0).astype(BF)
        m_scr[...] = jnp.full(m_scr.shape, NEG, F32)
        l_scr[...] = jnp.zeros(l_scr.shape, F32)
        acc_scr[...] = jnp.zeros(acc_scr.shape, F32)

    tq = t0 + jnp.concatenate([lax.broadcasted_iota(jnp.int32, (t, 1), 0)] * RH, axis=0)

    def key_bias(mref, g, nk):
        if mode == "sel":
            n = lax.broadcasted_iota(jnp.int32, (bps, nk), 0)
            kk = lax.broadcasted_iota(jnp.int32, (bps, nk), 1)
            expand = jnp.where(n == jnp.right_shift(kk, CMP_SHIFT), 1.0, 0.0).astype(BF)
            bias = (_dot(mref[g][:, 0:bps].astype(BF), expand) - 1.0) * (-NEG)
        else:
            bias = mref[...].astype(F32)
        return jnp.tile(bias, (RH, 1))

    def step(kparts, vparts, kpos, extra_ok, mref):
        nk = kpos.shape[1]
        dist = tq - kpos
        ok = dist >= 0
        if mode == "win":
            ok = ok & (dist < WINDOW)
        if extra_ok is not None:
            ok = ok & extra_ok
        distf = dist.astype(F32)
        raw = [_dot_nt(qs_scr[g], jnp.concatenate(kparts[g], axis=0).astype(BF)) for g in range(GROUPS)]
        probs = []
        for g in range(GROUPS):
            s = jnp.where(ok, raw[g] * SCALE - slope_ref[g] * distf, NEG)
            if mref is not None:
                s = s + key_bias(mref, g, nk)
            m_prev = m_scr[g]
            m_new = jnp.maximum(m_prev, jnp.max(s, axis=-1, keepdims=True))
            a = jnp.exp(m_prev - m_new)
            p = jnp.exp(s - m_new)
            l_scr[g] = a * l_scr[g] + jnp.sum(p, axis=-1, keepdims=True)
            m_scr[g] = m_new
            probs.append((a, p.astype(BF)))
        for g in range(GROUPS):
            a, p = probs[g]
            acc_scr[g] = a * acc_scr[g] + _dot(p, jnp.concatenate(vparts[g], axis=0).astype(BF))

    lane = lax.broadcasted_iota(jnp.int32, (1, pps * PAGE), 1)
    step([[r[pl.ds(kcomp * GROUPS + g, PAGE, stride=rpt), :] for r in page_refs] for g in range(GROUPS)],
         [[r[pl.ds((kcomp + 1) * GROUPS + g, PAGE, stride=rpt), :] for r in page_refs] for g in range(GROUPS)],
         kpos0 + j * (pps * PAGE) + lane, None, mask_ref)

    @pl.when(j == nsteps - 1)
    def _():
        lane1 = lax.broadcasted_iota(jnp.int32, (1, PAGE), 1)
        kv_new = jnp.concatenate([new_ref[...], jnp.zeros((PAGE - t, KV_W), F32)], axis=0)
        step([[kv_new[:, g * HD:(g + 1) * HD]] for g in range(GROUPS)],
             [[kv_new[:, (GROUPS + g) * HD:(GROUPS + g + 1) * HD]] for g in range(GROUPS)],
             t0 + lane1, lane1 < t, maskn_ref)
        for g in range(GROUPS):
            o = jnp.where(m_scr[g] > 0.5 * NEG, acc_scr[g] / jnp.maximum(l_scr[g], 1e-30), 0.0)
            for r in range(RH):
                o_ref[:, (g * RH + r) * HD:(g * RH + r + 1) * HD] = o[r * t:(r + 1) * t]


def _fa_pps(n_pages):
    return _pages_per_step(n_pages, FA_PPS)


def _fa_sample(mode, proj, past, table, slopes, t, t0, kpos0, q_col, rpt, kcomp, new_col, mask=None):
    bsz, nkb = table.shape
    pps = _fa_pps(nkb)
    nsteps = nkb // pps
    rows = RH * t
    wq = GROUPS * RH * HD
    slope_col = jnp.repeat(slopes.reshape(GROUPS, RH), t, axis=1).reshape(GROUPS, rows, 1)
    in_specs = [pl.BlockSpec((GROUPS, rows, 1), lambda b, j, pt: (0, 0, 0)),
                pl.BlockSpec((t, wq), lambda b, j, pt: (b, q_col // wq))]
    for p in range(pps):
        in_specs.append(pl.BlockSpec((PAGE * rpt, HD), lambda b, j, pt, p=p: (pt[b, j * pps + p], 0)))
    in_specs.append(pl.BlockSpec((t, KV_W), lambda b, j, pt: (b, new_col // KV_W)))
    args = [slope_col, proj] + [past] * pps + [proj]
    if mode == "sel":
        in_specs.append(pl.BlockSpec((None, GROUPS, None, t, LANE), lambda b, j, pt: (b, 0, j, 0, 0)))
        in_specs.append(pl.BlockSpec((None, GROUPS, None, t, LANE), lambda b, j, pt: (b, 0, nsteps, 0, 0)))
        args += [mask, mask]
    elif mode == "dsa":
        in_specs.append(pl.BlockSpec((None, t, pps * PAGE), lambda b, j, pt: (b, 0, j)))
        in_specs.append(pl.BlockSpec((None, t, PAGE), lambda b, j, pt: (b, 0, nkb)))
        args += [mask, mask]
    kern = functools.partial(_fa_sample_kernel, mode=mode, t=t, pps=pps, nsteps=nsteps, t0=t0, kpos0=kpos0,
                             rpt=rpt, kcomp=kcomp)
    gs = pltpu.PrefetchScalarGridSpec(
        num_scalar_prefetch=1, grid=(bsz, nsteps), in_specs=in_specs,
        out_specs=pl.BlockSpec((t, wq), lambda b, j, pt: (b, 0)),
        scratch_shapes=[pltpu.VMEM((GROUPS, rows, HD), BF), pltpu.VMEM((GROUPS, rows, 1), F32),
                        pltpu.VMEM((GROUPS, rows, 1), F32), pltpu.VMEM((GROUPS, rows, HD), F32)])
    return pl.pallas_call(
        kern, grid_spec=gs, out_shape=jax.ShapeDtypeStruct((bsz * t, wq), F32),
        compiler_params=_cp(("parallel", "arbitrary")), name="fa_sample_" + mode)(table, *args)


IDX_CHUNK = 256


def _dsa_index_prompt_kernel(q_ref, sq_ref, sk_ref, mask_ref, qb_scr, wb_scr, kpad_scr, score_scr, key_scr,
                             *, qb, t, top):
    i = pl.program_id(1)
    tq = i * qb + lax.broadcasted_iota(jnp.int32, (qb, 1), 0)

    @pl.when(i == 0)
    def _():
        kk = sk_ref[:, 0:IDX_D]
        zero = jnp.zeros((t, IDX_D), F32)
        kpad_scr[0] = jnp.concatenate([kk, zero], axis=1).astype(BF)
        kpad_scr[1] = jnp.concatenate([zero, kk], axis=1).astype(BF)

    qb_scr[...] = (q_ref[...] * (IDX_D ** -0.5)).astype(BF)
    w = sq_ref[:, SM_IDXW:SM_IDXW + IDX_H] * (IDX_H ** -0.5)
    for h in range(IDX_H):
        wb_scr[h] = jnp.broadcast_to(w[:, h:h + 1], (qb, LANE))
    score_scr[...] = jnp.zeros(score_scr.shape, F32)
    ch = min(IDX_CHUNK * LANE // qb, t)

    def chunk(c, carry):
        off = pl.multiple_of(c * ch, ch)
        acc = jnp.zeros((qb, ch), F32)
        for pair in range(IDX_H // 2):
            qp = qb_scr[:, pair * LANE:(pair + 1) * LANE]
            for e in range(2):
                lg = _dot_nt(qp, kpad_scr[e, pl.ds(off, ch), :])
                acc = acc + jnp.tile(wb_scr[2 * pair + e], (1, ch // LANE)) * jnp.maximum(lg, 0.0)
        score_scr[:, pl.ds(off, ch)] = acc
        return carry

    lax.fori_loop(0, ((i + 1) * qb + ch - 1) // ch, chunk, 0)
    s_pos = lax.broadcasted_iota(jnp.int32, (1, t), 1)
    causal = s_pos <= tq
    key_scr[...] = _sortable(jnp.where(causal, score_scr[...], NEG))
    sel = _topk_mask(key_scr, top, s_pos, max(1, (t - 1).bit_length()))
    bias = jnp.where(sel & causal, 0.0, NEG)
    for c in range(t // LANE):
        mask_ref[c * LANE:(c + 1) * LANE, :] = bias[:, c * LANE:(c + 1) * LANE].T.astype(BF)


def _dsa_index_prompt(proj, sm, bsz, t):
    qb = min(2 * LANE, t)
    nq = t // qb
    top = min(IDX_TOPK, t // 4)
    kern = functools.partial(_dsa_index_prompt_kernel, qb=qb, t=t, top=top)
    wq = IDX_H * IDX_D
    return pl.pallas_call(
        kern, grid=(bsz, nq),
        in_specs=[pl.BlockSpec((qb, wq), lambda b, i: (b * nq + i, C_QI // wq)),
                  pl.BlockSpec((qb, LANE), lambda b, i: (b * nq + i, 0)),
                  pl.BlockSpec((t, LANE), lambda b, i: (b, 0))],
        out_specs=pl.BlockSpec((None, t, qb), lambda b, i: (b, 0, i)),
        out_shape=jax.ShapeDtypeStruct((bsz, t, t), BF),
        scratch_shapes=[pltpu.VMEM((qb, wq), BF), pltpu.VMEM((IDX_H, qb, LANE), F32),
                        pltpu.VMEM((2, t, 2 * IDX_D), BF), pltpu.VMEM((qb, t), F32), pltpu.VMEM((qb, t), jnp.int32)],
        compiler_params=_cp(("parallel", "arbitrary")), name="dsa_index_prompt")(proj, sm, sm)


def _dsa_index_sample_kernel(pt_ref, qf_ref, wcol_ref, *refs, t, pps, nsteps, t0, lpad, top):
    page_refs = refs[:pps]
    sn_ref, mask_ref, score_scr, key_scr = refs[pps:]
    j = pl.program_id(1)

    @pl.when(j == 0)
    def _():
        score_scr[...] = jnp.full(score_scr.shape, NEG, F32)

    qf = qf_ref[...].astype(BF)
    wcol = wcol_ref[...] * (IDX_H ** -0.5)

    def scores(logits):
        r = jnp.maximum(logits * (IDX_D ** -0.5), 0.0) * wcol
        sc = r[0:t]
        for h in range(1, IDX_H):
            sc = sc + r[h * t:(h + 1) * t]
        return sc

    kpages_t = jnp.concatenate([r[...] for r in page_refs], axis=1).astype(BF)
    score_scr[:, pl.ds(pl.multiple_of(j * (pps * PAGE), pps * PAGE), pps * PAGE)] = scores(_dot(qf, kpages_t))

    @pl.when(j == nsteps - 1)
    def _():
        tq = t0 + lax.broadcasted_iota(jnp.int32, (t, 1), 0)
        lane = lax.broadcasted_iota(jnp.int32, (1, PAGE), 1)
        kn = jnp.concatenate([sn_ref[:, 0:IDX_D], jnp.zeros((PAGE - t, IDX_D), F32)], axis=0).astype(BF)
        ok = (lane < t) & (t0 + lane <= tq)
        score_scr[:, lpad - PAGE:lpad] = jnp.where(ok, scores(_dot_nt(qf, kn)), NEG)
        s_pos = lax.broadcasted_iota(jnp.int32, (1, lpad), 1)
        key_scr[...] = _sortable(score_scr[...])
        sel = _topk_mask(key_scr, top, s_pos, max(1, (lpad - 1).bit_length()))
        mask_ref[...] = jnp.where(sel & (s_pos <= tq) & (s_pos < t0 + t), 0.0, NEG).astype(BF)


def _dsa_index_sample(qf, wcol, idx_t, table, sm, t, t0):
    bsz, nkb = table.shape
    pps = _pages_per_step(nkb, IDX_PPS)
    nsteps = nkb // pps
    lpad = (nkb + 1) * PAGE
    top = min(IDX_TOPK, (t0 + t) // 4)
    rows = IDX_H * t
    kern = functools.partial(_dsa_index_sample_kernel, t=t, pps=pps, nsteps=nsteps, t0=t0, lpad=lpad, top=top)
    gs = pltpu.PrefetchScalarGridSpec(
        num_scalar_prefetch=1, grid=(bsz, nsteps),
        in_specs=[pl.BlockSpec((rows, IDX_D), lambda b, j, pt: (b, 0)),
                  pl.BlockSpec((rows, 1), lambda b, j, pt: (b, 0))]
        + [pl.BlockSpec((IDX_D, PAGE), lambda b, j, pt, p=p: (pt[b, j * pps + p], 0)) for p in range(pps)]
        + [pl.BlockSpec((t, LANE), lambda b, j, pt: (b, 0))],
        out_specs=pl.BlockSpec((None, t, lpad), lambda b, j, pt: (b, 0, 0)),
        scratch_shapes=[pltpu.VMEM((t, lpad), F32), pltpu.VMEM((t, lpad), jnp.int32)])
    return pl.pallas_call(
        kern, grid_spec=gs, out_shape=jax.ShapeDtypeStruct((bsz, t, lpad), BF),
        compiler_params=_cp(("parallel", "arbitrary")), name="dsa_index_sample")(
            table, qf, wcol, *([idx_t] * pps), sm)


def _silu(z):
    return z * jax.nn.sigmoid(z)


def _combine_nsa_kernel(oc_ref, os_ref, ow_ref, gate_ref, z_ref, o_ref):
    gate = jax.nn.sigmoid(gate_ref[:, 0:3 * NSA_H])
    for h in range(NSA_H):
        sl = slice(h * HD, (h + 1) * HD)
        o = (gate[:, 3 * h:3 * h + 1] * oc_ref[:, sl] + gate[:, 3 * h + 1:3 * h + 2] * os_ref[:, sl]
             + gate[:, 3 * h + 2:3 * h + 3] * ow_ref[:, sl])
        o_ref[:, sl] = (o * _silu(z_ref[:, sl])).astype(BF)


def _combine_nsa(o_cmp, o_sel, o_win, proj, sm, tm):
    m, n = o_cmp.shape
    row = pl.BlockSpec((tm, n), lambda i: (i, 0))
    return pl.pallas_call(
        _combine_nsa_kernel, grid=(m // tm,),
        in_specs=[row, row, row, pl.BlockSpec((tm, LANE), lambda i: (i, 1)),
                  pl.BlockSpec((tm, n), lambda i: (i, C_ZN // n))],
        out_specs=row, out_shape=jax.ShapeDtypeStruct((m, n), BF),
        compiler_params=_cp(("parallel",)), name="combine_nsa")(o_cmp, o_sel, o_win, sm, proj)


def _combine_dsa_kernel(o_ref_in, z_ref, o_ref):
    o_ref[...] = (o_ref_in[...] * _silu(z_ref[...])).astype(BF)


def _combine_dsa(o, proj, tm):
    m, n = o.shape
    row = pl.BlockSpec((tm, n), lambda i: (i, 0))
    return pl.pallas_call(
        _combine_dsa_kernel, grid=(m // tm,),
        in_specs=[row, pl.BlockSpec((tm, n), lambda i: (i, C_ZD // n))],
        out_specs=row, out_shape=jax.ShapeDtypeStruct((m, n), BF),
        compiler_params=_cp(("parallel",)), name="combine_dsa")(o, proj)


PREP_TN = 512
PREP_TK = 1024


def _prep_tables():
    src, off = {}, 0
    for name, width in IN_SPLITS:
        src[name] = off
        off += width
    width = dict(IN_SPLITS)
    shifts, base, cls = [], [], []
    for name, _ in PROJ_LAYOUT:
        for c in range(width[name] // PREP_TN):
            s = src[name] + c * PREP_TN
            if s % LANE not in shifts:
                shifts.append(s % LANE)
            base.append(s // LANE)
            cls.append(shifts.index(s % LANE))
    return tuple(shifts), base, cls


def _prep_kernel(base_ref, cls_ref, *refs, shifts):
    o_ref = refs[-1]
    j = pl.program_id(1)
    for k, s in enumerate(shifts):
        @pl.when(cls_ref[j] == k)
        def _():
            win = jnp.concatenate([r[...] for r in refs[:-1]], axis=0)
            o_ref[...] = win[s:s + PREP_TN, :].T.astype(BF)


def _prep_w_in(w_in_t, l):
    shifts, base, cls = _prep_tables()
    assert all(sh % 8 == 0 for sh in shifts)
    k = w_in_t.shape[2]
    nwin = PREP_TN // LANE + 1
    kern = functools.partial(_prep_kernel, shifts=shifts)
    gs = pltpu.PrefetchScalarGridSpec(
        num_scalar_prefetch=2, grid=(k // PREP_TK, len(base)),
        in_specs=[pl.BlockSpec((None, LANE, PREP_TK), lambda i, j, bs, cs, m=m: (l, bs[j] + m, i))
                  for m in range(nwin)],
        out_specs=pl.BlockSpec((PREP_TK, PREP_TN), lambda i, j, bs, cs: (i, j)))
    return pl.pallas_call(
        kern, grid_spec=gs, out_shape=jax.ShapeDtypeStruct((k, PROJ_W), BF),
        compiler_params=_cp(("parallel", "parallel")), name="prep_w_in")(
            jnp.asarray(base, jnp.int32), jnp.asarray(cls, jnp.int32), *([w_in_t] * nwin))


def _prep_small_kernel(g_ref, a_ref, b_ref, o_ref, *, gate_off, w_off, k_off):
    tk = o_ref.shape[0]
    ab = jnp.concatenate([a_ref[...], b_ref[...]], axis=0)
    rows = jnp.concatenate([ab[k_off:k_off + IDX_D], ab[w_off:w_off + IDX_H], jnp.zeros((32, tk), F32),
                            g_ref[gate_off:gate_off + 3 * NSA_H], jnp.zeros((SM_W - 176, tk), F32)], axis=0)
    o_ref[...] = rows.T.astype(BF)


def _prep_w_small(w_in_t, l):
    src, off = {}, 0
    for name, width in IN_SPLITS:
        src[name] = off
        off += width
    gate_blk, ab_blk = src['nsa_gate'] // LANE, src['idx_w'] // LANE
    offs = dict(gate_off=src['nsa_gate'] - gate_blk * LANE, w_off=src['idx_w'] - ab_blk * LANE,
                k_off=src['idx_k'] - ab_blk * LANE)
    assert all(v % 8 == 0 for v in offs.values()) and offs['k_off'] + IDX_D <= 2 * LANE
    k = w_in_t.shape[2]
    kern = functools.partial(_prep_small_kernel, **offs)
    return pl.pallas_call(
        kern, grid=(k // PREP_TK,),
        in_specs=[pl.BlockSpec((None, LANE, PREP_TK), lambda i, blk=blk: (l, blk, i))
                  for blk in (gate_blk, ab_blk, ab_blk + 1)],
        out_specs=pl.BlockSpec((PREP_TK, SM_W), lambda i: (i, 0)),
        out_shape=jax.ShapeDtypeStruct((k, SM_W), BF),
        compiler_params=_cp(("parallel",)), name="prep_w_small")(w_in_t, w_in_t, w_in_t)


def _slopes(n):
    return jnp.exp2(-8.0 * jnp.arange(1, n + 1, dtype=F32) / n)


def _tail(proj, xf, wts, a_in, b_in, tm_mm, tm_ln):
    hm = _gated_matmul(a_in, b_in, wts['pa'], wts['pd'], proj, tm_mm, 512, wts['l'])
    h = _matmul(hm, wts['out'], tm_mm, 1024, l=wts['l'])
    return _residual_ln(xf, h, wts['gain'], wts['bias'], tm_ln)


def _layer_prompt(xf, xb, bsz, t, wts):
    tm = min(1024, bsz * t)
    proj = _matmul(xb, wts['in'], tm, 1024)
    sm = _matmul(xb, wts['in_small'], tm, SM_W)
    n_pages = t // PAGE
    table = jnp.arange(bsz * n_pages, dtype=jnp.int32).reshape(bsz, n_pages)
    kcvc = _compress(proj, table, C_KVN // (4 * HD), 0, wts['pe_page'], wts['w1'], wts['w2'], n_pages, 1)
    sl_n, sl_d = _slopes(NSA_H), _slopes(DSA_H)
    o_cmp, selm = _nsa_select(proj, kcvc, sl_n, bsz, t, t // CMP, 0, 0)
    o_sel = _fa_prompt("sel", proj, sl_n, bsz, t, C_QN, C_KVN + KV_W, C_KVN + KV_W + GROUPS * HD, selm)
    o_win = _fa_prompt("win", proj, sl_n, bsz, t, C_QN, C_KVN + 2 * KV_W, C_KVN + 2 * KV_W + GROUPS * HD)
    dmask = _dsa_index_prompt(proj, sm, bsz, t)
    o_dsa = _fa_prompt("dsa", proj, sl_d, bsz, t, C_QD, C_KVD, C_KVD + GROUPS * HD, dmask)
    tme = min(256, bsz * t)
    a_in = _combine_nsa(o_cmp, o_sel, o_win, proj, sm, tme)
    b_in = _combine_dsa(o_dsa, proj, tme)
    y, yb = _tail(proj, xf, wts, a_in, b_in, tm, tme)
    return y, yb, proj, sm


def _layer_sample(xf, xb, bsz, t, wts, nsa_rows, win_rows, dsa_rows, idx_t, page_table, win_table):
    rows = bsz * t
    proj = _matmul(xb, wts['in'], rows, 512)
    sm = _matmul(xb, wts['in_small'], rows, SM_W)
    n_pages = page_table.shape[1]
    t0 = n_pages * PAGE
    wbuf = win_table.shape[1] * PAGE
    kcvc = _compress(nsa_rows, page_table, 0, 4 * GROUPS, wts['pe_page'], wts['w1'], wts['w2'], min(64, n_pages),
                     _pages_per_step(min(64, n_pages), FA_PPS))
    sl_n, sl_d = _slopes(NSA_H), _slopes(DSA_H)
    o_cmp, selm = _nsa_select(proj, kcvc, sl_n, bsz, t, -(-(t0 + t) // CMP), t0, n_pages)
    o_sel = _fa_sample("sel", proj, nsa_rows, page_table, sl_n, t, t0, 0, C_QN, 4 * GROUPS, 2, C_KVN + KV_W, selm)
    o_win = _fa_sample("win", proj, win_rows, win_table, sl_n, t, t0, t0 - wbuf, C_QN, 2 * GROUPS, 0,
                       C_KVN + 2 * KV_W)
    qi = proj[:, C_QI:C_QI + IDX_H * IDX_D].reshape(bsz, t, IDX_H, IDX_D).transpose(0, 2, 1, 3)
    qf = qi.reshape(bsz * IDX_H * t, IDX_D)
    wi = sm[:, SM_IDXW:SM_IDXW + IDX_H].reshape(bsz, t, IDX_H).transpose(0, 2, 1)
    wcol = wi.reshape(bsz * IDX_H * t, 1)
    dmask = _dsa_index_sample(qf, wcol, idx_t, page_table, sm, t, t0)
    o_dsa = _fa_sample("dsa", proj, dsa_rows, page_table, sl_d, t, t0, 0, C_QD, 2 * GROUPS, 0, C_KVD, dmask)
    a_in = _combine_nsa(o_cmp, o_sel, o_win, proj, sm, rows)
    b_in = _combine_dsa(o_dsa, proj, rows)
    y, yb = _tail(proj, xf, wts, a_in, b_in, rows, rows)
    return y, yb, proj, sm


def _layer_weights(l, w_in, cmp_pe, cmp_w1, cmp_w2, w_proj_nsa, w_proj_dsa, w_out, ln_gain, ln_bias):
    cmp_pe, cmp_w1, cmp_w2, ln_gain, ln_bias = (a[l] for a in (cmp_pe, cmp_w1, cmp_w2, ln_gain, ln_bias))
    pe = jnp.concatenate([cmp_pe, cmp_pe], axis=1)
    pe_page = jnp.concatenate([pe[0], pe[0], pe[1], pe[1]], axis=1)
    w_in_t = jnp.swapaxes(w_in, 1, 2)
    return {'in': _prep_w_in(w_in_t, l), 'in_small': _prep_w_small(w_in_t, l), 'pe_page': pe_page,
            'w1': cmp_w1.astype(BF), 'w2': cmp_w2.astype(BF),
            'pa': w_proj_nsa.astype(BF), 'pd': w_proj_dsa.astype(BF), 'out': w_out.astype(BF),
            'l': l, 'gain': ln_gain, 'bias': ln_bias}


STATE_TM = 256


def _state_kernel(x_ref, *refs, tm):
    nsa_ref, dsa_ref = refs[-2:]
    for cg in range(4 * GROUPS):
        nsa_ref[pl.ds(cg, tm, stride=4 * GROUPS), :] = x_ref[:, cg * HD:(cg + 1) * HD]
    for cg in range(2 * GROUPS):
        dsa_ref[pl.ds(cg, tm, stride=2 * GROUPS), :] = x_ref[:, 3 * KV_W + cg * HD:3 * KV_W + (cg + 1) * HD]


def _write_state(proj, l, prev):
    m = proj.shape[0]
    tm = min(STATE_TM, m)
    nt = m // tm
    kern = functools.partial(_state_kernel, tm=tm)
    width = 4 * KV_W
    in_specs = [pl.BlockSpec((tm, width), lambda i: (i, C_KVN // width))]
    args = [proj]
    aliases = {}
    if prev is not None:
        in_specs += [pl.BlockSpec(memory_space=pl.ANY), pl.BlockSpec(memory_space=pl.ANY)]
        args += list(prev)
        aliases = {1: 0, 2: 1}
    return pl.pallas_call(
        kern, grid=(nt,), in_specs=in_specs,
        out_specs=[pl.BlockSpec((tm * 4 * GROUPS, HD), lambda i: (l * nt + i, 0)),
                   pl.BlockSpec((tm * 2 * GROUPS, HD), lambda i: (l * nt + i, 0))],
        out_shape=[jax.ShapeDtypeStruct((DEPTH * m * 4 * GROUPS, HD), F32),
                   jax.ShapeDtypeStruct((DEPTH * m * 2 * GROUPS, HD), F32)],
        input_output_aliases=aliases,
        compiler_params=_cp(("arbitrary",)), name="write_state")(*args)


def _win_and_idx(proj, sm, bsz, t):
    win = proj[:, C_KVN + 2 * KV_W:C_KVN + 3 * KV_W].reshape(bsz, t, 2, GROUPS, HD)
    idx_k = sm[:, 0:IDX_D].reshape(bsz, t, IDX_D)
    return win, idx_k


def kernel(x_prompt, x_sample, cache_nsa_kv, state_nsa_win, cache_dsa_kv, cache_dsa_idx, page_table,
           w_in, cmp_pe, cmp_w1, cmp_w2, w_proj_nsa, w_proj_dsa, w_out, ln_gain, ln_bias):
    bp, tp, _ = x_prompt.shape
    bs, ts, _ = x_sample.shape
    n_pool = cache_nsa_kv.shape[1]
    wpages = state_nsa_win.shape[2] // PAGE
    nsa_rows = cache_nsa_kv.reshape(-1, HD)
    dsa_rows = cache_dsa_kv.reshape(-1, HD)
    idx_t = jnp.swapaxes(cache_dsa_idx, 2, 3).reshape(-1, PAGE)
    win_rows = state_nsa_win.reshape(-1, HD)
    win_table = jnp.arange(bs * wpages, dtype=jnp.int32).reshape(bs, wpages)
    yp, ys = x_prompt.reshape(bp * tp, D_MODEL), x_sample.reshape(bs * ts, D_MODEL)
    ypb, ysb = yp.astype(BF), ys.astype(BF)
    outs = [[] for _ in range(4)]
    state_p = state_s = None
    for l in range(DEPTH):
        wts = _layer_weights(l, w_in, cmp_pe, cmp_w1, cmp_w2, w_proj_nsa, w_proj_dsa, w_out, ln_gain, ln_bias)
        yp, ypb, proj_p, sm_p = _layer_prompt(yp, ypb, bp, tp, wts)
        ys, ysb, proj_s, sm_s = _layer_sample(ys, ysb, bs, ts, wts, nsa_rows, win_rows, dsa_rows, idx_t,
                                              page_table + l * n_pool, win_table + l * bs * wpages)
        state_p = _write_state(proj_p, l, state_p)
        state_s = _write_state(proj_s, l, state_s)
        win_p, idx_p = _win_and_idx(proj_p, sm_p, bp, tp)
        win_s, idx_s = _win_and_idx(proj_s, sm_s, bs, ts)
        win_all = jnp.concatenate([state_nsa_win[l], win_s], axis=1)
        keep_p, keep_s = min(WINDOW, tp), min(WINDOW, win_all.shape[1])
        for lst, val in zip(outs, (win_p[:, tp - keep_p:], win_all[:, win_all.shape[1] - keep_s:], idx_p, idx_s)):
            lst.append(val)
    win_po, win_so, idx_po, idx_so = (jnp.stack(o) for o in outs)
    return (yp.reshape(bp, tp, D_MODEL), ys.reshape(bs, ts, D_MODEL),
            state_p[0].reshape(DEPTH, bp, tp, 4, GROUPS, HD), state_s[0].reshape(DEPTH, bs, ts, 4, GROUPS, HD),
            win_po, win_so,
            state_p[1].reshape(DEPTH, bp, tp, 2, GROUPS, HD), state_s[1].reshape(DEPTH, bs, ts, 2, GROUPS, HD),
            idx_po, idx_so)
```
